```python
import jax, jax.numpy as jnp
from jax import lax
import numpy as np

D_MODEL = 1024
BATCH = 8
SEQ = 4096
DEPTH = 2
DEC_BATCH = 32
DEC_SEQ = 4
PAST_LEN = 16384
PAGE_SIZE = 128

N_ATT_LAYERS = (DEPTH + 1) // 2
N_CONV_LAYERS = DEPTH // 2
POOL_WIDTH = D_MODEL // 2
POOL_WINDOWS = (2, 4, 8, 16)
POOL_GROUPS = len(POOL_WINDOWS)
POOL_GROUP_DIM = POOL_WIDTH // POOL_GROUPS
POOL_HIST = max(POOL_WINDOWS) - 1
N_HEADS = 8
HEAD_DIM = 64
N_KV = 2
GROUP_R = N_HEADS // N_KV
ATT_WIDTH = N_HEADS * HEAD_DIM
KV_WIDTH = N_KV * HEAD_DIM
CMP_BLOCK = 32
CMP_STRIDE = 16
CMP_HIDDEN = 128
SEL_BLOCK = 64
N_SEL = 16
WINDOW = 512
N_KV_SLOTS = 4
IN0_SIZES = [POOL_WIDTH, ATT_WIDTH] + [KV_WIDTH] * 6 + [3 * N_HEADS]
IN0_WIDTH = sum(IN0_SIZES)
Q_BLOCK = 64
ROPE_THETA = 10000.0
CONV_WIDTH = 31
CONV_HIST = CONV_WIDTH - 1
D_FF = -(-(8 * D_MODEL) // (3 * 256)) * 256
NORM_EPS = 1e-6

kernel_name = "hybrid_pool_nsa_conformer_decoder_step"


def rms_norm(x, g):
    xf = x.astype(jnp.float32)
    y = xf * lax.rsqrt(jnp.mean(xf * xf, axis=-1, keepdims=True) + NORM_EPS)
    return (y * g.astype(jnp.float32)).astype(x.dtype)


def layer_norm(x, g, b):
    xf = x.astype(jnp.float32)
    mu = jnp.mean(xf, axis=-1, keepdims=True)
    var = jnp.mean(jnp.square(xf - mu), axis=-1, keepdims=True)
    y = (xf - mu) * lax.rsqrt(var + NORM_EPS)
    return (y * g.astype(jnp.float32) + b.astype(jnp.float32)).astype(x.dtype)


def ada_mod(c, w, b):
    m = jax.nn.silu(c) @ w + b
    return [t[:, None, :] for t in jnp.split(m, 6, axis=-1)]


def modulate(x, g, shift, scale):
    return rms_norm(x, g) * (1 + scale) + shift


def rope(x, pos):
    half = HEAD_DIM // 2
    inv = ROPE_THETA ** (-jnp.arange(half, dtype=jnp.float32) / half)
    ang = pos.astype(jnp.float32)[:, None] * inv[None, :]
    cos = jnp.cos(ang)[None, :, None, :]
    sin = jnp.sin(ang)[None, :, None, :]
    xf = x.astype(jnp.float32)
    x1, x2 = xf[..., :half], xf[..., half:]
    return jnp.concatenate([x1 * cos - x2 * sin, x1 * sin + x2 * cos], axis=-1).astype(x.dtype)


def masked_softmax(s, mask, axis):
    s = jnp.where(mask, s, -jnp.inf)
    m = jnp.max(s, axis=axis, keepdims=True)
    m = jnp.where(jnp.isfinite(m), m, 0.0)
    e = jnp.exp(s - m)
    d = jnp.sum(e, axis=axis, keepdims=True)
    return e / jnp.where(d > 0, d, 1.0)


def swiglu(h, w1, w2):
    g, u = jnp.split(h @ w1, 2, axis=-1)
    return (jax.nn.silu(g) * u) @ w2


def pool_mix(u, hist, pos0, w_grp, scale):
    B, T, C = u.shape
    full = jnp.concatenate([hist.astype(u.dtype), u], axis=1).astype(jnp.float32)
    cs = jnp.concatenate([jnp.zeros((B, 1, C), jnp.float32), jnp.cumsum(full, axis=1)], axis=1)
    upto = cs[:, POOL_HIST + 1:]
    pos = pos0 + jnp.arange(T)
    means = []
    for g, w in enumerate(POOL_WINDOWS):
        ch = slice(g * POOL_GROUP_DIM, (g + 1) * POOL_GROUP_DIM)
        before = cs[:, POOL_HIST + 1 - w: POOL_HIST + 1 - w + T, ch]
        cnt = jnp.minimum(pos + 1, w).astype(jnp.float32)[None, :, None]
        means.append((upto[..., ch] - before) / cnt)
    d = (jnp.concatenate(means, axis=-1) - full[:, POOL_HIST:]).astype(u.dtype)
    d = d.reshape(B, T, POOL_GROUPS, POOL_GROUP_DIM)
    y = jnp.einsum('btgc,gce->btge', d, w_grp).reshape(B, T, C)
    return y * scale


def compress(k, pos_emb, w1, w2):
    B, L, G, Dh = k.shape
    r = CMP_BLOCK // CMP_STRIDE
    nch = L // CMP_STRIDE
    n = nch - r + 1
    ch = k.reshape(B, nch, CMP_STRIDE, G, Dh)
    blk = jnp.concatenate([ch[:, o:o + n] for o in range(r)], axis=2)
    blk = blk + pos_emb[None, None, :, None, :].astype(k.dtype)
    flat = blk.transpose(0, 1, 3, 2, 4).reshape(B, n, G, CMP_BLOCK * Dh)
    return jax.nn.gelu(flat @ w1) @ w2


def sel_blocks(k):
    B, L, G, Dh = k.shape
    return k.reshape(B, L // SEL_BLOCK, SEL_BLOCK, G, Dh).transpose(0, 3, 1, 2, 4)


def nsa_block(q, qpos, kc, vc, ksb, vsb, kw, vw, kwpos, gates):
    B, T = q.shape[:2]
    scale = HEAD_DIM ** -0.5
    qg = q.reshape(B, T, N_KV, GROUP_R, HEAD_DIM)
    nc = kc.shape[1]
    s_c = jnp.einsum('btgrd,bngd->btgrn', qg, kc, preferred_element_type=jnp.float32) * scale
    c_end = jnp.arange(nc) * CMP_STRIDE + CMP_BLOCK - 1
    mask_c = (c_end[None, :] <= qpos[:, None])[None, :, None, None, :]
    p_c = masked_softmax(s_c, mask_c, -1)
    o_c = jnp.einsum('btgrn,bngd->btgrd', p_c.astype(vc.dtype), vc).astype(jnp.float32)
    ns = ksb.shape[2]
    per = SEL_BLOCK // CMP_STRIDE
    pad = CMP_BLOCK // CMP_STRIDE - 1
    imp = jnp.pad(jnp.sum(p_c, axis=3), ((0, 0), (0, 0), (0, 0), (pad, pad)))
    imp = sum(imp[..., o: o + per * ns: per] for o in range(per + pad))
    jb = jnp.arange(ns)[None, :]
    cur = (qpos // SEL_BLOCK)[:, None]
    valid = jb * SEL_BLOCK <= qpos[:, None]
    forced = (jb == 0) | (jb == cur) | (jb == cur - 1)
    score = jnp.where(valid[None, :, None, :], jnp.where(forced[None, :, None, :], jnp.inf, imp), -jnp.inf)
    _, idx = lax.top_k(score, min(N_SEL, ns))
    bi = jnp.arange(B)[:, None, None, None]
    gi = jnp.arange(N_KV)[None, None, :, None]
    k_g = ksb[bi, gi, idx]
    v_g = vsb[bi, gi, idx]
    s_s = jnp.einsum('btgrd,btgkld->btgrkl', qg, k_g, preferred_element_type=jnp.float32) * scale
    kpos = idx[..., None] * SEL_BLOCK + jnp.arange(SEL_BLOCK)
    mask_s = (kpos <= qpos[None, :, None, None, None])[:, :, :, None]
    p_s = masked_softmax(s_s, mask_s, (-2, -1))
    o_s = jnp.einsum('btgrkl,btgkld->btgrd', p_s.astype(v_g.dtype), v_g).astype(jnp.float32)
    s_w = jnp.einsum('btgrd,blgd->btgrl', qg, kw, preferred_element_type=jnp.float32) * scale
    dpos = qpos[:, None] - kwpos[None, :]
    mask_w = ((dpos >= 0) & (dpos < WINDOW) & (kwpos[None, :] >= 0))[None, :, None, None, :]
    p_w = masked_softmax(s_w, mask_w, -1)
    o_w = jnp.einsum('btgrl,blgd->btgrd', p_w.astype(vw.dtype), vw).astype(jnp.float32)
    g = gates.reshape(B, T, N_KV, GROUP_R, 3).astype(jnp.float32)
    o = o_c * g[..., 0:1] + o_s * g[..., 1:2] + o_w * g[..., 2:3]
    return o.astype(q.dtype).reshape(B, T, ATT_WIDTH)


def project_ab(h, pos, w_in):
    B, T, _ = h.shape
    cuts = np.cumsum(IN0_SIZES)[:-1].tolist()
    u_pool, q, kc, vc, ks, vs, kw, vw, g = jnp.split(h @ w_in, cuts, axis=-1)
    q = rope(q.reshape(B, T, N_HEADS, HEAD_DIM), pos)
    hk = lambda t: t.reshape(B, T, N_KV, HEAD_DIM)
    kv = [rope(hk(kc), pos), hk(vc), rope(hk(ks), pos), hk(vs)]
    win = [rope(hk(kw), pos), hk(vw)]
    gates = jax.nn.sigmoid(g.astype(jnp.float32)).astype(h.dtype).reshape(B, T, N_HEADS, 3)
    return u_pool, q, kv, win, gates


def mixer_ab_prompt(h, wb, w_in, w_out, pool_w, pool_scale, cmp_pos, cmp_w1, cmp_w2):
    B, S, _ = h.shape
    pos = jnp.arange(S)
    u_pool, q, kv, win, gates = project_ab(h, pos, w_in)
    hist0 = jnp.zeros((B, POOL_HIST, POOL_WIDTH), h.dtype)
    y_pool = pool_mix(u_pool, hist0, 0, pool_w, pool_scale)
    pool_state = jnp.concatenate([hist0, u_pool], axis=1)[:, -POOL_HIST:]
    kc = compress(kv[0], cmp_pos[0], cmp_w1[0], cmp_w2[0])
    vc = compress(kv[1], cmp_pos[1], cmp_w1[1], cmp_w2[1])
    ksb, vsb = sel_blocks(kv[2]), sel_blocks(kv[3])
    kw_pad = jnp.pad(win[0], ((0, 0), (WINDOW, 0), (0, 0), (0, 0)))
    vw_pad = jnp.pad(win[1], ((0, 0), (WINDOW, 0), (0, 0), (0, 0)))
    nb = S // Q_BLOCK
    band = WINDOW + Q_BLOCK
    qb = q.reshape(B, nb, Q_BLOCK, N_HEADS, HEAD_DIM).swapaxes(0, 1)
    gb = gates.reshape(B, nb, Q_BLOCK, N_HEADS, 3).swapaxes(0, 1)

    def step(args):
        q_i, g_i, i = args
        s0 = i * Q_BLOCK
        qpos = s0 + jnp.arange(Q_BLOCK)
        kw = lax.dynamic_slice_in_dim(kw_pad, s0, band, axis=1)
        vw = lax.dynamic_slice_in_dim(vw_pad, s0, band, axis=1)
        kwpos = s0 - WINDOW + jnp.arange(band)
        return nsa_block(q_i, qpos, kc, vc, ksb, vsb, kw, vw, kwpos, g_i)

    o = lax.map(step, (qb, gb, jnp.arange(nb)))
    o = o.swapaxes(0, 1).reshape(B, S, ATT_WIDTH)
    y = jnp.concatenate([y_pool, o], axis=-1) @ w_out
    kv_rows = jnp.stack(kv, axis=0)
    win_rows = jnp.stack(win, axis=0)
    if S >= wb:
        win_state = win_rows[:, :, S - wb:]
    else:
        win_state = jnp.pad(win_rows, ((0, 0), (0, 0), (wb - S, 0), (0, 0), (0, 0)))
    return y, kv_rows, win_state, pool_state


def mixer_ab_sample(h, kv_cache, page_table, win_cache, pool_hist, w_in, w_out, pool_w, pool_scale,
                    cmp_pos, cmp_w1, cmp_w2):
    B, T, _ = h.shape
    P = page_table.shape[1] * PAGE_SIZE
    wb = win_cache.shape[2]
    pos = P + jnp.arange(T)
    u_pool, q, kv, win, gates = project_ab(h, pos, w_in)
    y_pool = pool_mix(u_pool, pool_hist, P, pool_w, pool_scale)
    pool_state = jnp.concatenate([pool_hist.astype(u_pool.dtype), u_pool], axis=1)[:, -POOL_HIST:]
    L = P + T
    Lp = -(-L // SEL_BLOCK) * SEL_BLOCK

    def full_rows(slot):
        past = kv_cache[slot][page_table].reshape(B, P, N_KV, HEAD_DIM)
        tail = jnp.zeros((B, Lp - L, N_KV, HEAD_DIM), past.dtype)
        return jnp.concatenate([past, kv[slot].astype(past.dtype), tail], axis=1)

    kc = compress(full_rows(0), cmp_pos[0], cmp_w1[0], cmp_w2[0])
    vc = compress(full_rows(1), cmp_pos[1], cmp_w1[1], cmp_w2[1])
    ksb, vsb = sel_blocks(full_rows(2)), sel_blocks(full_rows(3))
    kw = jnp.concatenate([win_cache[0], win[0].astype(win_cache.dtype)], axis=1)
    vw = jnp.concatenate([win_cache[1], win[1].astype(win_cache.dtype)], axis=1)
    kwpos = P - wb + jnp.arange(wb + T)
    o = nsa_block(q, pos, kc, vc, ksb, vsb, kw, vw, kwpos, gates)
    y = jnp.concatenate([y_pool, o], axis=-1) @ w_out
    kv_rows = jnp.stack(kv, axis=0)
    win_state = jnp.stack([kw[:, -wb:], vw[:, -wb:]], axis=0)
    return y, kv_rows, win_state, pool_state


def conv_mix(h, hist, pw1, dw, dw_b, ln_g, ln_b, pw2):
    a, b = jnp.split(h @ pw1, 2, axis=-1)
    u = a * jax.nn.sigmoid(b)
    full = jnp.concatenate([hist.astype(u.dtype), u], axis=1)
    y = lax.conv_general_dilated(full, dw[:, None, :].astype(full.dtype), window_strides=(1,), padding='VALID',
                                 dimension_numbers=('NWC', 'WIO', 'NWC'), feature_group_count=full.shape[-1])
    y = jax.nn.silu(layer_norm(y + dw_b, ln_g, ln_b))
    return y @ pw2, full[:, -CONV_HIST:]


def setup_inputs(seed: int = 0) -> dict:
    key = jax.random.key(seed)
    k = jax.random.split(key, 28)
    nrm = lambda i, shape, s: jax.random.normal(k[i], shape, jnp.float32) * s
    n_pages = PAST_LEN // PAGE_SIZE
    n_phys = (DEC_BATCH * n_pages * 5 + 3) // 4
    wb = min(WINDOW, PAST_LEN)
    perm = jax.random.permutation(k[3], n_phys)[:DEC_BATCH * n_pages]
    return {
        "x_prompt": nrm(0, (BATCH, SEQ, D_MODEL), 1.0),
        "x_sample": nrm(1, (DEC_BATCH, DEC_SEQ, D_MODEL), 1.0),
        "cache_kv": nrm(2, (N_ATT_LAYERS, N_KV_SLOTS, n_phys, PAGE_SIZE, N_KV, HEAD_DIM), 1.0),
        "page_table": perm.reshape(DEC_BATCH, n_pages).astype(jnp.int32),
        "cache_win": nrm(4, (N_ATT_LAYERS, 2, DEC_BATCH, wb, N_KV, HEAD_DIM), 1.0),
        "state_pool": nrm(5, (N_ATT_LAYERS, DEC_BATCH, POOL_HIST, POOL_WIDTH), 1.0),
        "state_conv": nrm(6, (N_CONV_LAYERS, DEC_BATCH, CONV_HIST, D_MODEL), 0.5),
        "c_prompt": nrm(7, (BATCH, D_MODEL), 1.0),
        "c_sample": nrm(8, (DEC_BATCH, D_MODEL), 1.0),
        "ada_w": nrm(9, (DEPTH, D_MODEL, 6 * D_MODEL), D_MODEL ** -0.5),
        "ada_b": nrm(10, (DEPTH, 6 * D_MODEL), 0.01),
        "norm_g": 1.0 + nrm(11, (DEPTH, 2, D_MODEL), 0.02),
        "attn_w_in": nrm(12, (N_ATT_LAYERS, D_MODEL, IN0_WIDTH), D_MODEL ** -0.5),
        "attn_w_out": nrm(13, (N_ATT_LAYERS, POOL_WIDTH + ATT_WIDTH, D_MODEL), (POOL_WIDTH + ATT_WIDTH) ** -0.5),
        "pool_w": nrm(14, (N_ATT_LAYERS, POOL_GROUPS, POOL_GROUP_DIM, POOL_GROUP_DIM), POOL_GROUP_DIM ** -0.5),
        "pool_scale": 1.0 + nrm(15, (N_ATT_LAYERS, POOL_WIDTH), 0.02),
        "cmp_pos": nrm(16, (N_ATT_LAYERS, 2, CMP_BLOCK, HEAD_DIM), 0.02),
        "cmp_w1": nrm(17, (N_ATT_LAYERS, 2, CMP_BLOCK * HEAD_DIM, CMP_HIDDEN), (CMP_BLOCK * HEAD_DIM) ** -0.5),
        "cmp_w2": nrm(18, (N_ATT_LAYERS, 2, CMP_HIDDEN, HEAD_DIM), CMP_HIDDEN ** -0.5),
        "conv_pw1": nrm(19, (N_CONV_LAYERS, D_MODEL, 2 * D_MODEL), D_MODEL ** -0.5),
        "conv_dw": nrm(20, (N_CONV_LAYERS, CONV_WIDTH, D_MODEL), CONV_WIDTH ** -0.5),
        "conv_dw_b": nrm(21, (N_CONV_LAYERS, D_MODEL), 0.01),
        "conv_ln_g": 1.0 + nrm(22, (N_CONV_LAYERS, D_MODEL), 0.02),
        "conv_ln_b": nrm(23, (N_CONV_LAYERS, D_MODEL), 0.01),
        "conv_pw2": nrm(24, (N_CONV_LAYERS, D_MODEL, D_MODEL), D_MODEL ** -0.5),
        "ffn_w1": nrm(25, (DEPTH, D_MODEL, 2 * D_FF), D_MODEL ** -0.5),
        "ffn_w2": nrm(26, (DEPTH, D_FF, D_MODEL), D_FF ** -0.5),
        "final_g": 1.0 + nrm(27, (D_MODEL,), 0.02),
    }


def reference(x_prompt, x_sample, cache_kv, page_table, cache_win, state_pool, state_conv, c_prompt, c_sample,
              ada_w, ada_b, norm_g, attn_w_in, attn_w_out, pool_w, pool_scale, cmp_pos, cmp_w1, cmp_w2,
              conv_pw1, conv_dw, conv_dw_b, conv_ln_g, conv_ln_b, conv_pw2, ffn_w1, ffn_w2, final_g):
    wb = cache_win.shape[3]
    y_p, y_s = x_prompt, x_sample
    kv_p, kv_s, win_p, win_s, pool_p, pool_s, conv_p, conv_s = [], [], [], [], [], [], [], []
    for layer in range(DEPTH):
        i = layer // 2
        mp = ada_mod(c_prompt, ada_w[layer], ada_b[layer])
        ms = ada_mod(c_sample, ada_w[layer], ada_b[layer])
        hp = modulate(y_p, norm_g[layer, 0], mp[0], mp[1])
        hs = modulate(y_s, norm_g[layer, 0], ms[0], ms[1])
        if layer % 2 == 0:
            op, kvr, winr, poolr = mixer_ab_prompt(hp, wb, attn_w_in[i], attn_w_out[i], pool_w[i], pool_scale[i],
                                                   cmp_pos[i], cmp_w1[i], cmp_w2[i])
            os_, kvr2, winr2, poolr2 = mixer_ab_sample(hs, cache_kv[i], page_table, cache_win[i], state_pool[i],
                                                       attn_w_in[i], attn_w_out[i], pool_w[i], pool_scale[i],
                                                       cmp_pos[i], cmp_w1[i], cmp_w2[i])
            kv_p.append(kvr); win_p.append(winr); pool_p.append(poolr)
            kv_s.append(kvr2); win_s.append(winr2); pool_s.append(poolr2)
        else:
            zero_hist = jnp.zeros((hp.shape[0], CONV_HIST, D_MODEL), hp.dtype)
            op, cst = conv_mix(hp, zero_hist, conv_pw1[i], conv_dw[i], conv_dw_b[i], conv_ln_g[i], conv_ln_b[i], conv_pw2[i])
            os_, cst2 = conv_mix(hs, state_conv[i], conv_pw1[i], conv_dw[i], conv_dw_b[i], conv_ln_g[i], conv_ln_b[i], conv_pw2[i])
            conv_p.append(cst); conv_s.append(cst2)
        y_p = y_p + mp[2] * op
        y_s = y_s + ms[2] * os_
        y_p = y_p + mp[5] * swiglu(modulate(y_p, norm_g[layer, 1], mp[3], mp[4]), ffn_w1[layer], ffn_w2[layer])
        y_s = y_s + ms[5] * swiglu(modulate(y_s, norm_g[layer, 1], ms[3], ms[4]), ffn_w1[layer], ffn_w2[layer])
    y_prompt = rms_norm(y_p, final_g)
    y_sample = rms_norm(y_s, final_g)
    kv_prompt = jnp.stack(kv_p)
    kv_sample = jnp.stack(kv_s)
    win_prompt = jnp.stack(win_p)
    win_sample = jnp.stack(win_s)
    pool_prompt = jnp.stack(pool_p)
    pool_sample = jnp.stack(pool_s)
    conv_prompt = jnp.stack(conv_p)
    conv_sample = jnp.stack(conv_s)
    return (y_prompt, y_sample, kv_prompt, kv_sample, win_prompt, win_sample, pool_prompt, pool_sample, conv_prompt, conv_sample)
```

```python
import functools

import numpy as np
import jax
import jax.numpy as jnp
from jax import lax
from jax.experimental import pallas as pl
from jax.experimental.pallas import tpu as pltpu

F32 = jnp.float32
BF16 = jnp.bfloat16

NORM_EPS = 1e-6
N_HEADS = 8
HEAD_DIM = 64
N_KV = 2
GROUP_R = N_HEADS // N_KV
POOL_WINDOWS = (2, 4, 8, 16)
POOL_HIST = 15
POOL_PAD = 16
CMP_BLOCK = 32
CMP_STRIDE = 16
SEL_BLOCK = 64
N_SEL = 16
WINDOW = 512
PAGE_SIZE = 128
ROPE_THETA = 10000.0
CONV_WIDTH = 31
CONV_HIST = CONV_WIDTH - 1
CONV_PAD = 32
LANES = 128
NEG = -1e9
V7X_VMEM_LIMIT = 56 * 1024 * 1024


def _cparams(*sem):
    return pltpu.CompilerParams(dimension_semantics=sem, vmem_limit_bytes=V7X_VMEM_LIMIT)


def _dot(a, b):
    return jnp.dot(a, b, preferred_element_type=F32)


def _dot_nt(a, b):
    return lax.dot_general(a, b, (((1,), (1,)), ((), ())), preferred_element_type=F32)


def _sigmoid(x):
    return 1.0 / (1.0 + jnp.exp(-x))


def _silu(x):
    return x * _sigmoid(x)


def _modulate(x, g, shift, scale):
    ms = jnp.mean(x * x, axis=-1, keepdims=True)
    y = x * lax.rsqrt(ms + NORM_EPS)
    return (y * g) * (1.0 + scale) + shift


def _ada_kernel(c_ref, w_ref, b_ref, o_ref):
    a = _silu(c_ref[...]).astype(BF16)
    o_ref[0] = _dot(a, w_ref[0].astype(BF16)) + b_ref[0]


def _ada(c_all, ada_w, ada_b):
    depth, d, n6 = ada_w.shape
    bc = c_all.shape[0]
    tn = n6 // 4
    return pl.pallas_call(
        _ada_kernel,
        out_shape=jax.ShapeDtypeStruct((depth, bc, n6), F32),
        grid=(depth, n6 // tn),
        in_specs=[
            pl.BlockSpec((bc, d), lambda l, j: (0, 0)),
            pl.BlockSpec((1, d, tn), lambda l, j: (l, 0, j)),
            pl.BlockSpec((1, 1, tn), lambda l, j: (l, 0, j)),
        ],
        out_specs=pl.BlockSpec((1, bc, tn), lambda l, j: (l, 0, j)),
        compiler_params=_cparams("parallel", "parallel"),
        name="ada_mod",
    )(c_all, ada_w, ada_b.reshape(depth, 1, n6))


def _rope(x, cos, sin_signed):
    lane = lax.broadcasted_iota(jnp.int32, x.shape, 1)
    first = (lane % HEAD_DIM) < (HEAD_DIM // 2)
    swapped = jnp.where(first, pltpu.roll(x, LANES - HEAD_DIM // 2, 1), pltpu.roll(x, HEAD_DIM // 2, 1))
    return x * cos + swapped * sin_signed


def _pre0_kernel(x_ref, g_ref, sh_ref, sc_ref, w_ref, cos_ref, sin_ref,
                 u_ref, q_ref, kv_ref, win_ref, kvb_ref, gate_ref):
    h = _modulate(x_ref[...], g_ref[...], sh_ref[0], sc_ref[0]).astype(BF16)
    cos = cos_ref[...]
    sin = sin_ref[...]

    def proj(c0, width):
        return _dot(h, w_ref[:, c0:c0 + width])

    u_ref[...] = proj(0, 512)
    for j in range(4):
        qj = _rope(proj(512 + LANES * j, LANES), cos, sin)
        q_ref[:, LANES * j:LANES * (j + 1)] = (qj * (HEAD_DIM ** -0.5)).astype(BF16)
    kc = _rope(proj(1024, LANES), cos, sin)
    vc = proj(1152, LANES)
    ks = _rope(proj(1280, LANES), cos, sin)
    vs = proj(1408, LANES)
    kw = _rope(proj(1536, LANES), cos, sin)
    vw = proj(1664, LANES)
    kv_ref[0] = kc
    kv_ref[1] = vc
    kv_ref[2] = ks
    kv_ref[3] = vs
    win_ref[0] = kw
    win_ref[1] = vw
    kvb_ref[0] = ks.astype(BF16)
    kvb_ref[1] = vs.astype(BF16)
    kvb_ref[2] = kw.astype(BF16)
    kvb_ref[3] = vw.astype(BF16)
    gate_ref[...] = _sigmoid(proj(1792, LANES))


def _pre0(x2, g, shift, scale, w_pad, cos, sin, tm, rows_per_mod, pos_tiles):
    n, d = x2.shape
    r = shift.shape[1]
    mod_spec = pl.BlockSpec((1, r, d), lambda i: (i // rows_per_mod, 0, 0))
    pos_spec = pl.BlockSpec((tm, LANES), lambda i: (i % pos_tiles, 0))
    return pl.pallas_call(
        _pre0_kernel,
        out_shape=(
            jax.ShapeDtypeStruct((n, 512), F32),
            jax.ShapeDtypeStruct((n, 512), BF16),
            jax.ShapeDtypeStruct((4, n, LANES), F32),
            jax.ShapeDtypeStruct((2, n, LANES), F32),
            jax.ShapeDtypeStruct((4, n, LANES), BF16),
            jax.ShapeDtypeStruct((n, LANES), F32),
        ),
        grid=(n // tm,),
        in_specs=[
            pl.BlockSpec((tm, d), lambda i: (i, 0)),
            pl.BlockSpec((1, d), lambda i: (0, 0)),
            mod_spec, mod_spec,
            pl.BlockSpec(w_pad.shape, lambda i: (0, 0)),
            pos_spec, pos_spec,
        ],
        out_specs=(
            pl.BlockSpec((tm, 512), lambda i: (i, 0)),
            pl.BlockSpec((tm, 512), lambda i: (i, 0)),
            pl.BlockSpec((4, tm, LANES), lambda i: (0, i, 0)),
            pl.BlockSpec((2, tm, LANES), lambda i: (0, i, 0)),
            pl.BlockSpec((4, tm, LANES), lambda i: (0, i, 0)),
            pl.BlockSpec((tm, LANES), lambda i: (i, 0)),
        ),
        compiler_params=_cparams("parallel"),
        name="pre0_in_proj",
    )(x2, g, shift, scale, w_pad, cos, sin)


def _rope_tables(pos):
    half = HEAD_DIM // 2
    inv = ROPE_THETA ** (-jnp.arange(half, dtype=F32) / half)
    ang = pos.astype(F32)[:, None] * inv[None, :]
    cos, sin = jnp.cos(ang), jnp.sin(ang)
    cos_t = jnp.tile(jnp.concatenate([cos, cos], axis=1), (1, LANES // HEAD_DIM))
    sin_t = jnp.tile(jnp.concatenate([-sin, sin], axis=1), (1, LANES // HEAD_DIM))
    return cos_t, sin_t


def _pool_body(ext_ref, t_rows, pos0, w_ref, scale_ref, out_ref):
    pos = pos0 + lax.broadcasted_iota(jnp.int32, (t_rows, 1), 0)
    for g, w in enumerate(POOL_WINDOWS):
        cols = slice(LANES * g, LANES * (g + 1))
        x = ext_ref[POOL_PAD:POOL_PAD + t_rows, cols]
        s = x
        for j in range(1, w):
            s = s + ext_ref[POOL_PAD - j:POOL_PAD - j + t_rows, cols]
        cnt = jnp.minimum(pos + 1, w).astype(F32)
        dlt = (s / cnt - x).astype(BF16)
        y = _dot(dlt, w_ref[g]) * scale_ref[:, cols]
        out_ref[:, cols] = y.astype(out_ref.dtype)


def _pool_prompt_kernel(x_ref, prev_ref, w_ref, scale_ref, out_ref, ext_ref, *, tiles_per_batch, tm):
    t_in_b = pl.program_id(0) % tiles_per_batch
    ext_ref[0:POOL_PAD, :] = jnp.where(t_in_b > 0, prev_ref[...], 0.0)
    ext_ref[POOL_PAD:, :] = x_ref[...]
    _pool_body(ext_ref, tm, t_in_b * tm, w_ref, scale_ref, out_ref)


def _pool_prompt(u, w_grp, scale, seq, tm):
    n, c = u.shape
    tpb = seq // tm
    hb = tm // POOL_PAD
    return pl.pallas_call(
        functools.partial(_pool_prompt_kernel, tiles_per_batch=tpb, tm=tm),
        out_shape=jax.ShapeDtypeStruct((n, c), BF16),
        grid=(n // tm,),
        in_specs=[
            pl.BlockSpec((tm, c), lambda i: (i, 0)),
            pl.BlockSpec((POOL_PAD, c), lambda i: (jnp.maximum(i * hb - 1, 0), 0)),
            pl.BlockSpec(w_grp.shape, lambda i: (0, 0, 0)),
            pl.BlockSpec((1, c), lambda i: (0, 0)),
        ],
        out_specs=pl.BlockSpec((tm, c), lambda i: (i, 0)),
        scratch_shapes=[pltpu.VMEM((POOL_PAD + tm, c), F32)],
        compiler_params=_cparams("parallel"),
        name="pool_prompt",
    )(u, u, w_grp, scale)


def _pool_sample_kernel(ext_ref, w_ref, scale_ref, out_ref, *, pos0, t_rows):
    _pool_body(ext_ref.at[0], t_rows, pos0, w_ref, scale_ref, out_ref.at[0])


def _pool_sample(ext, w_grp, scale, pos0):
    b, rows, c = ext.shape
    t_rows = rows - POOL_PAD
    return pl.pallas_call(
        functools.partial(_pool_sample_kernel, pos0=pos0, t_rows=t_rows),
        out_shape=jax.ShapeDtypeStruct((b, t_rows, c), BF16),
        grid=(b,),
        in_specs=[
            pl.BlockSpec((1, rows, c), lambda i: (i, 0, 0)),
            pl.BlockSpec(w_grp.shape, lambda i: (0, 0, 0)),
            pl.BlockSpec((1, c), lambda i: (0, 0)),
        ],
        out_specs=pl.BlockSpec((1, t_rows, c), lambda i: (i, 0, 0)),
        compiler_params=_cparams("parallel"),
        name="pool_sample",
    )(ext, w_grp, scale)


def _chunk_rows(ref2d, rows):
    n = rows // CMP_STRIDE
    return jnp.concatenate([ref2d[pl.ds(r, n, stride=CMP_STRIDE), :] for r in range(CMP_STRIDE)], axis=1)


def _cmp_partial(a, pos_ref, w_ref, p_ref):
    hid2 = w_ref.shape[2] // 2
    p_ref[0, 0, :, 0:hid2] = _dot((a + pos_ref[0, 0:1, :]).astype(BF16), w_ref[0, :, 0:hid2])
    p_ref[0, 0, :, hid2:] = _dot((a + pos_ref[0, 1:2, :]).astype(BF16), w_ref[0, :, hid2:])


def _cmp_rows_kernel(x_ref, pos_ref, w_ref, p_ref, *, rows):
    _cmp_partial(_chunk_rows(x_ref.at[0, 0], rows), pos_ref, w_ref, p_ref)


def _cmp_rows(x4, pos_ab, w1_ab, nbatch, rows):
    nch = rows // CMP_STRIDE
    return pl.pallas_call(
        functools.partial(_cmp_rows_kernel, rows=rows),
        out_shape=jax.ShapeDtypeStruct((2, nbatch, nch, w1_ab.shape[2]), F32),
        grid=(2, nbatch),
        in_specs=[
            pl.BlockSpec((1, 1, rows, LANES), lambda s, b: (s, b, 0, 0)),
            pl.BlockSpec((1, 2, pos_ab.shape[2]), lambda s, b: (s, 0, 0)),
            pl.BlockSpec((1,) + w1_ab.shape[1:], lambda s, b: (s, 0, 0)),
        ],
        out_specs=pl.BlockSpec((1, 1, nch, w1_ab.shape[2]), lambda s, b: (s, b, 0, 0)),
        compiler_params=_cparams("parallel", "parallel"),
        name="cmp_rows",
    )(x4, pos_ab, w1_ab)


PAGES_PER_STEP = 16


def _cmp_pages_kernel(pt_ref, *refs):
    del pt_ref
    page_refs = refs[:PAGES_PER_STEP]
    pos_ref, w_ref, p_ref = refs[PAGES_PER_STEP:]
    a = jnp.concatenate([_chunk_rows(r.at[0, 0], PAGE_SIZE) for r in page_refs], axis=0)
    _cmp_partial(a, pos_ref, w_ref, p_ref)


def _cmp_pages(cache4, page_table_flat, n_pages, pos_ab, w1_ab, nbatch):
    steps = n_pages // PAGES_PER_STEP
    nch = PAGES_PER_STEP * PAGE_SIZE // CMP_STRIDE

    def page_spec(k):
        return pl.BlockSpec(
            (1, 1, PAGE_SIZE, LANES),
            lambda s, b, j, pt: (s, pt[b * n_pages + j * PAGES_PER_STEP + k], 0, 0))

    grid_spec = pltpu.PrefetchScalarGridSpec(
        num_scalar_prefetch=1,
        grid=(2, nbatch, steps),
        in_specs=[page_spec(k) for k in range(PAGES_PER_STEP)] + [
            pl.BlockSpec((1, 2, pos_ab.shape[2]), lambda s, b, j, pt: (s, 0, 0)),
            pl.BlockSpec((1,) + w1_ab.shape[1:], lambda s, b, j, pt: (s, 0, 0)),
        ],
        out_specs=pl.BlockSpec((1, 1, nch, w1_ab.shape[2]), lambda s, b, j, pt: (s, b, j, 0)),
    )
    return pl.pallas_call(
        _cmp_pages_kernel,
        out_shape=jax.ShapeDtypeStruct((2, nbatch, steps * nch, w1_ab.shape[2]), F32),
        grid_spec=grid_spec,
        compiler_params=_cparams("parallel", "parallel", "parallel"),
        name="cmp_pages",
    )(page_table_flat, *([cache4] * PAGES_PER_STEP), pos_ab, w1_ab)


def _gelu_tanh(x):
    return x * (0.5 * (1.0 + jnp.tanh(np.sqrt(2.0 / np.pi).astype(np.float32) * (x + 0.044715 * (x * x * x)))))


def _cmp_combine_kernel(p_ref, w2_ref, o_ref, pb_ref, *, n_p):
    hid2 = p_ref.shape[3] // 2
    pb_ref[0:n_p, :] = p_ref[0, 0, :, hid2:]
    pb_ref[n_p:n_p + 8, :] = jnp.zeros((8, hid2), F32)
    hsum = p_ref[0, 0, :, 0:hid2] + pb_ref[1:n_p + 1, :]
    o_ref[0, 0] = _dot(_gelu_tanh(hsum).astype(BF16), w2_ref[0]).astype(o_ref.dtype)


def _cmp_combine(p, w2_bd):
    _, nbatch, n_p, width = p.shape
    return pl.pallas_call(
        functools.partial(_cmp_combine_kernel, n_p=n_p),
        out_shape=jax.ShapeDtypeStruct((2, nbatch, n_p, LANES), BF16),
        grid=(2, nbatch),
        in_specs=[
            pl.BlockSpec((1, 1, n_p, width), lambda s, b: (s, b, 0, 0)),
            pl.BlockSpec((1,) + w2_bd.shape[1:], lambda s, b: (s, 0, 0)),
        ],
        out_specs=pl.BlockSpec((1, 1, n_p, LANES), lambda s, b: (s, b, 0, 0)),
        scratch_shapes=[pltpu.VMEM((n_p + 8, width // 2), F32)],
        compiler_params=_cparams("parallel", "parallel"),
        name="cmp_combine",
    )(p, w2_bd)


def _cmp_weights(cmp_pos, cmp_w1, cmp_w2):
    hid = cmp_w1.shape[2]
    half = CMP_STRIDE * HEAD_DIM
    eye = jnp.eye(N_KV, dtype=F32)
    pos_ab = jnp.tile(cmp_pos.reshape(2, 2, CMP_STRIDE, 1, HEAD_DIM), (1, 1, 1, N_KV, 1))
    pos_ab = pos_ab.reshape(2, 2, CMP_STRIDE * N_KV * HEAD_DIM)
    w1 = cmp_w1.reshape(2, 2, CMP_STRIDE, HEAD_DIM, hid)
    w1_bd = jnp.einsum("shrdj,gk->shrgdkj", w1, eye)
    w1_bd = w1_bd.reshape(2, 2, CMP_STRIDE * N_KV * HEAD_DIM, N_KV * hid)
    w1_ab = jnp.concatenate([w1_bd[:, 0], w1_bd[:, 1]], axis=2).astype(BF16)
    w2_bd = jnp.einsum("sjd,gk->sgjkd", cmp_w2, eye).reshape(2, N_KV * hid, N_KV * HEAD_DIM).astype(BF16)
    del half
    return pos_ab, w1_ab, w2_bd


def _softmax_update(s, v, state):
    m_c = jnp.max(s, axis=-1, keepdims=True)
    if state is None:
        p = jnp.exp(s - m_c)
        return m_c, jnp.sum(p, axis=-1, keepdims=True), _dot(p.astype(BF16), v)
    m_o, l_o, acc_o = state
    m_n = jnp.maximum(m_o, m_c)
    alpha = jnp.exp(m_o - m_n)
    p = jnp.exp(s - m_n)
    return m_n, alpha * l_o + jnp.sum(p, axis=-1, keepdims=True), alpha * acc_o + _dot(p.astype(BF16), v)


def _attn_prompt_kernel(q_ref, kc_ref, vc_ref, ks_ref, vs_ref, kw_ref, vw_ref, gate_ref, mt_ref, e_ref,
                        o_ref, *, tq, nblk):
    i = pl.program_id(1)
    s0 = i * tq
    ncmp = kc_ref.shape[2]
    row = lax.broadcasted_iota(jnp.int32, (tq, 1), 0)
    col = lax.broadcasted_iota(jnp.int32, (1, tq), 1)
    qpos = s0 + row
    causal = col <= row
    upper = col > row
    gates = gate_ref[...]
    c_end = lax.broadcasted_iota(jnp.int32, (1, ncmp), 1) * CMP_STRIDE + (CMP_BLOCK - 1)
    mask_c = c_end <= qpos

    jb = lax.broadcasted_iota(jnp.int32, (LANES, 1), 0)
    qp_l = s0 + col
    cur = qp_l // SEL_BLOCK
    valid = (jb * SEL_BLOCK <= qp_l) & (jb < nblk)
    forced = (jb == 0) | (jb == cur) | (jb == cur - 1)

    for g in range(N_KV):
        gs = slice(HEAD_DIM * g, HEAD_DIM * (g + 1))
        heads = [GROUP_R * g + h for h in range(GROUP_R)]
        qs = [q_ref[:, HEAD_DIM * hh:HEAD_DIM * (hh + 1)] for hh in heads]

        kc_g = kc_ref[0, 0, :, gs]
        vc_g = vc_ref[0, 0, :, gs]
        p_sum = jnp.zeros((tq, ncmp), F32)
        o_cmp = []
        for qh in qs:
            s = jnp.where(mask_c, _dot_nt(qh, kc_g), -jnp.inf)
            m = jnp.max(s, axis=-1, keepdims=True)
            m = jnp.where(m == -jnp.inf, 0.0, m)
            e = jnp.exp(s - m)
            d = jnp.sum(e, axis=-1, keepdims=True)
            p = e / jnp.where(d > 0, d, 1.0)
            p_sum = p_sum + p
            o_cmp.append(_dot(p.astype(BF16), vc_g))

        p_hi = p_sum.astype(BF16)
        p_lo = (p_sum - p_hi.astype(F32)).astype(BF16)
        imp = _dot_nt(mt_ref[...], p_hi) + _dot_nt(mt_ref[...], p_lo)
        score = jnp.where(valid, jnp.where(forced, jnp.inf, imp), -jnp.inf)
        cnt = jnp.zeros((LANES, tq), jnp.int32)
        for j in range(nblk):
            rj = score[j:j + 1, :]
            beats = (rj > score) | ((rj == score) & (jb > j))
            cnt = cnt + beats.astype(jnp.int32)
        sel = (cnt < N_SEL) & valid
        selneg = jnp.where(sel, 0.0, NEG).T.astype(BF16)

        def sel_chunk(c, state, diag):
            c0 = pl.multiple_of(c * tq, tq)
            bias = _dot(selneg, e_ref[:, pl.ds(c0, tq)])
            if diag:
                bias = jnp.where(causal, bias, NEG)
            k_g = ks_ref[0, 0, pl.ds(c0, tq), gs]
            v_g = vs_ref[0, 0, pl.ds(c0, tq), gs]
            out = []
            for h, qh in enumerate(qs):
                st = None if state is None else state[3 * h:3 * h + 3]
                out.extend(_softmax_update(_dot_nt(qh, k_g) + bias, v_g, st))
            return tuple(out)

        init = []
        for _ in qs:
            init.extend([jnp.full((tq, 1), -3e38, F32), jnp.zeros((tq, 1), F32), jnp.zeros((tq, HEAD_DIM), F32)])
        state = lax.fori_loop(0, i, lambda c, st: sel_chunk(c, st, False), tuple(init))
        state = sel_chunk(i, state, True)
        o_sel = [state[3 * h + 2] / state[3 * h + 1] for h in range(GROUP_R)]

        wstate = [None] * GROUP_R
        for back, mask in ((0, causal), (1, None), (2, upper)):
            c = i - back
            c0 = pl.multiple_of(jnp.maximum(c, 0) * tq, tq)
            k_g = kw_ref[0, 0, pl.ds(c0, tq), gs]
            v_g = vw_ref[0, 0, pl.ds(c0, tq), gs]
            ok = c >= 0
            keep = ok if mask is None else (mask & ok)
            for h, qh in enumerate(qs):
                s = jnp.where(keep, _dot_nt(qh, k_g), NEG)
                wstate[h] = _softmax_update(s, v_g, wstate[h])
        for h, hh in enumerate(heads):
            o_win = wstate[h][2] / wstate[h][1]
            o = (o_cmp[h] * gates[:, 3 * hh:3 * hh + 1] + o_sel[h] * gates[:, 3 * hh + 1:3 * hh + 2]
                 + o_win * gates[:, 3 * hh + 2:3 * hh + 3])
            o_ref[:, HEAD_DIM * hh:HEAD_DIM * (hh + 1)] = o.astype(o_ref.dtype)


def _attn_prompt(q, cmp_kv, kvb, gates, mt, e_mat, nbatch, seq, tq):
    n = q.shape[0]
    nq = seq // tq
    ncmp = cmp_kv.shape[2]
    kv_spec = lambda slot: pl.BlockSpec((1, 1, seq, LANES), lambda b, i: (slot, b, 0, 0))
    cmp_spec = lambda slot: pl.BlockSpec((1, 1, ncmp, LANES), lambda b, i: (slot, b, 0, 0))
    kvb4 = kvb.reshape(4, nbatch, seq, LANES)
    return pl.pallas_call(
        functools.partial(_attn_prompt_kernel, tq=tq, nblk=seq // SEL_BLOCK),
        out_shape=jax.ShapeDtypeStruct((n, N_HEADS * HEAD_DIM), BF16),
        grid=(nbatch, nq),
        in_specs=[
            pl.BlockSpec((tq, N_HEADS * HEAD_DIM), lambda b, i: (b * nq + i, 0)),
            cmp_spec(0), cmp_spec(1),
            kv_spec(0), kv_spec(1), kv_spec(2), kv_spec(3),
            pl.BlockSpec((tq, LANES), lambda b, i: (b * nq + i, 0)),
            pl.BlockSpec(mt.shape, lambda b, i: (0, 0)),
            pl.BlockSpec(e_mat.shape, lambda b, i: (0, 0)),
        ],
        out_specs=pl.BlockSpec((tq, N_HEADS * HEAD_DIM), lambda b, i: (b * nq + i, 0)),
        compiler_params=_cparams("parallel", "arbitrary"),
        name="attn_prompt",
    )(q, cmp_kv, cmp_kv, kvb4, kvb4, kvb4, kvb4, gates, mt, e_mat)


def _imp_matrix(ncmp, nblk_pad):
    per = SEL_BLOCK // CMP_STRIDE
    pad = CMP_BLOCK // CMP_STRIDE - 1
    n = np.arange(ncmp)[:, None]
    j = np.arange(nblk_pad)[None, :]
    return ((n >= per * j - pad) & (n <= per * j + per - 1)).astype(np.float32)


def _attn_sample_cmp_kernel(q_ref, kc_ref, vc_ref, m_ref, oc_ref, idx_ref, *, n_valid_cmp, qpos0, nblk):
    ncmp = kc_ref.shape[2]
    rows = q_ref.shape[2]
    t_row = lax.broadcasted_iota(jnp.int32, (rows, 1), 0) % 8
    c_idx = lax.broadcasted_iota(jnp.int32, (1, ncmp), 1)
    mask_c = (c_idx * CMP_STRIDE + (CMP_BLOCK - 1) <= qpos0 + t_row) & (c_idx < n_valid_cmp)
    p_tok = []
    for g in range(N_KV):
        gs = slice(HEAD_DIM * g, HEAD_DIM * (g + 1))
        s = jnp.where(mask_c, _dot_nt(q_ref[0, g], kc_ref[0, 0, :, gs]), -jnp.inf)
        m = jnp.max(s, axis=-1, keepdims=True)
        m = jnp.where(m == -jnp.inf, 0.0, m)
        e = jnp.exp(s - m)
        d = jnp.sum(e, axis=-1, keepdims=True)
        p = e / jnp.where(d > 0, d, 1.0)
        oc_ref[0, g] = _dot(p.astype(BF16), vc_ref[0, 0, :, gs])
        p_tok.append(jnp.sum(p.reshape(GROUP_R, 8, ncmp), axis=0))
    p_all = jnp.concatenate(p_tok, axis=0)
    p_hi = p_all.astype(BF16)
    p_lo = (p_all - p_hi.astype(F32)).astype(BF16)
    imp = _dot(p_hi, m_ref[...]) + _dot(p_lo, m_ref[...])
    nb_pad = imp.shape[1]
    jb = lax.broadcasted_iota(jnp.int32, (1, nb_pad), 1)
    qpos = qpos0 + lax.broadcasted_iota(jnp.int32, (2 * 8, 1), 0) % 8
    cur = qpos // SEL_BLOCK
    valid = (jb * SEL_BLOCK <= qpos) & (jb < nblk)
    forced = (jb == 0) | (jb == cur) | (jb == cur - 1)
    score = jnp.where(valid, jnp.where(forced, jnp.inf, imp), -jnp.inf)
    avail = jb < nblk
    lane = lax.broadcasted_iota(jnp.int32, (2 * 8, LANES), 1)
    picked = jnp.zeros((2 * 8, LANES), jnp.int32)
    for k in range(N_SEL):
        best = jnp.max(jnp.where(avail, score, -jnp.inf), axis=-1, keepdims=True)
        cand = avail & (score == best)
        idx = jnp.min(jnp.where(cand, jb, nb_pad), axis=-1, keepdims=True)
        picked = jnp.where(lane == k, idx, picked)
        avail = avail & (jb != idx)
    idx_ref[0] = picked


def _attn_sample_cmp(qg, cmp_kv, m_mat, n_valid_cmp, qpos0, nblk):
    nbatch = qg.shape[0]
    ncmp = cmp_kv.shape[2]
    rows = qg.shape[2]
    cmp_spec = lambda slot: pl.BlockSpec((1, 1, ncmp, LANES), lambda b: (slot, b, 0, 0))
    return pl.pallas_call(
        functools.partial(_attn_sample_cmp_kernel, n_valid_cmp=n_valid_cmp, qpos0=qpos0, nblk=nblk),
        out_shape=(
            jax.ShapeDtypeStruct((nbatch, N_KV, rows, HEAD_DIM), F32),
            jax.ShapeDtypeStruct((nbatch, 2 * 8, LANES), jnp.int32),
        ),
        grid=(nbatch,),
        in_specs=[
            pl.BlockSpec((1, N_KV, rows, HEAD_DIM), lambda b: (b, 0, 0, 0)),
            cmp_spec(0), cmp_spec(1),
            pl.BlockSpec(m_mat.shape, lambda b: (0, 0)),
        ],
        out_specs=(
            pl.BlockSpec((1, N_KV, rows, HEAD_DIM), lambda b: (b, 0, 0, 0)),
            pl.BlockSpec((1, 2 * 8, LANES), lambda b: (b, 0, 0)),
        ),
        compiler_params=_cparams("parallel"),
        name="attn_sample_cmp",
    )(qg, cmp_kv, cmp_kv, m_mat)


def _attn_sample_sel_kernel(idx_ref, pt_ref, *refs, dec_seq, qpos0, n_cache_blk, win_rows):
    del pt_ref
    nsel = N_KV * N_SEL
    k_refs = refs[:nsel]
    v_refs = refs[nsel:2 * nsel]
    q_ref, tail_ref, kw_ref, vw_ref, oc_ref, gate_ref, o_ref, osel_ref = refs[2 * nsel:]
    b = pl.program_id(0)
    t = pl.program_id(1)
    rows = q_ref.shape[2]
    t_row = lax.broadcasted_iota(jnp.int32, (rows, 1), 0) % 8
    qpos = qpos0 + t_row
    nkeys = N_SEL * SEL_BLOCK
    colk = lax.broadcasted_iota(jnp.int32, (1, nkeys), 1)

    @pl.when(t == 0)
    def _():
        osel_ref[...] = jnp.zeros(osel_ref.shape, F32)

    for g in range(N_KV):
        gs = slice(HEAD_DIM * g, HEAD_DIM * (g + 1))
        kpos = colk % SEL_BLOCK
        k_blocks, v_blocks = [], []
        for k in range(N_SEL):
            blk = idx_ref[((b * dec_seq + t) * N_KV + g) * N_SEL + k]
            is_new = blk >= n_cache_blk
            k_blocks.append(jnp.where(is_new, tail_ref[0, 0, :, gs], k_refs[g * N_SEL + k][0, 0, :, gs]))
            v_blocks.append(jnp.where(is_new, tail_ref[1, 0, :, gs], v_refs[g * N_SEL + k][0, 0, :, gs]))
            kpos = kpos + jnp.where(colk // SEL_BLOCK == k, blk * SEL_BLOCK, 0)
        k_all = jnp.concatenate(k_blocks, axis=0).astype(BF16)
        v_all = jnp.concatenate(v_blocks, axis=0).astype(BF16)
        s = jnp.where(kpos <= qpos, _dot_nt(q_ref[0, g], k_all), -jnp.inf)
        m = jnp.max(s, axis=-1, keepdims=True)
        m = jnp.where(m == -jnp.inf, 0.0, m)
        e = jnp.exp(s - m)
        d = jnp.sum(e, axis=-1, keepdims=True)
        p = e / jnp.where(d > 0, d, 1.0)
        o = _dot(p.astype(BF16), v_all)
        osel_ref[g] = osel_ref[g] + jnp.where(t_row == t, o, 0.0)

    @pl.when(t == dec_seq - 1)
    def _():
        colw = lax.broadcasted_iota(jnp.int32, (1, kw_ref.shape[1]), 1)
        dpos = t_row + (win_rows - dec_seq) - colw
        mask_w = (dpos >= 0) & (dpos < WINDOW) & (colw < win_rows)
        for g in range(N_KV):
            gs = slice(HEAD_DIM * g, HEAD_DIM * (g + 1))
            s = jnp.where(mask_w, _dot_nt(q_ref[0, g], kw_ref[0, :, gs].astype(BF16)), -jnp.inf)
            m = jnp.max(s, axis=-1, keepdims=True)
            m = jnp.where(m == -jnp.inf, 0.0, m)
            e = jnp.exp(s - m)
            d = jnp.sum(e, axis=-1, keepdims=True)
            p = e / jnp.where(d > 0, d, 1.0)
            o_win = _dot(p.astype(BF16), vw_ref[0, :, gs].astype(BF16))
            gt = gate_ref[0, g]
            o_ref[0, g] = oc_ref[0, g] * gt[:, 0:1] + osel_ref[g] * gt[:, 1:2] + o_win * gt[:, 2:3]


def _attn_sample_sel(idx_flat, pt_flat, cache_blk, qg, tail, kw, vw, o_cmp, gates_g, dec_seq, n_pages, qpos0,
                     win_rows):
    nbatch = qg.shape[0]
    rows = qg.shape[2]
    n_cache_blk = n_pages * (PAGE_SIZE // SEL_BLOCK)
    per_page = PAGE_SIZE // SEL_BLOCK

    def blk_spec(slot, g, k):
        def imap(b, t, idx, pt):
            blk = jnp.minimum(idx[((b * dec_seq + t) * N_KV + g) * N_SEL + k], n_cache_blk - 1)
            return (slot, pt[b * n_pages + blk // per_page] * per_page + blk % per_page, 0, 0)
        return pl.BlockSpec((1, 1, SEL_BLOCK, LANES), imap)

    k_specs = [blk_spec(2, g, k) for g in range(N_KV) for k in range(N_SEL)]
    v_specs = [blk_spec(3, g, k) for g in range(N_KV) for k in range(N_SEL)]
    per_b = lambda shape: pl.BlockSpec((1,) + shape, lambda b, t, idx, pt: (b,) + (0,) * len(shape))
    grid_spec = pltpu.PrefetchScalarGridSpec(
        num_scalar_prefetch=2,
        grid=(nbatch, dec_seq),
        in_specs=k_specs + v_specs + [
            per_b((N_KV, rows, HEAD_DIM)),
            pl.BlockSpec((2, 1, SEL_BLOCK, LANES), lambda b, t, idx, pt: (0, b, 0, 0)),
            per_b(kw.shape[1:]), per_b(vw.shape[1:]),
            per_b((N_KV, rows, HEAD_DIM)),
            per_b((N_KV, rows, LANES)),
        ],
        out_specs=per_b((N_KV, rows, HEAD_DIM)),
        scratch_shapes=[pltpu.VMEM((N_KV, rows, HEAD_DIM), F32)],
    )
    n_blk_specs = 2 * N_KV * N_SEL
    return pl.pallas_call(
        functools.partial(_attn_sample_sel_kernel, dec_seq=dec_seq, qpos0=qpos0, n_cache_blk=n_cache_blk,
                          win_rows=win_rows),
        out_shape=jax.ShapeDtypeStruct((nbatch, N_KV, rows, HEAD_DIM), F32),
        grid_spec=grid_spec,
        compiler_params=_cparams("parallel", "arbitrary"),
        name="attn_sample_sel",
    )(idx_flat, pt_flat, *([cache_blk] * n_blk_specs), qg, tail, kw, vw, o_cmp, gates_g)


FFN_CHUNK = 256


def _post_kernel(*refs, n_a, final):
    y_ref = refs[0]
    a_refs = refs[1:1 + 2 * n_a]
    gmix_ref, g2_ref, sh_ref, sc_ref, gffn_ref, w1_ref, w2_ref = refs[1 + 2 * n_a:8 + 2 * n_a]
    rest = refs[8 + 2 * n_a:]
    out_ref = rest[-1]
    mix = _dot(a_refs[0][...], a_refs[1][...])
    for k in range(1, n_a):
        mix = mix + _dot(a_refs[2 * k][...], a_refs[2 * k + 1][...])
    y1 = y_ref[...] + gmix_ref[0] * mix
    h = _modulate(y1, g2_ref[...], sh_ref[0], sc_ref[0]).astype(BF16)
    d_ff = w2_ref.shape[0]
    acc = jnp.zeros(y1.shape, F32)
    for c in range(d_ff // FFN_CHUNK):
        c0 = c * FFN_CHUNK
        gate = _dot(h, w1_ref[:, c0:c0 + FFN_CHUNK])
        up = _dot(h, w1_ref[:, d_ff + c0:d_ff + c0 + FFN_CHUNK])
        acc = acc + _dot((_silu(gate) * up).astype(BF16), w2_ref[c0:c0 + FFN_CHUNK, :])
    y2 = y1 + gffn_ref[0] * acc
    if final:
        fg_ref = rest[0]
        ms = jnp.mean(y2 * y2, axis=-1, keepdims=True)
        y2 = (y2 * lax.rsqrt(ms + NORM_EPS)) * fg_ref[...]
    out_ref[...] = y2


def _post(y, a_list, wo_list, gmix, g2, shift, scale, gffn, w1, w2, final_g, tm, rows_per_mod):
    n, d = y.shape
    r = shift.shape[1]
    mod_spec = pl.BlockSpec((1, r, d), lambda i: (i // rows_per_mod, 0, 0))
    const = lambda arr: pl.BlockSpec(arr.shape, lambda i: (0,) * arr.ndim, pipeline_mode=pl.Buffered(1))
    in_specs = [pl.BlockSpec((tm, d), lambda i: (i, 0))]
    args = [y]
    for a, wo in zip(a_list, wo_list):
        in_specs += [pl.BlockSpec((tm, a.shape[1]), lambda i: (i, 0)), const(wo)]
        args += [a, wo]
    in_specs += [mod_spec, pl.BlockSpec((1, d), lambda i: (0, 0)), mod_spec, mod_spec, mod_spec, const(w1), const(w2)]
    args += [gmix, g2, shift, scale, gffn, w1, w2]
    if final_g is not None:
        in_specs.append(pl.BlockSpec((1, d), lambda i: (0, 0)))
        args.append(final_g)
    return pl.pallas_call(
        functools.partial(_post_kernel, n_a=len(a_list), final=final_g is not None),
        out_shape=jax.ShapeDtypeStruct((n, d), F32),
        grid=(n // tm,),
        in_specs=in_specs,
        out_specs=pl.BlockSpec((tm, d), lambda i: (i, 0)),
        compiler_params=_cparams("parallel"),
        name="post_proj_ffn",
    )(*args)


def _pre1_kernel(x_ref, g_ref, sh_ref, sc_ref, w_ref, u_ref):
    h = _modulate(x_ref[...], g_ref[...], sh_ref[0], sc_ref[0]).astype(BF16)
    d = u_ref.shape[1]
    a = _dot(h, w_ref[:, 0:d])
    b = _dot(h, w_ref[:, d:2 * d])
    u_ref[...] = a * _sigmoid(b)


def _pre1(x2, g, shift, scale, pw1, tm, rows_per_mod):
    n, d = x2.shape
    r = shift.shape[1]
    mod_spec = pl.BlockSpec((1, r, d), lambda i: (i // rows_per_mod, 0, 0))
    return pl.pallas_call(
        _pre1_kernel,
        out_shape=jax.ShapeDtypeStruct((n, d), F32),
        grid=(n // tm,),
        in_specs=[
            pl.BlockSpec((tm, d), lambda i: (i, 0)),
            pl.BlockSpec((1, d), lambda i: (0, 0)),
            mod_spec, mod_spec,
            pl.BlockSpec(pw1.shape, lambda i: (0, 0)),
        ],
        out_specs=pl.BlockSpec((tm, d), lambda i: (i, 0)),
        compiler_params=_cparams("parallel"),
        name="pre1_pw_glu",
    )(x2, g, shift, scale, pw1)


def _conv_body(ext_ref, t_rows, dw_ref, dwb_ref, lng_ref, lnb_ref, out_ref):
    off = CONV_PAD - CONV_HIST
    acc = ext_ref[off:off + t_rows, :] * dw_ref[0:1, :]
    for k in range(1, CONV_WIDTH):
        acc = acc + ext_ref[off + k:off + k + t_rows, :] * dw_ref[k:k + 1, :]
    y = acc + dwb_ref[...]
    mu = jnp.mean(y, axis=-1, keepdims=True)
    var = jnp.mean(jnp.square(y - mu), axis=-1, keepdims=True)
    z = (y - mu) * lax.rsqrt(var + NORM_EPS) * lng_ref[...] + lnb_ref[...]
    out_ref[...] = _silu(z).astype(out_ref.dtype)


def _conv_prompt_kernel(x_ref, prev_ref, dw_ref, dwb_ref, lng_ref, lnb_ref, out_ref, ext_ref, *,
                        tiles_per_batch, tm):
    t_in_b = pl.program_id(0) % tiles_per_batch
    ext_ref[0:CONV_PAD, :] = jnp.where(t_in_b > 0, prev_ref[...], 0.0)
    ext_ref[CONV_PAD:, :] = x_ref[...]
    _conv_body(ext_ref, tm, dw_ref, dwb_ref, lng_ref, lnb_ref, out_ref)


def _conv_prompt(u, dw, dwb, lng, lnb, seq, tm):
    n, d = u.shape
    tpb = seq // tm
    hb = tm // CONV_PAD
    vec = pl.BlockSpec((1, d), lambda i: (0, 0))
    return pl.pallas_call(
        functools.partial(_conv_prompt_kernel, tiles_per_batch=tpb, tm=tm),
        out_shape=jax.ShapeDtypeStruct((n, d), BF16),
        grid=(n // tm,),
        in_specs=[
            pl.BlockSpec((tm, d), lambda i: (i, 0)),
            pl.BlockSpec((CONV_PAD, d), lambda i: (jnp.maximum(i * hb - 1, 0), 0)),
            pl.BlockSpec(dw.shape, lambda i: (0, 0)),
            vec, vec, vec,
        ],
        out_specs=pl.BlockSpec((tm, d), lambda i: (i, 0)),
        scratch_shapes=[pltpu.VMEM((CONV_PAD + tm, d), F32)],
        compiler_params=_cparams("parallel"),
        name="conv_prompt",
    )(u, u, dw, dwb, lng, lnb)


def _conv_sample_kernel(ext_ref, dw_ref, dwb_ref, lng_ref, lnb_ref, out_ref, *, t_rows):
    _conv_body(ext_ref.at[0], t_rows, dw_ref, dwb_ref, lng_ref, lnb_ref, out_ref.at[0])


def _conv_sample(ext, dw, dwb, lng, lnb):
    b, rows, d = ext.shape
    t_rows = rows - CONV_PAD
    vec = pl.BlockSpec((1, d), lambda i: (0, 0))
    return pl.pallas_call(
        functools.partial(_conv_sample_kernel, t_rows=t_rows),
        out_shape=jax.ShapeDtypeStruct((b, t_rows, d), BF16),
        grid=(b,),
        in_specs=[
            pl.BlockSpec((1, rows, d), lambda i: (i, 0, 0)),
            pl.BlockSpec(dw.shape, lambda i: (0, 0)),
            vec, vec, vec,
        ],
        out_specs=pl.BlockSpec((1, t_rows, d), lambda i: (i, 0, 0)),
        compiler_params=_cparams("parallel"),
        name="conv_sample",
    )(ext, dw, dwb, lng, lnb)


ROW_TILE = 512
ATTN_TILE = 256


def _group_rows(x, nbatch, dec_seq):
    w = x.shape[1] // N_HEADS
    x = x.reshape(nbatch, dec_seq, N_KV, GROUP_R, w).transpose(0, 2, 3, 1, 4)
    x = jnp.pad(x, ((0, 0), (0, 0), (0, 0), (0, 8 - dec_seq), (0, 0)))
    return x.reshape(nbatch, N_KV, GROUP_R * 8, w)


def _ungroup_rows(x, nbatch, dec_seq):
    w = x.shape[3]
    x = x.reshape(nbatch, N_KV, GROUP_R, 8, w)[:, :, :, :dec_seq]
    return x.transpose(0, 3, 1, 2, 4).reshape(nbatch * dec_seq, N_HEADS * w)


def kernel(x_prompt, x_sample, cache_kv, page_table, cache_win, state_pool, state_conv, c_prompt, c_sample,
           ada_w, ada_b, norm_g, attn_w_in, attn_w_out, pool_w, pool_scale, cmp_pos, cmp_w1, cmp_w2,
           conv_pw1, conv_dw, conv_dw_b, conv_ln_g, conv_ln_b, conv_pw2, ffn_w1, ffn_w2, final_g):
    nb_p, seq, d = x_prompt.shape
    nb_s, dec_seq, _ = x_sample.shape
    n_p, n_s = nb_p * seq, nb_s * dec_seq
    n_pages = page_table.shape[1]
    past = n_pages * PAGE_SIZE
    wb = cache_win.shape[3]
    tm = min(ROW_TILE, seq)
    tq = min(ATTN_TILE, seq)
    pool_width = pool_w.shape[1] * pool_w.shape[2]
    att_width = N_HEADS * HEAD_DIM

    mods = _ada(jnp.concatenate([c_prompt, c_sample], axis=0), ada_w, ada_b)

    def mod_p(layer, k):
        return mods[layer, :nb_p, k * d:(k + 1) * d].reshape(nb_p, 1, d)

    def mod_s(layer, k):
        return jnp.repeat(mods[layer, nb_p:, k * d:(k + 1) * d], dec_seq, axis=0).reshape(1, n_s, d)

    y_p = x_prompt.reshape(n_p, d)
    y_s = x_sample.reshape(n_s, d)
    row = lambda v: v.reshape(1, -1)

    w_in = attn_w_in[0]
    w_in_pad = jnp.pad(w_in, ((0, 0), (0, (-w_in.shape[1]) % LANES))).astype(BF16)
    cos_p, sin_p = _rope_tables(jnp.arange(seq))
    cos_s, sin_s = _rope_tables(past + jnp.arange(n_s) % dec_seq)
    g0 = row(norm_g[0, 0])
    u_p, q_p, kv_p, win_p, kvb_p, gate_p = _pre0(y_p, g0, mod_p(0, 0), mod_p(0, 1), w_in_pad, cos_p, sin_p,
                                                  tm, seq // tm, seq // tm)
    u_s, q_s, kv_s, win_s, _, gate_s = _pre0(y_s, g0, mod_s(0, 0), mod_s(0, 1), w_in_pad, cos_s, sin_s,
                                             n_s, 1, 1)

    pool_w_b = pool_w[0].astype(BF16)
    pool_sc = row(pool_scale[0])
    ypool_p = _pool_prompt(u_p, pool_w_b, pool_sc, seq, tm)
    u_s3 = u_s.reshape(nb_s, dec_seq, pool_width)
    pool_ext = jnp.concatenate([
        jnp.zeros((nb_s, POOL_PAD - POOL_HIST, pool_width), F32), state_pool[0], u_s3,
        jnp.zeros((nb_s, 8 - dec_seq, pool_width), F32)], axis=1)
    ypool_s = _pool_sample(pool_ext, pool_w_b, pool_sc, past)[:, :dec_seq].reshape(n_s, pool_width)

    pos_ab, w1_ab, w2_bd = _cmp_weights(cmp_pos[0], cmp_w1[0], cmp_w2[0])
    kv_p4 = kv_p.reshape(4, nb_p, seq, LANES)
    cmp_p = _cmp_combine(_cmp_rows(kv_p4, pos_ab, w1_ab, nb_p, seq), w2_bd)
    cache4 = cache_kv[0].reshape(4, cache_kv.shape[2], PAGE_SIZE, LANES)
    pt_flat = page_table.reshape(-1)
    part_past = _cmp_pages(cache4, pt_flat, n_pages, pos_ab, w1_ab, nb_s)
    tail_rows = 2 * PAGE_SIZE
    kv_s4 = kv_s.reshape(4, nb_s, dec_seq, LANES)
    tail4 = jnp.pad(kv_s4, ((0, 0), (0, 0), (0, tail_rows - dec_seq), (0, 0)))
    part_tail = _cmp_rows(tail4, pos_ab, w1_ab, nb_s, tail_rows)
    cmp_s = _cmp_combine(jnp.concatenate([part_past, part_tail], axis=2), w2_bd)
    total_len = past + dec_seq
    padded_len = -(-total_len // SEL_BLOCK) * SEL_BLOCK
    n_cmp_s = padded_len // CMP_STRIDE - CMP_BLOCK // CMP_STRIDE + 1
    nblk_s = padded_len // SEL_BLOCK

    nblk_p = seq // SEL_BLOCK
    ncmp_p = cmp_p.shape[2]
    mt_p = jnp.asarray(_imp_matrix(ncmp_p, LANES).T, BF16)
    e_mat = jnp.asarray((np.arange(seq)[None, :] // SEL_BLOCK == np.arange(LANES)[:, None]).astype(np.float32),
                        BF16)
    o_p = _attn_prompt(q_p, cmp_p, kvb_p, gate_p, mt_p, e_mat, nb_p, seq, tq)

    qg_s = _group_rows(q_s, nb_s, dec_seq)
    nblk_pad = -(-nblk_s // LANES) * LANES
    m_s = jnp.asarray(_imp_matrix(cmp_s.shape[2], nblk_pad), BF16)
    o_cmp_s, picked = _attn_sample_cmp(qg_s, cmp_s, m_s, n_cmp_s, past, nblk_s)
    idx = picked.reshape(nb_s, N_KV, 8, LANES)[:, :, :dec_seq, :N_SEL].transpose(0, 2, 1, 3).reshape(-1)
    cache_blk = cache_kv[0].reshape(4, cache_kv.shape[2] * (PAGE_SIZE // SEL_BLOCK), SEL_BLOCK, LANES)
    tail_blk = jnp.pad(kv_s4[2:4], ((0, 0), (0, 0), (0, SEL_BLOCK - dec_seq), (0, 0)))
    win_s4 = win_s.reshape(2, nb_s, dec_seq, LANES)
    band = jnp.concatenate([cache_win[0].reshape(2, nb_s, wb, LANES), win_s4], axis=2)
    band_pad = jnp.pad(band, ((0, 0), (0, 0), (0, (-band.shape[2]) % 8), (0, 0)))
    gates_g = jnp.pad(_group_rows(gate_s[:, :3 * N_HEADS], nb_s, dec_seq), ((0, 0), (0, 0), (0, 0), (0, LANES - 3)))
    o_s = _attn_sample_sel(idx, pt_flat, cache_blk, qg_s, tail_blk, band_pad[0], band_pad[1], o_cmp_s, gates_g,
                           dec_seq, n_pages, past, wb + dec_seq)
    o_s = _ungroup_rows(o_s, nb_s, dec_seq).astype(BF16)

    w_out = attn_w_out[0].astype(BF16)
    wo_list = [w_out[:pool_width], w_out[pool_width:]]
    g1 = row(norm_g[0, 1])
    ffn1_0, ffn2_0 = ffn_w1[0].astype(BF16), ffn_w2[0].astype(BF16)
    y_p = _post(y_p, [ypool_p, o_p], wo_list, mod_p(0, 2), g1, mod_p(0, 3), mod_p(0, 4), mod_p(0, 5),
                ffn1_0, ffn2_0, None, tm, seq // tm)
    y_s = _post(y_s, [ypool_s, o_s], wo_list, mod_s(0, 2), g1, mod_s(0, 3), mod_s(0, 4), mod_s(0, 5),
                ffn1_0, ffn2_0, None, n_s, 1)

    g0 = row(norm_g[1, 0])
    pw1 = conv_pw1[0].astype(BF16)
    uc_p = _pre1(y_p, g0, mod_p(1, 0), mod_p(1, 1), pw1, tm, seq // tm)
    uc_s = _pre1(y_s, g0, mod_s(1, 0), mod_s(1, 1), pw1, n_s, 1)
    dwb, lng, lnb = row(conv_dw_b[0]), row(conv_ln_g[0]), row(conv_ln_b[0])
    cv_p = _conv_prompt(uc_p, conv_dw[0], dwb, lng, lnb, seq, tm)
    uc_s3 = uc_s.reshape(nb_s, dec_seq, d)
    conv_ext = jnp.concatenate([
        jnp.zeros((nb_s, CONV_PAD - CONV_HIST, d), F32), state_conv[0], uc_s3,
        jnp.zeros((nb_s, 8 - dec_seq, d), F32)], axis=1)
    cv_s = _conv_sample(conv_ext, conv_dw[0], dwb, lng, lnb)[:, :dec_seq].reshape(n_s, d)

    pw2 = conv_pw2[0].astype(BF16)
    g1 = row(norm_g[1, 1])
    ffn1_1, ffn2_1 = ffn_w1[1].astype(BF16), ffn_w2[1].astype(BF16)
    fg = row(final_g)
    y_p = _post(y_p, [cv_p], [pw2], mod_p(1, 2), g1, mod_p(1, 3), mod_p(1, 4), mod_p(1, 5),
                ffn1_1, ffn2_1, fg, tm, seq // tm)
    y_s = _post(y_s, [cv_s], [pw2], mod_s(1, 2), g1, mod_s(1, 3), mod_s(1, 4), mod_s(1, 5),
                ffn1_1, ffn2_1, fg, n_s, 1)

    y_prompt = y_p.reshape(nb_p, seq, d)
    y_sample = y_s.reshape(nb_s, dec_seq, d)
    kv_prompt = kv_p.reshape(1, 4, nb_p, seq, N_KV, HEAD_DIM)
    kv_sample = kv_s.reshape(1, 4, nb_s, dec_seq, N_KV, HEAD_DIM)
    win_p5 = win_p.reshape(2, nb_p, seq, N_KV, HEAD_DIM)
    if seq >= wb:
        win_prompt = win_p5[:, :, seq - wb:]
    else:
        win_prompt = jnp.pad(win_p5, ((0, 0), (0, 0), (wb - seq, 0), (0, 0), (0, 0)))
    win_sample = band[:, :, dec_seq:].reshape(2, nb_s, wb, N_KV, HEAD_DIM)
    u_p3 = u_p.reshape(nb_p, seq, pool_width)
    pool_prompt = jnp.concatenate([jnp.zeros((nb_p, POOL_HIST, pool_width), F32), u_p3], axis=1)[:, -POOL_HIST:]
    pool_sample = jnp.concatenate([state_pool[0], u_s3], axis=1)[:, -POOL_HIST:]
    uc_p3 = uc_p.reshape(nb_p, seq, d)
    conv_prompt = jnp.concatenate([jnp.zeros((nb_p, CONV_HIST, d), F32), uc_p3], axis=1)[:, -CONV_HIST:]
    conv_sample = jnp.concatenate([state_conv[0], uc_s3], axis=1)[:, -CONV_HIST:]
    return (y_prompt, y_sample, kv_prompt, kv_sample, win_prompt[None], win_sample[None],
            pool_prompt[None], pool_sample[None], conv_prompt[None], conv_sample[None])
```

```python
import functools

import numpy as np
import jax
import jax.numpy as jnp
from jax import lax
from jax.experimental import pallas as pl
from jax.experimental.pallas import tpu as pltpu

F32 = jnp.float32
BF16 = jnp.bfloat16

NORM_EPS = 1e-6
N_HEADS = 8
HEAD_DIM = 64
N_KV = 2
GROUP_R = N_HEADS // N_KV
POOL_WINDOWS = (2, 4, 8, 16)
POOL_HIST = 15
POOL_PAD = 16
CMP_BLOCK = 32
CMP_STRIDE = 16
SEL_BLOCK = 64
N_SEL = 16
WINDOW = 512
PAGE_SIZE = 128
ROPE_THETA = 10000.0
CONV_WIDTH = 31
CONV_HIST = CONV_WIDTH - 1
CONV_PAD = 32
LANES = 128
NEG = -1e9
V7X_VMEM_LIMIT = 56 * 1024 * 1024


def _cparams(*sem):
    return pltpu.CompilerParams(dimension_semantics=sem, vmem_limit_bytes=V7X_VMEM_LIMIT)


def _dot(a, b):
    return jnp.dot(a, b, preferred_element_type=F32)


def _dot_nt(a, b):
    return lax.dot_general(a, b, (((1,), (1,)), ((), ())), preferred_element_type=F32)


def _sigmoid(x):
    return 1.0 / (1.0 + jnp.exp(-x))


def _silu(x):
    return x * _sigmoid(x)


def _modulate(x, g, shift, scale):
    ms = jnp.mean(x * x, axis=-1, keepdims=True)
    y = x * lax.rsqrt(ms + NORM_EPS)
    return (y * g) * (1.0 + scale) + shift


def _ada_kernel(c_ref, w_ref, b_ref, o_ref):
    a = _silu(c_ref[...]).astype(BF16)
    o_ref[0] = _dot(a, w_ref[0].astype(BF16)) + b_ref[0]


def _ada(c_all, ada_w, ada_b):
    depth, d, n6 = ada_w.shape
    bc = c_all.shape[0]
    tn = n6 // 4
    return pl.pallas_call(
        _ada_kernel,
        out_shape=jax.ShapeDtypeStruct((depth, bc, n6), F32),
        grid=(depth, n6 // tn),
        in_specs=[
            pl.BlockSpec((bc, d), lambda l, j: (0, 0)),
            pl.BlockSpec((1, d, tn), lambda l, j: (l, 0, j)),
            pl.BlockSpec((1, 1, tn), lambda l, j: (l, 0, j)),
        ],
        out_specs=pl.BlockSpec((1, bc, tn), lambda l, j: (l, 0, j)),
        compiler_params=_cparams("parallel", "parallel"),
        name="ada_mod",
    )(c_all, ada_w, ada_b.reshape(depth, 1, n6))


def _rope(x, cos, sin_signed):
    lane = lax.broadcasted_iota(jnp.int32, x.shape, 1)
    first = (lane % HEAD_DIM) < (HEAD_DIM // 2)
    swapped = jnp.where(first, pltpu.roll(x, LANES - HEAD_DIM // 2, 1), pltpu.roll(x, HEAD_DIM // 2, 1))
    return x * cos + swapped * sin_signed


def _pre0_project(x_ref, g_ref, sh_ref, sc_ref, w_ref, cos_ref, sin_ref, u_ref, q_ref, gate_ref):
    h = _modulate(x_ref[...], g_ref[...], sh_ref[0], sc_ref[0]).astype(BF16)
    cos = cos_ref[...]
    sin = sin_ref[...]

    def proj(c0, width):
        return _dot(h, w_ref[:, c0:c0 + width])

    u_ref[...] = proj(0, 512)
    for j in range(4):
        qj = _rope(proj(512 + LANES * j, LANES), cos, sin)
        q_ref[:, LANES * j:LANES * (j + 1)] = (qj * (HEAD_DIM ** -0.5)).astype(BF16)
    gate_ref[...] = _sigmoid(proj(1792, LANES))
    kc = _rope(proj(1024, LANES), cos, sin)
    vc = proj(1152, LANES)
    ks = _rope(proj(1280, LANES), cos, sin)
    vs = proj(1408, LANES)
    kw = _rope(proj(1536, LANES), cos, sin)
    vw = proj(1664, LANES)
    return kc, vc, ks, vs, kw, vw


def _pre0_rows_kernel(x_ref, g_ref, sh_ref, sc_ref, w_ref, cos_ref, sin_ref,
                      u_ref, q_ref, gate_ref, kv_ref, win_ref):
    kc, vc, ks, vs, kw, vw = _pre0_project(x_ref, g_ref, sh_ref, sc_ref, w_ref, cos_ref, sin_ref,
                                           u_ref, q_ref, gate_ref)
    for k, seg in enumerate((kc, vc, ks, vs)):
        kv_ref[k] = seg
    win_ref[0] = kw
    win_ref[1] = vw


def _store_transposed(ref, lead, x):
    xt = x.T
    for g in range(N_KV):
        ref[lead + (g,)] = xt[HEAD_DIM * g:HEAD_DIM * (g + 1), :].astype(ref.dtype)


def _pre0_seq_kernel(x_ref, g_ref, sh_ref, sc_ref, w_ref, cos_ref, sin_ref,
                     u_ref, q_ref, gate_ref, kvt_ref, wint_ref, kvbt_ref, cmp_ref):
    kc, vc, ks, vs, kw, vw = _pre0_project(x_ref, g_ref, sh_ref, sc_ref, w_ref, cos_ref, sin_ref,
                                           u_ref, q_ref, gate_ref)
    for k, seg in enumerate((kc, vc, ks, vs)):
        _store_transposed(kvt_ref, (k, 0), seg)
    _store_transposed(wint_ref, (0, 0), kw)
    _store_transposed(wint_ref, (1, 0), vw)
    for k, seg in enumerate((ks, vs, kw, vw)):
        _store_transposed(kvbt_ref, (k, 0), seg)
    cmp_ref[0] = kc
    cmp_ref[1] = vc


def _pre0(x2, g, shift, scale, w_pad, cos, sin, tm, rows_per_mod, pos_tiles, seq=None):
    n, d = x2.shape
    r = shift.shape[1]
    mod_spec = pl.BlockSpec((1, r, d), lambda i: (i // rows_per_mod, 0, 0))
    pos_spec = pl.BlockSpec((tm, LANES), lambda i: (i % pos_tiles, 0))
    row_spec = lambda w: pl.BlockSpec((tm, w), lambda i: (i, 0))
    out_shape = [jax.ShapeDtypeStruct((n, 512), F32), jax.ShapeDtypeStruct((n, 512), BF16),
                 jax.ShapeDtypeStruct((n, LANES), F32)]
    out_specs = [row_spec(512), row_spec(512), row_spec(LANES)]
    if seq is None:
        body = _pre0_rows_kernel
        out_shape += [jax.ShapeDtypeStruct((4, n, LANES), F32), jax.ShapeDtypeStruct((2, n, LANES), F32)]
        out_specs += [pl.BlockSpec((4, tm, LANES), lambda i: (0, i, 0)),
                      pl.BlockSpec((2, tm, LANES), lambda i: (0, i, 0))]
    else:
        body = _pre0_seq_kernel
        nb, tpb = n // seq, seq // tm
        t_spec = lambda k: pl.BlockSpec((k, 1, N_KV, HEAD_DIM, tm), lambda i: (0, i // tpb, 0, 0, i % tpb))
        out_shape += [jax.ShapeDtypeStruct((4, nb, N_KV, HEAD_DIM, seq), F32),
                      jax.ShapeDtypeStruct((2, nb, N_KV, HEAD_DIM, seq), F32),
                      jax.ShapeDtypeStruct((4, nb, N_KV, HEAD_DIM, seq), BF16),
                      jax.ShapeDtypeStruct((2, n, LANES), F32)]
        out_specs += [t_spec(4), t_spec(2), t_spec(4), pl.BlockSpec((2, tm, LANES), lambda i: (0, i, 0))]
    return pl.pallas_call(
        body,
        out_shape=tuple(out_shape),
        grid=(n // tm,),
        in_specs=[
            pl.BlockSpec((tm, d), lambda i: (i, 0)),
            pl.BlockSpec((1, d), lambda i: (0, 0)),
            mod_spec, mod_spec,
            pl.BlockSpec(w_pad.shape, lambda i: (0, 0)),
            pos_spec, pos_spec,
        ],
        out_specs=tuple(out_specs),
        compiler_params=_cparams("parallel"),
        name="pre0_in_proj",
    )(x2, g, shift, scale, w_pad, cos, sin)


def _rope_tables(pos):
    half = HEAD_DIM // 2
    inv = ROPE_THETA ** (-jnp.arange(half, dtype=F32) / half)
    ang = pos.astype(F32)[:, None] * inv[None, :]
    cos, sin = jnp.cos(ang), jnp.sin(ang)
    cos_t = jnp.tile(jnp.concatenate([cos, cos], axis=1), (1, LANES // HEAD_DIM))
    sin_t = jnp.tile(jnp.concatenate([-sin, sin], axis=1), (1, LANES // HEAD_DIM))
    return cos_t, sin_t


def _pool_body(ext_ref, t_rows, pos0, w_ref, scale_ref, out_ref):
    pos = pos0 + lax.broadcasted_iota(jnp.int32, (t_rows, 1), 0)
    for g, w in enumerate(POOL_WINDOWS):
        cols = slice(LANES * g, LANES * (g + 1))
        x = ext_ref[POOL_PAD:POOL_PAD + t_rows, cols]
        s = x
        for j in range(1, w):
            s = s + ext_ref[POOL_PAD - j:POOL_PAD - j + t_rows, cols]
        cnt = jnp.minimum(pos + 1, w).astype(F32)
        dlt = (s / cnt - x).astype(BF16)
        y = _dot(dlt, w_ref[g]) * scale_ref[:, cols]
        out_ref[:, cols] = y.astype(out_ref.dtype)


def _pool_prompt_kernel(x_ref, prev_ref, w_ref, scale_ref, out_ref, ext_ref, *, tiles_per_batch, tm):
    t_in_b = pl.program_id(0) % tiles_per_batch
    ext_ref[0:POOL_PAD, :] = jnp.where(t_in_b > 0, prev_ref[...], 0.0)
    ext_ref[POOL_PAD:, :] = x_ref[...]
    _pool_body(ext_ref, tm, t_in_b * tm, w_ref, scale_ref, out_ref)


def _pool_prompt(u, w_grp, scale, seq, tm):
    n, c = u.shape
    tpb = seq // tm
    hb = tm // POOL_PAD
    return pl.pallas_call(
        functools.partial(_pool_prompt_kernel, tiles_per_batch=tpb, tm=tm),
        out_shape=jax.ShapeDtypeStruct((n, c), BF16),
        grid=(n // tm,),
        in_specs=[
            pl.BlockSpec((tm, c), lambda i: (i, 0)),
            pl.BlockSpec((POOL_PAD, c), lambda i: (jnp.maximum(i * hb - 1, 0), 0)),
            pl.BlockSpec(w_grp.shape, lambda i: (0, 0, 0)),
            pl.BlockSpec((1, c), lambda i: (0, 0)),
        ],
        out_specs=pl.BlockSpec((tm, c), lambda i: (i, 0)),
        scratch_shapes=[pltpu.VMEM((POOL_PAD + tm, c), F32)],
        compiler_params=_cparams("parallel"),
        name="pool_prompt",
    )(u, u, w_grp, scale)


def _pool_sample_kernel(ext_ref, w_ref, scale_ref, out_ref, *, pos0, t_rows):
    _pool_body(ext_ref.at[0], t_rows, pos0, w_ref, scale_ref, out_ref.at[0])


def _pool_sample(ext, w_grp, scale, pos0):
    b, rows, c = ext.shape
    t_rows = rows - POOL_PAD
    return pl.pallas_call(
        functools.partial(_pool_sample_kernel, pos0=pos0, t_rows=t_rows),
        out_shape=jax.ShapeDtypeStruct((b, t_rows, c), BF16),
        grid=(b,),
        in_specs=[
            pl.BlockSpec((1, rows, c), lambda i: (i, 0, 0)),
            pl.BlockSpec(w_grp.shape, lambda i: (0, 0, 0)),
            pl.BlockSpec((1, c), lambda i: (0, 0)),
        ],
        out_specs=pl.BlockSpec((1, t_rows, c), lambda i: (i, 0, 0)),
        compiler_params=_cparams("parallel"),
        name="pool_sample",
    )(ext, w_grp, scale)


def _chunk_rows(ref2d, rows):
    n = rows // CMP_STRIDE
    return jnp.concatenate([ref2d[pl.ds(r, n, stride=CMP_STRIDE), :] for r in range(CMP_STRIDE)], axis=1)


def _cmp_partial(a, pos_ref, w_ref, p_ref):
    hid2 = w_ref.shape[2] // 2
    p_ref[0, 0, :, 0:hid2] = _dot((a + pos_ref[0, 0:1, :]).astype(BF16), w_ref[0, :, 0:hid2])
    p_ref[0, 0, :, hid2:] = _dot((a + pos_ref[0, 1:2, :]).astype(BF16), w_ref[0, :, hid2:])


def _cmp_rows_kernel(x_ref, pos_ref, w_ref, p_ref, *, rows):
    _cmp_partial(_chunk_rows(x_ref.at[0, 0], rows), pos_ref, w_ref, p_ref)


def _cmp_rows(x4, pos_ab, w1_ab, nbatch, rows):
    nch = rows // CMP_STRIDE
    return pl.pallas_call(
        functools.partial(_cmp_rows_kernel, rows=rows),
        out_shape=jax.ShapeDtypeStruct((2, nbatch, nch, w1_ab.shape[2]), F32),
        grid=(2, nbatch),
        in_specs=[
            pl.BlockSpec((1, 1, rows, LANES), lambda s, b: (s, b, 0, 0)),
            pl.BlockSpec((1, 2, pos_ab.shape[2]), lambda s, b: (s, 0, 0)),
            pl.BlockSpec((1,) + w1_ab.shape[1:], lambda s, b: (s, 0, 0)),
        ],
        out_specs=pl.BlockSpec((1, 1, nch, w1_ab.shape[2]), lambda s, b: (s, b, 0, 0)),
        compiler_params=_cparams("parallel", "parallel"),
        name="cmp_rows",
    )(x4, pos_ab, w1_ab)


PAGES_PER_STEP = 16


def _cmp_pages_kernel(pt_ref, *refs):
    del pt_ref
    page_refs = refs[:PAGES_PER_STEP]
    pos_ref, w_ref, p_ref, rows_ref = refs[PAGES_PER_STEP:]
    for k, r in enumerate(page_refs):
        page_t = jnp.concatenate([r[0, 0, g] for g in range(N_KV)], axis=0)
        rows_ref[k * PAGE_SIZE:(k + 1) * PAGE_SIZE, :] = page_t.T
    _cmp_partial(_chunk_rows(rows_ref, PAGES_PER_STEP * PAGE_SIZE), pos_ref, w_ref, p_ref)


def _cmp_pages(cache_t, page_table_flat, n_pages, pos_ab, w1_ab, nbatch):
    steps = n_pages // PAGES_PER_STEP
    nch = PAGES_PER_STEP * PAGE_SIZE // CMP_STRIDE

    def page_spec(k):
        return pl.BlockSpec(
            (1, 1, N_KV, HEAD_DIM, PAGE_SIZE),
            lambda s, b, j, pt: (s, pt[b * n_pages + j * PAGES_PER_STEP + k], 0, 0, 0))

    grid_spec = pltpu.PrefetchScalarGridSpec(
        num_scalar_prefetch=1,
        grid=(2, nbatch, steps),
        in_specs=[page_spec(k) for k in range(PAGES_PER_STEP)] + [
            pl.BlockSpec((1, 2, pos_ab.shape[2]), lambda s, b, j, pt: (s, 0, 0)),
            pl.BlockSpec((1,) + w1_ab.shape[1:], lambda s, b, j, pt: (s, 0, 0)),
        ],
        out_specs=pl.BlockSpec((1, 1, nch, w1_ab.shape[2]), lambda s, b, j, pt: (s, b, j, 0)),
        scratch_shapes=[pltpu.VMEM((PAGES_PER_STEP * PAGE_SIZE, LANES), F32)],
    )
    return pl.pallas_call(
        _cmp_pages_kernel,
        out_shape=jax.ShapeDtypeStruct((2, nbatch, steps * nch, w1_ab.shape[2]), F32),
        grid_spec=grid_spec,
        compiler_params=_cparams("parallel", "parallel", "parallel"),
        name="cmp_pages",
    )(page_table_flat, *([cache_t] * PAGES_PER_STEP), pos_ab, w1_ab)


def _gelu_tanh(x):
    return x * (0.5 * (1.0 + jnp.tanh(np.sqrt(2.0 / np.pi).astype(np.float32) * (x + 0.044715 * (x * x * x)))))


def _cmp_combine_kernel(p_ref, w2_ref, w2t_ref, o_ref, ot_ref, pb_ref, *, n_p):
    hid2 = p_ref.shape[3] // 2
    pb_ref[0:n_p, :] = p_ref[0, 0, :, hid2:]
    pb_ref[n_p:n_p + 8, :] = jnp.zeros((8, hid2), F32)
    hsum = p_ref[0, 0, :, 0:hid2] + pb_ref[1:n_p + 1, :]
    act = _gelu_tanh(hsum).astype(BF16)
    o_ref[0, 0] = _dot(act, w2_ref[0]).astype(o_ref.dtype)
    ot_ref[0, 0] = _dot_nt(w2t_ref[0], act).astype(ot_ref.dtype)


def _cmp_combine(p, w2_bd):
    _, nbatch, n_p, width = p.shape
    w2t_bd = w2_bd.transpose(0, 2, 1)
    return pl.pallas_call(
        functools.partial(_cmp_combine_kernel, n_p=n_p),
        out_shape=(jax.ShapeDtypeStruct((2, nbatch, n_p, LANES), BF16),
                   jax.ShapeDtypeStruct((2, nbatch, LANES, n_p), BF16)),
        grid=(2, nbatch),
        in_specs=[
            pl.BlockSpec((1, 1, n_p, width), lambda s, b: (s, b, 0, 0)),
            pl.BlockSpec((1,) + w2_bd.shape[1:], lambda s, b: (s, 0, 0)),
            pl.BlockSpec((1,) + w2t_bd.shape[1:], lambda s, b: (s, 0, 0)),
        ],
        out_specs=(pl.BlockSpec((1, 1, n_p, LANES), lambda s, b: (s, b, 0, 0)),
                   pl.BlockSpec((1, 1, LANES, n_p), lambda s, b: (s, b, 0, 0))),
        scratch_shapes=[pltpu.VMEM((n_p + 8, width // 2), F32)],
        compiler_params=_cparams("parallel", "parallel"),
        name="cmp_combine",
    )(p, w2_bd, w2t_bd)


def _cmp_weights(cmp_pos, cmp_w1, cmp_w2):
    hid = cmp_w1.shape[2]
    half = CMP_STRIDE * HEAD_DIM
    eye = jnp.eye(N_KV, dtype=F32)
    pos_ab = jnp.tile(cmp_pos.reshape(2, 2, CMP_STRIDE, 1, HEAD_DIM), (1, 1, 1, N_KV, 1))
    pos_ab = pos_ab.reshape(2, 2, CMP_STRIDE * N_KV * HEAD_DIM)
    w1 = cmp_w1.reshape(2, 2, CMP_STRIDE, HEAD_DIM, hid)
    w1_bd = jnp.einsum("shrdj,gk->shrgdkj", w1, eye)
    w1_bd = w1_bd.reshape(2, 2, CMP_STRIDE * N_KV * HEAD_DIM, N_KV * hid)
    w1_ab = jnp.concatenate([w1_bd[:, 0], w1_bd[:, 1]], axis=2).astype(BF16)
    w2_bd = jnp.einsum("sjd,gk->sgjkd", cmp_w2, eye).reshape(2, N_KV * hid, N_KV * HEAD_DIM).astype(BF16)
    del half
    return pos_ab, w1_ab, w2_bd


def _online_update(s, v_aug_t, m_ref, acc_ref, h):
    m_old = m_ref[h]
    m_new = jnp.maximum(m_old, jnp.max(s, axis=-1, keepdims=True))
    p = jnp.exp(s - m_new).astype(BF16)
    acc_ref[h] = jnp.exp(m_old - m_new) * acc_ref[h] + _dot_nt(p, v_aug_t)
    m_ref[h] = m_new


def _attn_prompt_kernel(q_ref, cmpt_ref, kst_ref, vst_ref, kwt_ref, vwt_ref, gate_ref, mt_ref, e_ref,
                        o_ref, kaug_s, vaug_s, vwaug_s, m_s, acc_s, *, tq, nblk):
    i = pl.program_id(1)
    s0 = i * tq
    seq = kst_ref.shape[4]
    ncmp = cmpt_ref.shape[3]

    @pl.when(i == 0)
    def _():
        ones_row = (lax.broadcasted_iota(jnp.int32, (HEAD_DIM, seq), 0) == 0).astype(BF16)
        for g in range(N_KV):
            kaug_s[g, 0:LANES, :] = e_ref[...]
            kaug_s[g, LANES:LANES + HEAD_DIM, :] = kst_ref[0, 0, g]
            vaug_s[g, 0:HEAD_DIM, :] = vst_ref[0, 0, g]
            vaug_s[g, HEAD_DIM:, :] = ones_row
            vwaug_s[g, 0:HEAD_DIM, :] = vwt_ref[0, 0, g]
            vwaug_s[g, HEAD_DIM:, :] = ones_row

    row = lax.broadcasted_iota(jnp.int32, (tq, 1), 0)
    col = lax.broadcasted_iota(jnp.int32, (1, tq), 1)
    qpos = s0 + row
    causal = col <= row
    upper = col > row
    gates = gate_ref[...]
    c_end = lax.broadcasted_iota(jnp.int32, (1, ncmp), 1) * CMP_STRIDE + (CMP_BLOCK - 1)
    mask_c = c_end <= qpos
    qp_l = s0 + col
    cur = qp_l // SEL_BLOCK
    n_rb = -(-nblk // 8)

    for g in range(N_KV):
        heads = [GROUP_R * g + h for h in range(GROUP_R)]
        qs = [q_ref[:, HEAD_DIM * hh:HEAD_DIM * (hh + 1)] for hh in heads]

        kct_g = cmpt_ref[0, 0, HEAD_DIM * g:HEAD_DIM * (g + 1), :]
        vct_g = cmpt_ref[1, 0, HEAD_DIM * g:HEAD_DIM * (g + 1), :]
        p_sum = jnp.zeros((tq, ncmp), F32)
        o_cmp = []
        for qh in qs:
            s = jnp.where(mask_c, _dot(qh, kct_g), -jnp.inf)
            m = jnp.max(s, axis=-1, keepdims=True)
            m = jnp.where(m == -jnp.inf, 0.0, m)
            e = jnp.exp(s - m)
            d = jnp.sum(e, axis=-1, keepdims=True)
            p = e / jnp.where(d > 0, d, 1.0)
            p_sum = p_sum + p
            o_cmp.append(_dot_nt(p.astype(BF16), vct_g))

        p_hi = p_sum.astype(BF16)
        p_lo = (p_sum - p_hi.astype(F32)).astype(BF16)
        imp = _dot_nt(mt_ref[...], p_hi) + _dot_nt(mt_ref[...], p_lo)
        score, valid = [], []
        for r in range(n_rb):
            jb = 8 * r + lax.broadcasted_iota(jnp.int32, (8, 1), 0)
            ok = (jb * SEL_BLOCK <= qp_l) & (jb < nblk)
            forced = (jb == 0) | (jb == cur) | (jb == cur - 1)
            valid.append(ok)
            score.append(jnp.where(ok, jnp.where(forced, jnp.inf, imp[8 * r:8 * r + 8, :]), -jnp.inf))
        cnt = [jnp.zeros((8, tq), jnp.int32) for _ in range(n_rb)]
        for j in range(nblk):
            rj = jnp.broadcast_to(score[j // 8][j % 8:j % 8 + 1, :], (8, tq))
            for r in range(n_rb):
                if 8 * r > j:
                    beats = rj >= score[r]
                elif 8 * r + 7 < j:
                    beats = rj > score[r]
                else:
                    later = 8 * r + lax.broadcasted_iota(jnp.int32, (8, 1), 0) > j
                    beats = (rj > score[r]) | ((rj == score[r]) & later)
                cnt[r] = cnt[r] + jnp.where(beats, 1, 0)
        selneg_t = [jnp.where((cnt[r] < N_SEL) & valid[r], 0.0, NEG) for r in range(n_rb)]
        selneg_t.append(jnp.zeros((LANES - 8 * n_rb, tq), F32))
        selneg = jnp.concatenate(selneg_t, axis=0).T.astype(BF16)

        q_aug = [jnp.concatenate([selneg, qh], axis=1) for qh in qs]
        for h in range(GROUP_R):
            m_s[h] = jnp.full((tq, 1), -3e38, F32)
            acc_s[h] = jnp.zeros((tq, LANES), F32)

        def sel_chunk(c, diag):
            c0 = pl.multiple_of(c * tq, tq)
            k_aug = kaug_s[g, :, pl.ds(c0, tq)]
            v_aug = vaug_s[g, :, pl.ds(c0, tq)]
            for h in range(GROUP_R):
                s = _dot(q_aug[h], k_aug)
                if diag:
                    s = jnp.where(causal, s, NEG)
                _online_update(s, v_aug, m_s, acc_s, h)

        def sel_body(c, carry):
            sel_chunk(c, False)
            return carry

        lax.fori_loop(0, i, sel_body, 0)
        sel_chunk(i, True)
        o_sel = []
        for h in range(GROUP_R):
            acc = acc_s[h]
            o_sel.append(acc[:, 0:HEAD_DIM] / acc[:, HEAD_DIM:HEAD_DIM + 1])

        for h in range(GROUP_R):
            m_s[h] = jnp.full((tq, 1), -3e38, F32)
            acc_s[h] = jnp.zeros((tq, LANES), F32)
        n_back = -(-WINDOW // tq)
        for back in range(n_back + 1):
            c = i - back
            c0 = pl.multiple_of(jnp.maximum(c, 0) * tq, tq)
            k_t = kwt_ref[0, 0, g, :, pl.ds(c0, tq)]
            v_aug = vwaug_s[g, :, pl.ds(c0, tq)]
            if back == 0:
                keep = causal
            elif back * tq == WINDOW:
                keep = upper & (c >= 0)
            else:
                keep = (col >= 0) & (c >= 0)
            for h, qh in enumerate(qs):
                s = jnp.where(keep, _dot(qh, k_t), NEG)
                _online_update(s, v_aug, m_s, acc_s, h)
        for h, hh in enumerate(heads):
            acc = acc_s[h]
            o_win = acc[:, 0:HEAD_DIM] / acc[:, HEAD_DIM:HEAD_DIM + 1]
            o = (o_cmp[h] * gates[:, 3 * hh:3 * hh + 1] + o_sel[h] * gates[:, 3 * hh + 1:3 * hh + 2]
                 + o_win * gates[:, 3 * hh + 2:3 * hh + 3])
            o_ref[:, HEAD_DIM * hh:HEAD_DIM * (hh + 1)] = o.astype(o_ref.dtype)


def _attn_prompt(q, cmp_t, kvbt, gates, mt, e_mat, nbatch, seq, tq):
    assert WINDOW % tq == 0
    n = q.shape[0]
    nq = seq // tq
    ncmp = cmp_t.shape[3]
    kv_spec = lambda slot: pl.BlockSpec((1, 1, N_KV, HEAD_DIM, seq), lambda b, i: (slot, b, 0, 0, 0))
    return pl.pallas_call(
        functools.partial(_attn_prompt_kernel, tq=tq, nblk=seq // SEL_BLOCK),
        out_shape=jax.ShapeDtypeStruct((n, N_HEADS * HEAD_DIM), BF16),
        grid=(nbatch, nq),
        in_specs=[
            pl.BlockSpec((tq, N_HEADS * HEAD_DIM), lambda b, i: (b * nq + i, 0)),
            pl.BlockSpec((2, 1, LANES, ncmp), lambda b, i: (0, b, 0, 0)),
            kv_spec(0), kv_spec(1), kv_spec(2), kv_spec(3),
            pl.BlockSpec((tq, LANES), lambda b, i: (b * nq + i, 0)),
            pl.BlockSpec(mt.shape, lambda b, i: (0, 0)),
            pl.BlockSpec(e_mat.shape, lambda b, i: (0, 0)),
        ],
        out_specs=pl.BlockSpec((tq, N_HEADS * HEAD_DIM), lambda b, i: (b * nq + i, 0)),
        scratch_shapes=[
            pltpu.VMEM((N_KV, LANES + HEAD_DIM, seq), BF16),
            pltpu.VMEM((N_KV, LANES, seq), BF16),
            pltpu.VMEM((N_KV, LANES, seq), BF16),
            pltpu.VMEM((GROUP_R, tq, 1), F32),
            pltpu.VMEM((GROUP_R, tq, LANES), F32),
        ],
        compiler_params=_cparams("arbitrary", "arbitrary"),
        name="attn_prompt",
    )(q, cmp_t, kvbt, kvbt, kvbt, kvbt, gates, mt, e_mat)


def _imp_matrix(ncmp, nblk_pad):
    per = SEL_BLOCK // CMP_STRIDE
    pad = CMP_BLOCK // CMP_STRIDE - 1
    n = np.arange(ncmp)[:, None]
    j = np.arange(nblk_pad)[None, :]
    return ((n >= per * j - pad) & (n <= per * j + per - 1)).astype(np.float32)


def _attn_sample_cmp_kernel(q_ref, kc_ref, vc_ref, m_ref, oc_ref, idx_ref, *, n_valid_cmp, qpos0, nblk):
    ncmp = kc_ref.shape[2]
    rows = q_ref.shape[2]
    t_row = lax.broadcasted_iota(jnp.int32, (rows, 1), 0) % 8
    c_idx = lax.broadcasted_iota(jnp.int32, (1, ncmp), 1)
    mask_c = (c_idx * CMP_STRIDE + (CMP_BLOCK - 1) <= qpos0 + t_row) & (c_idx < n_valid_cmp)
    p_tok = []
    for g in range(N_KV):
        gs = slice(HEAD_DIM * g, HEAD_DIM * (g + 1))
        s = jnp.where(mask_c, _dot_nt(q_ref[0, g], kc_ref[0, 0, :, gs]), -jnp.inf)
        m = jnp.max(s, axis=-1, keepdims=True)
        m = jnp.where(m == -jnp.inf, 0.0, m)
        e = jnp.exp(s - m)
        d = jnp.sum(e, axis=-1, keepdims=True)
        p = e / jnp.where(d > 0, d, 1.0)
        oc_ref[0, g] = _dot(p.astype(BF16), vc_ref[0, 0, :, gs])
        p_tok.append(jnp.sum(p.reshape(GROUP_R, 8, ncmp), axis=0))
    p_all = jnp.concatenate(p_tok, axis=0)
    p_hi = p_all.astype(BF16)
    p_lo = (p_all - p_hi.astype(F32)).astype(BF16)
    imp = _dot(p_hi, m_ref[...]) + _dot(p_lo, m_ref[...])
    nb_pad = imp.shape[1]
    jb = lax.broadcasted_iota(jnp.int32, (1, nb_pad), 1)
    qpos = qpos0 + lax.broadcasted_iota(jnp.int32, (2 * 8, 1), 0) % 8
    cur = qpos // SEL_BLOCK
    valid = (jb * SEL_BLOCK <= qpos) & (jb < nblk)
    forced = (jb == 0) | (jb == cur) | (jb == cur - 1)
    score = jnp.where(valid, jnp.where(forced, jnp.inf, imp), -jnp.inf)
    avail = jb < nblk
    lane = lax.broadcasted_iota(jnp.int32, (2 * 8, LANES), 1)
    picked = jnp.zeros((2 * 8, LANES), jnp.int32)
    for k in range(N_SEL):
        best = jnp.max(jnp.where(avail, score, -jnp.inf), axis=-1, keepdims=True)
        cand = avail & (score == best)
        idx = jnp.min(jnp.where(cand, jb, nb_pad), axis=-1, keepdims=True)
        picked = jnp.where(lane == k, idx, picked)
        avail = avail & (jb != idx)
    idx_ref[0] = picked


def _attn_sample_cmp(qg, cmp_kv, m_mat, n_valid_cmp, qpos0, nblk):
    nbatch = qg.shape[0]
    ncmp = cmp_kv.shape[2]
    rows = qg.shape[2]
    cmp_spec = lambda slot: pl.BlockSpec((1, 1, ncmp, LANES), lambda b: (slot, b, 0, 0))
    return pl.pallas_call(
        functools.partial(_attn_sample_cmp_kernel, n_valid_cmp=n_valid_cmp, qpos0=qpos0, nblk=nblk),
        out_shape=(
            jax.ShapeDtypeStruct((nbatch, N_KV, rows, HEAD_DIM), F32),
            jax.ShapeDtypeStruct((nbatch, 2 * 8, LANES), jnp.int32),
        ),
        grid=(nbatch,),
        in_specs=[
            pl.BlockSpec((1, N_KV, rows, HEAD_DIM), lambda b: (b, 0, 0, 0)),
            cmp_spec(0), cmp_spec(1),
            pl.BlockSpec(m_mat.shape, lambda b: (0, 0)),
        ],
        out_specs=(
            pl.BlockSpec((1, N_KV, rows, HEAD_DIM), lambda b: (b, 0, 0, 0)),
            pl.BlockSpec((1, 2 * 8, LANES), lambda b: (b, 0, 0)),
        ),
        compiler_params=_cparams("parallel"),
        name="attn_sample_cmp",
    )(qg, cmp_kv, cmp_kv, m_mat)


def _masked_softmax(s, mask):
    s = jnp.where(mask, s, -jnp.inf)
    m = jnp.max(s, axis=-1, keepdims=True)
    m = jnp.where(m == -jnp.inf, 0.0, m)
    e = jnp.exp(s - m)
    d = jnp.sum(e, axis=-1, keepdims=True)
    return e / jnp.where(d > 0, d, 1.0)


def _attn_sample_sel_kernel(idx_ref, pt_ref, *refs, dec_seq, qpos0, n_cache_blk, wb):
    del pt_ref
    nsel = N_KV * N_SEL
    per_page = PAGE_SIZE // SEL_BLOCK
    k_refs = refs[:nsel]
    v_refs = refs[nsel:2 * nsel]
    q_ref, tail_ref, kw_ref, vw_ref, wnew_ref, oc_ref, gate_ref, o_ref, osel_ref = refs[2 * nsel:]
    b = pl.program_id(0)
    t = pl.program_id(1)
    rows = q_ref.shape[2]
    t_row = lax.broadcasted_iota(jnp.int32, (rows, 1), 0) % 8
    qpos = qpos0 + t_row
    colk = lax.broadcasted_iota(jnp.int32, (1, N_SEL * PAGE_SIZE), 1)
    slot_of_col = colk // PAGE_SIZE
    row_in_page = colk % PAGE_SIZE

    @pl.when(t == 0)
    def _():
        osel_ref[...] = jnp.zeros(osel_ref.shape, F32)

    for g in range(N_KV):
        blk_of_col = jnp.zeros_like(colk)
        k_pages, v_pages = [], []
        for k in range(N_SEL):
            blk = idx_ref[((b * dec_seq + t) * N_KV + g) * N_SEL + k]
            is_new = blk >= n_cache_blk
            k_pages.append(jnp.where(is_new, tail_ref[0, 0, g], k_refs[g * N_SEL + k][0, 0, 0]))
            v_pages.append(jnp.where(is_new, tail_ref[1, 0, g], v_refs[g * N_SEL + k][0, 0, 0]))
            blk_of_col = blk_of_col + jnp.where(slot_of_col == k, blk, 0)
        in_block = row_in_page // SEL_BLOCK == blk_of_col % per_page
        kpos = blk_of_col * SEL_BLOCK + row_in_page % SEL_BLOCK
        k_t = jnp.concatenate(k_pages, axis=1).astype(BF16)
        v_t = jnp.concatenate(v_pages, axis=1).astype(BF16)
        p = _masked_softmax(_dot(q_ref[0, g], k_t), in_block & (kpos <= qpos))
        o = _dot_nt(p.astype(BF16), v_t)
        osel_ref[g] = osel_ref[g] + jnp.where(t_row == t, o, 0.0)

    @pl.when(t == dec_seq - 1)
    def _():
        colw = lax.broadcasted_iota(jnp.int32, (1, wb + LANES), 1)
        kwpos = jnp.where(colw < wb, qpos0 - wb + colw, qpos0 + colw - wb)
        dpos = qpos - kwpos
        mask_w = (dpos >= 0) & (dpos < WINDOW) & (kwpos >= 0) & (colw < wb + dec_seq)
        for g in range(N_KV):
            k_t = jnp.concatenate([kw_ref[0, 0, g], wnew_ref[0, 0, g]], axis=1).astype(BF16)
            v_t = jnp.concatenate([vw_ref[0, 0, g], wnew_ref[1, 0, g]], axis=1).astype(BF16)
            p = _masked_softmax(_dot(q_ref[0, g], k_t), mask_w)
            o_win = _dot_nt(p.astype(BF16), v_t)
            gt = gate_ref[0, g]
            o_ref[0, g] = oc_ref[0, g] * gt[:, 0:1] + osel_ref[g] * gt[:, 1:2] + o_win * gt[:, 2:3]


def _attn_sample_sel(idx_flat, pt_flat, cache_t, qg, tail_t, win_t, wnew_t, o_cmp, gates_g, dec_seq, n_pages,
                     qpos0):
    nbatch = qg.shape[0]
    rows = qg.shape[2]
    wb = win_t.shape[4]
    per_page = PAGE_SIZE // SEL_BLOCK
    n_cache_blk = n_pages * per_page

    def page_spec(slot, g, k):
        def imap(b, t, idx, pt):
            blk = jnp.minimum(idx[((b * dec_seq + t) * N_KV + g) * N_SEL + k], n_cache_blk - 1)
            return (slot, pt[b * n_pages + blk // per_page], g, 0, 0)
        return pl.BlockSpec((1, 1, 1, HEAD_DIM, PAGE_SIZE), imap)

    k_specs = [page_spec(2, g, k) for g in range(N_KV) for k in range(N_SEL)]
    v_specs = [page_spec(3, g, k) for g in range(N_KV) for k in range(N_SEL)]
    per_b = lambda shape: pl.BlockSpec((1,) + shape, lambda b, t, idx, pt: (b,) + (0,) * len(shape))
    new_spec = pl.BlockSpec((2, 1, N_KV, HEAD_DIM, LANES), lambda b, t, idx, pt: (0, b, 0, 0, 0))
    win_spec = lambda s: pl.BlockSpec((1, 1, N_KV, HEAD_DIM, wb), lambda b, t, idx, pt: (s, b, 0, 0, 0))
    grid_spec = pltpu.PrefetchScalarGridSpec(
        num_scalar_prefetch=2,
        grid=(nbatch, dec_seq),
        in_specs=k_specs + v_specs + [
            per_b((N_KV, rows, HEAD_DIM)),
            new_spec, win_spec(0), win_spec(1), new_spec,
            per_b((N_KV, rows, HEAD_DIM)),
            per_b((N_KV, rows, LANES)),
        ],
        out_specs=per_b((N_KV, rows, HEAD_DIM)),
        scratch_shapes=[pltpu.VMEM((N_KV, rows, HEAD_DIM), F32)],
    )
    n_page_specs = 2 * N_KV * N_SEL
    return pl.pallas_call(
        functools.partial(_attn_sample_sel_kernel, dec_seq=dec_seq, qpos0=qpos0, n_cache_blk=n_cache_blk, wb=wb),
        out_shape=jax.ShapeDtypeStruct((nbatch, N_KV, rows, HEAD_DIM), F32),
        grid_spec=grid_spec,
        compiler_params=_cparams("parallel", "arbitrary"),
        name="attn_sample_sel",
    )(idx_flat, pt_flat, *([cache_t] * n_page_specs), qg, tail_t, win_t, win_t, wnew_t, o_cmp, gates_g)


FFN_CHUNK = 256


def _post_kernel(*refs, n_a, final):
    y_ref = refs[0]
    a_refs = refs[1:1 + 2 * n_a]
    gmix_ref, g2_ref, sh_ref, sc_ref, gffn_ref, w1_ref, w2_ref = refs[1 + 2 * n_a:8 + 2 * n_a]
    rest = refs[8 + 2 * n_a:]
    out_ref = rest[-1]
    mix = _dot(a_refs[0][...], a_refs[1][...])
    for k in range(1, n_a):
        mix = mix + _dot(a_refs[2 * k][...], a_refs[2 * k + 1][...])
    y1 = y_ref[...] + gmix_ref[0] * mix
    h = _modulate(y1, g2_ref[...], sh_ref[0], sc_ref[0]).astype(BF16)
    d_ff = w2_ref.shape[0]
    acc = jnp.zeros(y1.shape, F32)
    for c in range(d_ff // FFN_CHUNK):
        c0 = c * FFN_CHUNK
        gate = _dot(h, w1_ref[:, c0:c0 + FFN_CHUNK])
        up = _dot(h, w1_ref[:, d_ff + c0:d_ff + c0 + FFN_CHUNK])
        acc = acc + _dot((_silu(gate) * up).astype(BF16), w2_ref[c0:c0 + FFN_CHUNK, :])
    y2 = y1 + gffn_ref[0] * acc
    if final:
        fg_ref = rest[0]
        ms = jnp.mean(y2 * y2, axis=-1, keepdims=True)
        y2 = (y2 * lax.rsqrt(ms + NORM_EPS)) * fg_ref[...]
    out_ref[...] = y2


def _post(y, a_list, wo_list, gmix, g2, shift, scale, gffn, w1, w2, final_g, tm, rows_per_mod):
    n, d = y.shape
    r = shift.shape[1]
    mod_spec = pl.BlockSpec((1, r, d), lambda i: (i // rows_per_mod, 0, 0))
    const = lambda arr: pl.BlockSpec(arr.shape, lambda i: (0,) * arr.ndim, pipeline_mode=pl.Buffered(1))
    in_specs = [pl.BlockSpec((tm, d), lambda i: (i, 0))]
    args = [y]
    for a, wo in zip(a_list, wo_list):
        in_specs += [pl.BlockSpec((tm, a.shape[1]), lambda i: (i, 0)), const(wo)]
        args += [a, wo]
    in_specs += [mod_spec, pl.BlockSpec((1, d), lambda i: (0, 0)), mod_spec, mod_spec, mod_spec, const(w1), const(w2)]
    args += [gmix, g2, shift, scale, gffn, w1, w2]
    if final_g is not None:
        in_specs.append(pl.BlockSpec((1, d), lambda i: (0, 0)))
        args.append(final_g)
    return pl.pallas_call(
        functools.partial(_post_kernel, n_a=len(a_list), final=final_g is not None),
        out_shape=jax.ShapeDtypeStruct((n, d), F32),
        grid=(n // tm,),
        in_specs=in_specs,
        out_specs=pl.BlockSpec((tm, d), lambda i: (i, 0)),
        compiler_params=_cparams("parallel"),
        name="post_proj_ffn",
    )(*args)


def _pre1_kernel(x_ref, g_ref, sh_ref, sc_ref, w_ref, u_ref):
    h = _modulate(x_ref[...], g_ref[...], sh_ref[0], sc_ref[0]).astype(BF16)
    d = u_ref.shape[1]
    a = _dot(h, w_ref[:, 0:d])
    b = _dot(h, w_ref[:, d:2 * d])
    u_ref[...] = a * _sigmoid(b)


def _pre1(x2, g, shift, scale, pw1, tm, rows_per_mod):
    n, d = x2.shape
    r = shift.shape[1]
    mod_spec = pl.BlockSpec((1, r, d), lambda i: (i // rows_per_mod, 0, 0))
    return pl.pallas_call(
        _pre1_kernel,
        out_shape=jax.ShapeDtypeStruct((n, d), F32),
        grid=(n // tm,),
        in_specs=[
            pl.BlockSpec((tm, d), lambda i: (i, 0)),
            pl.BlockSpec((1, d), lambda i: (0, 0)),
            mod_spec, mod_spec,
            pl.BlockSpec(pw1.shape, lambda i: (0, 0)),
        ],
        out_specs=pl.BlockSpec((tm, d), lambda i: (i, 0)),
        compiler_params=_cparams("parallel"),
        name="pre1_pw_glu",
    )(x2, g, shift, scale, pw1)


def _conv_body(ext_ref, t_rows, dw_ref, dwb_ref, lng_ref, lnb_ref, out_ref):
    off = CONV_PAD - CONV_HIST
    acc = ext_ref[off:off + t_rows, :] * dw_ref[0:1, :]
    for k in range(1, CONV_WIDTH):
        acc = acc + ext_ref[off + k:off + k + t_rows, :] * dw_ref[k:k + 1, :]
    y = acc + dwb_ref[...]
    mu = jnp.mean(y, axis=-1, keepdims=True)
    var = jnp.mean(jnp.square(y - mu), axis=-1, keepdims=True)
    z = (y - mu) * lax.rsqrt(var + NORM_EPS) * lng_ref[...] + lnb_ref[...]
    out_ref[...] = _silu(z).astype(out_ref.dtype)


def _conv_prompt_kernel(x_ref, prev_ref, dw_ref, dwb_ref, lng_ref, lnb_ref, out_ref, ext_ref, *,
                        tiles_per_batch, tm):
    t_in_b = pl.program_id(0) % tiles_per_batch
    ext_ref[0:CONV_PAD, :] = jnp.where(t_in_b > 0, prev_ref[...], 0.0)
    ext_ref[CONV_PAD:, :] = x_ref[...]
    _conv_body(ext_ref, tm, dw_ref, dwb_ref, lng_ref, lnb_ref, out_ref)


def _conv_prompt(u, dw, dwb, lng, lnb, seq, tm):
    n, d = u.shape
    tpb = seq // tm
    hb = tm // CONV_PAD
    vec = pl.BlockSpec((1, d), lambda i: (0, 0))
    return pl.pallas_call(
        functools.partial(_conv_prompt_kernel, tiles_per_batch=tpb, tm=tm),
        out_shape=jax.ShapeDtypeStruct((n, d), BF16),
        grid=(n // tm,),
        in_specs=[
            pl.BlockSpec((tm, d), lambda i: (i, 0)),
            pl.BlockSpec((CONV_PAD, d), lambda i: (jnp.maximum(i * hb - 1, 0), 0)),
            pl.BlockSpec(dw.shape, lambda i: (0, 0)),
            vec, vec, vec,
        ],
        out_specs=pl.BlockSpec((tm, d), lambda i: (i, 0)),
        scratch_shapes=[pltpu.VMEM((CONV_PAD + tm, d), F32)],
        compiler_params=_cparams("parallel"),
        name="conv_prompt",
    )(u, u, dw, dwb, lng, lnb)


def _conv_sample_kernel(ext_ref, dw_ref, dwb_ref, lng_ref, lnb_ref, out_ref, *, t_rows):
    _conv_body(ext_ref.at[0], t_rows, dw_ref, dwb_ref, lng_ref, lnb_ref, out_ref.at[0])


def _conv_sample(ext, dw, dwb, lng, lnb):
    b, rows, d = ext.shape
    t_rows = rows - CONV_PAD
    vec = pl.BlockSpec((1, d), lambda i: (0, 0))
    return pl.pallas_call(
        functools.partial(_conv_sample_kernel, t_rows=t_rows),
        out_shape=jax.ShapeDtypeStruct((b, t_rows, d), BF16),
        grid=(b,),
        in_specs=[
            pl.BlockSpec((1, rows, d), lambda i: (i, 0, 0)),
            pl.BlockSpec(dw.shape, lambda i: (0, 0)),
            vec, vec, vec,
        ],
        out_specs=pl.BlockSpec((1, t_rows, d), lambda i: (i, 0, 0)),
        compiler_params=_cparams("parallel"),
        name="conv_sample",
    )(ext, dw, dwb, lng, lnb)


ROW_TILE = 512
ATTN_TILE = 256


def _group_rows(x, nbatch, dec_seq):
    w = x.shape[1] // N_HEADS
    x = x.reshape(nbatch, dec_seq, N_KV, GROUP_R, w).transpose(0, 2, 3, 1, 4)
    x = jnp.pad(x, ((0, 0), (0, 0), (0, 0), (0, 8 - dec_seq), (0, 0)))
    return x.reshape(nbatch, N_KV, GROUP_R * 8, w)


def _ungroup_rows(x, nbatch, dec_seq):
    w = x.shape[3]
    x = x.reshape(nbatch, N_KV, GROUP_R, 8, w)[:, :, :, :dec_seq]
    return x.transpose(0, 3, 1, 2, 4).reshape(nbatch * dec_seq, N_HEADS * w)


def kernel(x_prompt, x_sample, cache_kv, page_table, cache_win, state_pool, state_conv, c_prompt, c_sample,
           ada_w, ada_b, norm_g, attn_w_in, attn_w_out, pool_w, pool_scale, cmp_pos, cmp_w1, cmp_w2,
           conv_pw1, conv_dw, conv_dw_b, conv_ln_g, conv_ln_b, conv_pw2, ffn_w1, ffn_w2, final_g):
    nb_p, seq, d = x_prompt.shape
    nb_s, dec_seq, _ = x_sample.shape
    n_p, n_s = nb_p * seq, nb_s * dec_seq
    n_pages = page_table.shape[1]
    past = n_pages * PAGE_SIZE
    wb = cache_win.shape[3]
    tm = min(ROW_TILE, seq)
    tq = min(ATTN_TILE, seq)
    pool_width = pool_w.shape[1] * pool_w.shape[2]
    att_width = N_HEADS * HEAD_DIM

    mods = _ada(jnp.concatenate([c_prompt, c_sample], axis=0), ada_w, ada_b)

    def mod_p(layer, k):
        return mods[layer, :nb_p, k * d:(k + 1) * d].reshape(nb_p, 1, d)

    def mod_s(layer, k):
        return jnp.repeat(mods[layer, nb_p:, k * d:(k + 1) * d], dec_seq, axis=0).reshape(1, n_s, d)

    y_p = x_prompt.reshape(n_p, d)
    y_s = x_sample.reshape(n_s, d)
    row = lambda v: v.reshape(1, -1)

    w_in = attn_w_in[0]
    w_in_pad = jnp.pad(w_in, ((0, 0), (0, (-w_in.shape[1]) % LANES))).astype(BF16)
    cos_p, sin_p = _rope_tables(jnp.arange(seq))
    cos_s, sin_s = _rope_tables(past + jnp.arange(n_s) % dec_seq)
    g0 = row(norm_g[0, 0])
    u_p, q_p, gate_p, kvt_p, wint_p, kvbt_p, cmp_in_p = _pre0(
        y_p, g0, mod_p(0, 0), mod_p(0, 1), w_in_pad, cos_p, sin_p, tm, seq // tm, seq // tm, seq=seq)
    u_s, q_s, gate_s, kv_s, win_s = _pre0(y_s, g0, mod_s(0, 0), mod_s(0, 1), w_in_pad, cos_s, sin_s, n_s, 1, 1)

    pool_w_b = pool_w[0].astype(BF16)
    pool_sc = row(pool_scale[0])
    ypool_p = _pool_prompt(u_p, pool_w_b, pool_sc, seq, tm)
    u_s3 = u_s.reshape(nb_s, dec_seq, pool_width)
    pool_ext = jnp.concatenate([
        jnp.zeros((nb_s, POOL_PAD - POOL_HIST, pool_width), F32), state_pool[0], u_s3,
        jnp.zeros((nb_s, 8 - dec_seq, pool_width), F32)], axis=1)
    ypool_s = _pool_sample(pool_ext, pool_w_b, pool_sc, past)[:, :dec_seq].reshape(n_s, pool_width)

    pos_ab, w1_ab, w2_bd = _cmp_weights(cmp_pos[0], cmp_w1[0], cmp_w2[0])
    cmp_in_p4 = cmp_in_p.reshape(2, nb_p, seq, LANES)
    _, cmp_t_p = _cmp_combine(_cmp_rows(cmp_in_p4, pos_ab, w1_ab, nb_p, seq), w2_bd)
    cache_t = cache_kv[0].transpose(0, 1, 3, 4, 2)
    pt_flat = page_table.reshape(-1)
    part_past = _cmp_pages(cache_t, pt_flat, n_pages, pos_ab, w1_ab, nb_s)
    tail_rows = 2 * PAGE_SIZE
    kv_s4 = kv_s.reshape(4, nb_s, dec_seq, LANES)
    tail4 = jnp.pad(kv_s4, ((0, 0), (0, 0), (0, tail_rows - dec_seq), (0, 0)))
    part_tail = _cmp_rows(tail4, pos_ab, w1_ab, nb_s, tail_rows)
    cmp_s, _ = _cmp_combine(jnp.concatenate([part_past, part_tail], axis=2), w2_bd)
    total_len = past + dec_seq
    padded_len = -(-total_len // SEL_BLOCK) * SEL_BLOCK
    n_cmp_s = padded_len // CMP_STRIDE - CMP_BLOCK // CMP_STRIDE + 1
    nblk_s = padded_len // SEL_BLOCK

    assert seq // SEL_BLOCK <= LANES and dec_seq <= SEL_BLOCK
    ncmp_p = cmp_t_p.shape[3]
    mt_p = jnp.asarray(_imp_matrix(ncmp_p, LANES).T, BF16)
    e_mat = jnp.asarray((np.arange(seq)[None, :] // SEL_BLOCK == np.arange(LANES)[:, None]).astype(np.float32),
                        BF16)
    o_p = _attn_prompt(q_p, cmp_t_p, kvbt_p, gate_p, mt_p, e_mat, nb_p, seq, tq)

    qg_s = _group_rows(q_s, nb_s, dec_seq)
    nblk_pad = -(-nblk_s // LANES) * LANES
    m_s = jnp.asarray(_imp_matrix(cmp_s.shape[2], nblk_pad), BF16)
    o_cmp_s, picked = _attn_sample_cmp(qg_s, cmp_s, m_s, n_cmp_s, past, nblk_s)
    idx = picked.reshape(nb_s, N_KV, 8, LANES)[:, :, :dec_seq, :N_SEL].transpose(0, 2, 1, 3).reshape(-1)

    def new_rows_t(rows):
        x = rows.reshape(rows.shape[0], nb_s, dec_seq, N_KV, HEAD_DIM).transpose(0, 1, 3, 4, 2)
        return jnp.pad(x, ((0, 0), (0, 0), (0, 0), (0, 0), (0, LANES - dec_seq)))

    tail_t = new_rows_t(kv_s[2:4])
    wnew_t = new_rows_t(win_s)
    win_t = cache_win[0].transpose(0, 1, 3, 4, 2)
    gates_g = jnp.pad(_group_rows(gate_s[:, :3 * N_HEADS], nb_s, dec_seq), ((0, 0), (0, 0), (0, 0), (0, LANES - 3)))
    o_s = _attn_sample_sel(idx, pt_flat, cache_t, qg_s, tail_t, win_t, wnew_t, o_cmp_s, gates_g,
                           dec_seq, n_pages, past)
    o_s = _ungroup_rows(o_s, nb_s, dec_seq).astype(BF16)

    w_out = attn_w_out[0].astype(BF16)
    wo_list = [w_out[:pool_width], w_out[pool_width:]]
    g1 = row(norm_g[0, 1])
    ffn1_0, ffn2_0 = ffn_w1[0].astype(BF16), ffn_w2[0].astype(BF16)
    y_p = _post(y_p, [ypool_p, o_p], wo_list, mod_p(0, 2), g1, mod_p(0, 3), mod_p(0, 4), mod_p(0, 5),
                ffn1_0, ffn2_0, None, tm, seq // tm)
    y_s = _post(y_s, [ypool_s, o_s], wo_list, mod_s(0, 2), g1, mod_s(0, 3), mod_s(0, 4), mod_s(0, 5),
                ffn1_0, ffn2_0, None, n_s, 1)

    g0 = row(norm_g[1, 0])
    pw1 = conv_pw1[0].astype(BF16)
    uc_p = _pre1(y_p, g0, mod_p(1, 0), mod_p(1, 1), pw1, tm, seq // tm)
    uc_s = _pre1(y_s, g0, mod_s(1, 0), mod_s(1, 1), pw1, n_s, 1)
    dwb, lng, lnb = row(conv_dw_b[0]), row(conv_ln_g[0]), row(conv_ln_b[0])
    cv_p = _conv_prompt(uc_p, conv_dw[0], dwb, lng, lnb, seq, tm)
    uc_s3 = uc_s.reshape(nb_s, dec_seq, d)
    conv_ext = jnp.concatenate([
        jnp.zeros((nb_s, CONV_PAD - CONV_HIST, d), F32), state_conv[0], uc_s3,
        jnp.zeros((nb_s, 8 - dec_seq, d), F32)], axis=1)
    cv_s = _conv_sample(conv_ext, conv_dw[0], dwb, lng, lnb)[:, :dec_seq].reshape(n_s, d)

    pw2 = conv_pw2[0].astype(BF16)
    g1 = row(norm_g[1, 1])
    ffn1_1, ffn2_1 = ffn_w1[1].astype(BF16), ffn_w2[1].astype(BF16)
    fg = row(final_g)
    y_p = _post(y_p, [cv_p], [pw2], mod_p(1, 2), g1, mod_p(1, 3), mod_p(1, 4), mod_p(1, 5),
                ffn1_1, ffn2_1, fg, tm, seq // tm)
    y_s = _post(y_s, [cv_s], [pw2], mod_s(1, 2), g1, mod_s(1, 3), mod_s(1, 4), mod_s(1, 5),
                ffn1_1, ffn2_1, fg, n_s, 1)

    y_prompt = y_p.reshape(nb_p, seq, d)
    y_sample = y_s.reshape(nb_s, dec_seq, d)
    kv_prompt = kvt_p.transpose(0, 1, 4, 2, 3)[None]
    kv_sample = kv_s.reshape(1, 4, nb_s, dec_seq, N_KV, HEAD_DIM)
    if seq >= wb:
        win_prompt_t = wint_p[..., seq - wb:]
    else:
        win_prompt_t = jnp.pad(wint_p, ((0, 0),) * 4 + ((wb - seq, 0),))
    win_prompt = win_prompt_t.transpose(0, 1, 4, 2, 3)
    win_sample_t = jnp.concatenate([win_t, wnew_t[..., :dec_seq]], axis=-1)[..., -wb:]
    win_sample = win_sample_t.transpose(0, 1, 4, 2, 3)

    def last_rows(x3, hist, state):
        full = x3 if state is None and x3.shape[1] >= hist else jnp.concatenate(
            [jnp.zeros((x3.shape[0], hist, x3.shape[2]), F32) if state is None else state, x3], axis=1)
        return full[:, full.shape[1] - hist:]

    pool_prompt = last_rows(u_p.reshape(nb_p, seq, pool_width), POOL_HIST, None)
    pool_sample = last_rows(u_s3, POOL_HIST, state_pool[0])
    conv_prompt = last_rows(uc_p.reshape(nb_p, seq, d), CONV_HIST, None)
    conv_sample = last_rows(uc_s3, CONV_HIST, state_conv[0])
    return (y_prompt, y_sample, kv_prompt, kv_sample, win_prompt[None], win_sample[None],
            pool_prompt[None], pool_sample[None], conv_prompt[None], conv_sample[None])
```

```python
import functools

import numpy as np
import jax
import jax.numpy as jnp
from jax import lax
from jax.experimental import pallas as pl
from jax.experimental.pallas import tpu as pltpu

F32 = jnp.float32
BF16 = jnp.bfloat16

NORM_EPS = 1e-6
N_HEADS = 8
HEAD_DIM = 64
N_KV = 2
GROUP_R = N_HEADS // N_KV
POOL_WINDOWS = (2, 4, 8, 16)
POOL_HIST = 15
POOL_PAD = 16
CMP_BLOCK = 32
CMP_STRIDE = 16
SEL_BLOCK = 64
N_SEL = 16
WINDOW = 512
PAGE_SIZE = 128
ROPE_THETA = 10000.0
CONV_WIDTH = 31
CONV_HIST = CONV_WIDTH - 1
CONV_PAD = 32
LANES = 128
NEG = -1e9
V7X_VMEM_LIMIT = 56 * 1024 * 1024


def _cparams(*sem):
    return pltpu.CompilerParams(dimension_semantics=sem, vmem_limit_bytes=V7X_VMEM_LIMIT)


def _dot(a, b):
    return jnp.dot(a, b, preferred_element_type=F32)


def _dot_nt(a, b):
    return lax.dot_general(a, b, (((1,), (1,)), ((), ())), preferred_element_type=F32)


def _sigmoid(x):
    return 1.0 / (1.0 + jnp.exp(-x))


def _silu(x):
    return x * _sigmoid(x)


def _modulate(x, g, shift, scale):
    ms = jnp.mean(x * x, axis=-1, keepdims=True)
    y = x * lax.rsqrt(ms + NORM_EPS)
    return (y * g) * (1.0 + scale) + shift


def _ada_kernel(c_ref, w_ref, b_ref, o_ref):
    a = _silu(c_ref[...]).astype(BF16)
    o_ref[0] = _dot(a, w_ref[0].astype(BF16)) + b_ref[0]


def _ada(c_all, ada_w, ada_b):
    depth, d, n6 = ada_w.shape
    bc = c_all.shape[0]
    tn = n6 // 4
    return pl.pallas_call(
        _ada_kernel,
        out_shape=jax.ShapeDtypeStruct((depth, bc, n6), F32),
        grid=(depth, n6 // tn),
        in_specs=[
            pl.BlockSpec((bc, d), lambda l, j: (0, 0)),
            pl.BlockSpec((1, d, tn), lambda l, j: (l, 0, j)),
            pl.BlockSpec((1, 1, tn), lambda l, j: (l, 0, j)),
        ],
        out_specs=pl.BlockSpec((1, bc, tn), lambda l, j: (l, 0, j)),
        compiler_params=_cparams("parallel", "parallel"),
        name="ada_mod",
    )(c_all, ada_w, ada_b.reshape(depth, 1, n6))


def _rope(x, cos, sin_signed):
    lane = lax.broadcasted_iota(jnp.int32, x.shape, 1)
    first = (lane % HEAD_DIM) < (HEAD_DIM // 2)
    swapped = jnp.where(first, pltpu.roll(x, LANES - HEAD_DIM // 2, 1), pltpu.roll(x, HEAD_DIM // 2, 1))
    return x * cos + swapped * sin_signed


def _pre0_project(x_ref, g_ref, sh_ref, sc_ref, w_ref, cos_ref, sin_ref, u_ref, q_ref, gate_ref):
    h = _modulate(x_ref[...], g_ref[...], sh_ref[0], sc_ref[0]).astype(BF16)
    cos = cos_ref[...]
    sin = sin_ref[...]

    def proj(c0, width):
        return _dot(h, w_ref[:, c0:c0 + width])

    u_ref[...] = proj(0, 512)
    for j in range(4):
        qj = _rope(proj(512 + LANES * j, LANES), cos, sin)
        q_ref[:, LANES * j:LANES * (j + 1)] = (qj * (HEAD_DIM ** -0.5)).astype(BF16)
    gate_ref[...] = _sigmoid(proj(1792, LANES))
    kc = _rope(proj(1024, LANES), cos, sin)
    vc = proj(1152, LANES)
    ks = _rope(proj(1280, LANES), cos, sin)
    vs = proj(1408, LANES)
    kw = _rope(proj(1536, LANES), cos, sin)
    vw = proj(1664, LANES)
    return kc, vc, ks, vs, kw, vw


def _pre0_rows_kernel(x_ref, g_ref, sh_ref, sc_ref, w_ref, cos_ref, sin_ref,
                      u_ref, q_ref, gate_ref, kv_ref, win_ref):
    kc, vc, ks, vs, kw, vw = _pre0_project(x_ref, g_ref, sh_ref, sc_ref, w_ref, cos_ref, sin_ref,
                                           u_ref, q_ref, gate_ref)
    for k, seg in enumerate((kc, vc, ks, vs)):
        kv_ref[k] = seg
    win_ref[0] = kw
    win_ref[1] = vw


def _store_transposed(ref, lead, x):
    xt = x.T
    for g in range(N_KV):
        ref[lead + (g,)] = xt[HEAD_DIM * g:HEAD_DIM * (g + 1), :].astype(ref.dtype)


def _pre0_seq_kernel(x_ref, g_ref, sh_ref, sc_ref, w_ref, cos_ref, sin_ref,
                     u_ref, q_ref, gate_ref, kvt_ref, wint_ref, kvbt_ref, cmp_ref):
    kc, vc, ks, vs, kw, vw = _pre0_project(x_ref, g_ref, sh_ref, sc_ref, w_ref, cos_ref, sin_ref,
                                           u_ref, q_ref, gate_ref)
    for k, seg in enumerate((kc, vc, ks, vs)):
        _store_transposed(kvt_ref, (k, 0), seg)
    _store_transposed(wint_ref, (0, 0), kw)
    _store_transposed(wint_ref, (1, 0), vw)
    for k, seg in enumerate((ks, vs, kw, vw)):
        _store_transposed(kvbt_ref, (k, 0), seg)
    cmp_ref[0] = kc
    cmp_ref[1] = vc


def _pre0(x2, g, shift, scale, w_pad, cos, sin, tm, rows_per_mod, pos_tiles, seq=None):
    n, d = x2.shape
    r = shift.shape[1]
    mod_spec = pl.BlockSpec((1, r, d), lambda i: (i // rows_per_mod, 0, 0))
    pos_spec = pl.BlockSpec((tm, LANES), lambda i: (i % pos_tiles, 0))
    row_spec = lambda w: pl.BlockSpec((tm, w), lambda i: (i, 0))
    out_shape = [jax.ShapeDtypeStruct((n, 512), F32), jax.ShapeDtypeStruct((n, 512), BF16),
                 jax.ShapeDtypeStruct((n, LANES), F32)]
    out_specs = [row_spec(512), row_spec(512), row_spec(LANES)]
    if seq is None:
        body = _pre0_rows_kernel
        out_shape += [jax.ShapeDtypeStruct((4, n, LANES), F32), jax.ShapeDtypeStruct((2, n, LANES), F32)]
        out_specs += [pl.BlockSpec((4, tm, LANES), lambda i: (0, i, 0)),
                      pl.BlockSpec((2, tm, LANES), lambda i: (0, i, 0))]
    else:
        body = _pre0_seq_kernel
        nb, tpb = n // seq, seq // tm
        t_spec = lambda k: pl.BlockSpec((k, 1, N_KV, HEAD_DIM, tm), lambda i: (0, i // tpb, 0, 0, i % tpb))
        out_shape += [jax.ShapeDtypeStruct((4, nb, N_KV, HEAD_DIM, seq), F32),
                      jax.ShapeDtypeStruct((2, nb, N_KV, HEAD_DIM, seq), F32),
                      jax.ShapeDtypeStruct((4, nb, N_KV, HEAD_DIM, seq), BF16),
                      jax.ShapeDtypeStruct((2, n, LANES), F32)]
        out_specs += [t_spec(4), t_spec(2), t_spec(4), pl.BlockSpec((2, tm, LANES), lambda i: (0, i, 0))]
    return pl.pallas_call(
        body,
        out_shape=tuple(out_shape),
        grid=(n // tm,),
        in_specs=[
            pl.BlockSpec((tm, d), lambda i: (i, 0)),
            pl.BlockSpec((1, d), lambda i: (0, 0)),
            mod_spec, mod_spec,
            pl.BlockSpec(w_pad.shape, lambda i: (0, 0)),
            pos_spec, pos_spec,
        ],
        out_specs=tuple(out_specs),
        compiler_params=_cparams("parallel"),
        name="pre0_in_proj",
    )(x2, g, shift, scale, w_pad, cos, sin)


def _rope_tables(pos):
    half = HEAD_DIM // 2
    inv = ROPE_THETA ** (-jnp.arange(half, dtype=F32) / half)
    ang = pos.astype(F32)[:, None] * inv[None, :]
    cos, sin = jnp.cos(ang), jnp.sin(ang)
    cos_t = jnp.tile(jnp.concatenate([cos, cos], axis=1), (1, LANES // HEAD_DIM))
    sin_t = jnp.tile(jnp.concatenate([-sin, sin], axis=1), (1, LANES // HEAD_DIM))
    return cos_t, sin_t


def _pool_body(ext_ref, t_rows, pos0, w_ref, scale_ref, out_ref):
    pos = pos0 + lax.broadcasted_iota(jnp.int32, (t_rows, 1), 0)
    for g, w in enumerate(POOL_WINDOWS):
        cols = slice(LANES * g, LANES * (g + 1))
        x = ext_ref[POOL_PAD:POOL_PAD + t_rows, cols]
        s = x
        for j in range(1, w):
            s = s + ext_ref[POOL_PAD - j:POOL_PAD - j + t_rows, cols]
        cnt = jnp.minimum(pos + 1, w).astype(F32)
        dlt = (s / cnt - x).astype(BF16)
        y = _dot(dlt, w_ref[g]) * scale_ref[:, cols]
        out_ref[:, cols] = y.astype(out_ref.dtype)


def _pool_prompt_kernel(x_ref, prev_ref, w_ref, scale_ref, out_ref, ext_ref, *, tiles_per_batch, tm):
    t_in_b = pl.program_id(0) % tiles_per_batch
    ext_ref[0:POOL_PAD, :] = jnp.where(t_in_b > 0, prev_ref[...], 0.0)
    ext_ref[POOL_PAD:, :] = x_ref[...]
    _pool_body(ext_ref, tm, t_in_b * tm, w_ref, scale_ref, out_ref)


def _pool_prompt(u, w_grp, scale, seq, tm):
    n, c = u.shape
    tpb = seq // tm
    hb = tm // POOL_PAD
    return pl.pallas_call(
        functools.partial(_pool_prompt_kernel, tiles_per_batch=tpb, tm=tm),
        out_shape=jax.ShapeDtypeStruct((n, c), BF16),
        grid=(n // tm,),
        in_specs=[
            pl.BlockSpec((tm, c), lambda i: (i, 0)),
            pl.BlockSpec((POOL_PAD, c), lambda i: (jnp.maximum(i * hb - 1, 0), 0)),
            pl.BlockSpec(w_grp.shape, lambda i: (0, 0, 0)),
            pl.BlockSpec((1, c), lambda i: (0, 0)),
        ],
        out_specs=pl.BlockSpec((tm, c), lambda i: (i, 0)),
        scratch_shapes=[pltpu.VMEM((POOL_PAD + tm, c), F32)],
        compiler_params=_cparams("parallel"),
        name="pool_prompt",
    )(u, u, w_grp, scale)


def _pool_sample_kernel(ext_ref, w_ref, scale_ref, out_ref, *, pos0, t_rows):
    _pool_body(ext_ref.at[0], t_rows, pos0, w_ref, scale_ref, out_ref.at[0])


def _pool_sample(ext, w_grp, scale, pos0):
    b, rows, c = ext.shape
    t_rows = rows - POOL_PAD
    return pl.pallas_call(
        functools.partial(_pool_sample_kernel, pos0=pos0, t_rows=t_rows),
        out_shape=jax.ShapeDtypeStruct((b, t_rows, c), BF16),
        grid=(b,),
        in_specs=[
            pl.BlockSpec((1, rows, c), lambda i: (i, 0, 0)),
            pl.BlockSpec(w_grp.shape, lambda i: (0, 0, 0)),
            pl.BlockSpec((1, c), lambda i: (0, 0)),
        ],
        out_specs=pl.BlockSpec((1, t_rows, c), lambda i: (i, 0, 0)),
        compiler_params=_cparams("parallel"),
        name="pool_sample",
    )(ext, w_grp, scale)


def _chunk_rows(ref2d, rows):
    n = rows // CMP_STRIDE
    return jnp.concatenate([ref2d[pl.ds(r, n, stride=CMP_STRIDE), :] for r in range(CMP_STRIDE)], axis=1)


def _cmp_partial(a, pos_ref, w_ref, p_ref):
    hid2 = w_ref.shape[2] // 2
    p_ref[0, 0, :, 0:hid2] = _dot((a + pos_ref[0, 0:1, :]).astype(BF16), w_ref[0, :, 0:hid2])
    p_ref[0, 0, :, hid2:] = _dot((a + pos_ref[0, 1:2, :]).astype(BF16), w_ref[0, :, hid2:])


def _cmp_rows_kernel(x_ref, pos_ref, w_ref, p_ref, *, rows):
    _cmp_partial(_chunk_rows(x_ref.at[0, 0], rows), pos_ref, w_ref, p_ref)


def _cmp_rows(x4, pos_ab, w1_ab, nbatch, rows):
    nch = rows // CMP_STRIDE
    return pl.pallas_call(
        functools.partial(_cmp_rows_kernel, rows=rows),
        out_shape=jax.ShapeDtypeStruct((2, nbatch, nch, w1_ab.shape[2]), F32),
        grid=(2, nbatch),
        in_specs=[
            pl.BlockSpec((1, 1, rows, LANES), lambda s, b: (s, b, 0, 0)),
            pl.BlockSpec((1, 2, pos_ab.shape[2]), lambda s, b: (s, 0, 0)),
            pl.BlockSpec((1,) + w1_ab.shape[1:], lambda s, b: (s, 0, 0)),
        ],
        out_specs=pl.BlockSpec((1, 1, nch, w1_ab.shape[2]), lambda s, b: (s, b, 0, 0)),
        compiler_params=_cparams("parallel", "parallel"),
        name="cmp_rows",
    )(x4, pos_ab, w1_ab)


PAGES_PER_STEP = 64


def _cmp_pages_kernel(pt_ref, *refs, pps):
    del pt_ref
    page_refs = refs[:pps]
    pos_ref, w_ref, p_ref, rows_ref = refs[pps:]
    for k, r in enumerate(page_refs):
        page_t = jnp.concatenate([r[0, 0, g] for g in range(N_KV)], axis=0)
        rows_ref[k * PAGE_SIZE:(k + 1) * PAGE_SIZE, :] = page_t.T
    _cmp_partial(_chunk_rows(rows_ref, pps * PAGE_SIZE), pos_ref, w_ref, p_ref)


def _cmp_pages(cache_t, page_table_flat, n_pages, pos_ab, w1_ab, nbatch):
    pps = min(PAGES_PER_STEP, n_pages)
    assert n_pages % pps == 0
    steps = n_pages // pps
    nch = pps * PAGE_SIZE // CMP_STRIDE

    def page_spec(k):
        return pl.BlockSpec(
            (1, 1, N_KV, HEAD_DIM, PAGE_SIZE),
            lambda s, b, j, pt: (s, pt[b * n_pages + j * pps + k], 0, 0, 0))

    grid_spec = pltpu.PrefetchScalarGridSpec(
        num_scalar_prefetch=1,
        grid=(2, nbatch, steps),
        in_specs=[page_spec(k) for k in range(pps)] + [
            pl.BlockSpec((1, 2, pos_ab.shape[2]), lambda s, b, j, pt: (s, 0, 0)),
            pl.BlockSpec((1,) + w1_ab.shape[1:], lambda s, b, j, pt: (s, 0, 0)),
        ],
        out_specs=pl.BlockSpec((1, 1, nch, w1_ab.shape[2]), lambda s, b, j, pt: (s, b, j, 0)),
        scratch_shapes=[pltpu.VMEM((pps * PAGE_SIZE, LANES), F32)],
    )
    return pl.pallas_call(
        functools.partial(_cmp_pages_kernel, pps=pps),
        out_shape=jax.ShapeDtypeStruct((2, nbatch, steps * nch, w1_ab.shape[2]), F32),
        grid_spec=grid_spec,
        compiler_params=_cparams("parallel", "parallel", "parallel"),
        name="cmp_pages",
    )(page_table_flat, *([cache_t] * pps), pos_ab, w1_ab)


def _gelu_tanh(x):
    return x * (0.5 * (1.0 + jnp.tanh(np.sqrt(2.0 / np.pi).astype(np.float32) * (x + 0.044715 * (x * x * x)))))


def _cmp_combine_kernel(p_ref, w2_ref, w2t_ref, o_ref, ot_ref, pb_ref, *, n_p):
    hid2 = p_ref.shape[3] // 2
    pb_ref[0:n_p, :] = p_ref[0, 0, :, hid2:]
    pb_ref[n_p:n_p + 8, :] = jnp.zeros((8, hid2), F32)
    hsum = p_ref[0, 0, :, 0:hid2] + pb_ref[1:n_p + 1, :]
    act = _gelu_tanh(hsum).astype(BF16)
    o_ref[0, 0] = _dot(act, w2_ref[0]).astype(o_ref.dtype)
    ot_ref[0, 0] = _dot_nt(w2t_ref[0], act).astype(ot_ref.dtype)


def _cmp_combine(p, w2_bd):
    _, nbatch, n_p, width = p.shape
    w2t_bd = w2_bd.transpose(0, 2, 1)
    return pl.pallas_call(
        functools.partial(_cmp_combine_kernel, n_p=n_p),
        out_shape=(jax.ShapeDtypeStruct((2, nbatch, n_p, LANES), BF16),
                   jax.ShapeDtypeStruct((2, nbatch, LANES, n_p), BF16)),
        grid=(2, nbatch),
        in_specs=[
            pl.BlockSpec((1, 1, n_p, width), lambda s, b: (s, b, 0, 0)),
            pl.BlockSpec((1,) + w2_bd.shape[1:], lambda s, b: (s, 0, 0)),
            pl.BlockSpec((1,) + w2t_bd.shape[1:], lambda s, b: (s, 0, 0)),
        ],
        out_specs=(pl.BlockSpec((1, 1, n_p, LANES), lambda s, b: (s, b, 0, 0)),
                   pl.BlockSpec((1, 1, LANES, n_p), lambda s, b: (s, b, 0, 0))),
        scratch_shapes=[pltpu.VMEM((n_p + 8, width // 2), F32)],
        compiler_params=_cparams("parallel", "parallel"),
        name="cmp_combine",
    )(p, w2_bd, w2t_bd)


def _cmp_weights(cmp_pos, cmp_w1, cmp_w2):
    hid = cmp_w1.shape[2]
    half = CMP_STRIDE * HEAD_DIM
    eye = jnp.eye(N_KV, dtype=F32)
    pos_ab = jnp.tile(cmp_pos.reshape(2, 2, CMP_STRIDE, 1, HEAD_DIM), (1, 1, 1, N_KV, 1))
    pos_ab = pos_ab.reshape(2, 2, CMP_STRIDE * N_KV * HEAD_DIM)
    w1 = cmp_w1.reshape(2, 2, CMP_STRIDE, HEAD_DIM, hid)
    w1_bd = jnp.einsum("shrdj,gk->shrgdkj", w1, eye)
    w1_bd = w1_bd.reshape(2, 2, CMP_STRIDE * N_KV * HEAD_DIM, N_KV * hid)
    w1_ab = jnp.concatenate([w1_bd[:, 0], w1_bd[:, 1]], axis=2).astype(BF16)
    w2_bd = jnp.einsum("sjd,gk->sgjkd", cmp_w2, eye).reshape(2, N_KV * hid, N_KV * HEAD_DIM).astype(BF16)
    del half
    return pos_ab, w1_ab, w2_bd


M_COL = HEAD_DIM + 1


def _online_update(s, v_aug_t, state):
    m_new = jnp.max(s, axis=-1, keepdims=True)
    if state is not None:
        m_old = state[:, M_COL:M_COL + 1]
        m_new = jnp.maximum(m_old, m_new)
    acc = _dot_nt(jnp.exp(s - m_new).astype(BF16), v_aug_t)
    if state is not None:
        acc = jnp.exp(m_old - m_new) * state + acc
    lane = lax.broadcasted_iota(jnp.int32, acc.shape, 1)
    return jnp.where(lane == M_COL, m_new, acc)


def _attn_prompt_kernel(q_ref, cmpt_ref, kst_ref, vst_ref, kwt_ref, vwt_ref, gate_ref, mt_ref, e_ref,
                        o_ref, kaug_s, vaug_s, vwaug_s, *, tq, nblk):
    i = pl.program_id(1)
    s0 = i * tq
    seq = kst_ref.shape[4]
    ncmp = cmpt_ref.shape[3]

    @pl.when(i == 0)
    def _():
        ones_row = (lax.broadcasted_iota(jnp.int32, (HEAD_DIM, seq), 0) == 0).astype(BF16)
        for g in range(N_KV):
            kaug_s[g, 0:LANES, :] = e_ref[...]
            kaug_s[g, LANES:LANES + HEAD_DIM, :] = kst_ref[0, 0, g]
            vaug_s[g, 0:HEAD_DIM, :] = vst_ref[0, 0, g]
            vaug_s[g, HEAD_DIM:, :] = ones_row
            vwaug_s[g, 0:HEAD_DIM, :] = vwt_ref[0, 0, g]
            vwaug_s[g, HEAD_DIM:, :] = ones_row

    row = lax.broadcasted_iota(jnp.int32, (tq, 1), 0)
    col = lax.broadcasted_iota(jnp.int32, (1, tq), 1)
    qpos = s0 + row
    gates = gate_ref[...]
    c_end = lax.broadcasted_iota(jnp.int32, (1, ncmp), 1) * CMP_STRIDE + (CMP_BLOCK - 1)
    mask_c = c_end <= qpos
    qp_l = s0 + col
    cur = qp_l // SEL_BLOCK
    n_rb = -(-nblk // 8)
    qs = [q_ref[:, HEAD_DIM * hh:HEAD_DIM * (hh + 1)] for hh in range(N_HEADS)]

    o_cmp, q_aug = [], []
    for g in range(N_KV):
        kct_g = cmpt_ref[0, 0, HEAD_DIM * g:HEAD_DIM * (g + 1), :]
        vct_g = cmpt_ref[1, 0, HEAD_DIM * g:HEAD_DIM * (g + 1), :]
        p_sum = jnp.zeros((tq, ncmp), F32)
        for qh in qs[GROUP_R * g:GROUP_R * (g + 1)]:
            s = jnp.where(mask_c, _dot(qh, kct_g), -jnp.inf)
            m = jnp.max(s, axis=-1, keepdims=True)
            m = jnp.where(m == -jnp.inf, 0.0, m)
            e = jnp.exp(s - m)
            d = jnp.sum(e, axis=-1, keepdims=True)
            p = e / jnp.where(d > 0, d, 1.0)
            p_sum = p_sum + p
            o_cmp.append(_dot_nt(p.astype(BF16), vct_g))

        p_hi = p_sum.astype(BF16)
        p_lo = (p_sum - p_hi.astype(F32)).astype(BF16)
        imp = _dot_nt(mt_ref[...], p_hi) + _dot_nt(mt_ref[...], p_lo)
        score, valid = [], []
        for r in range(n_rb):
            jb = 8 * r + lax.broadcasted_iota(jnp.int32, (8, 1), 0)
            ok = (jb * SEL_BLOCK <= qp_l) & (jb < nblk)
            forced = (jb == 0) | (jb == cur) | (jb == cur - 1)
            valid.append(ok)
            score.append(jnp.where(ok, jnp.where(forced, jnp.inf, imp[8 * r:8 * r + 8, :]), -jnp.inf))
        cnt = [jnp.zeros((8, tq), jnp.int32) for _ in range(n_rb)]
        for j in range(nblk):
            rj = jnp.broadcast_to(score[j // 8][j % 8:j % 8 + 1, :], (8, tq))
            for r in range(n_rb):
                if 8 * r > j:
                    beats = rj >= score[r]
                elif 8 * r + 7 < j:
                    beats = rj > score[r]
                else:
                    later = 8 * r + lax.broadcasted_iota(jnp.int32, (8, 1), 0) > j
                    beats = (rj > score[r]) | ((rj == score[r]) & later)
                cnt[r] = cnt[r] + jnp.where(beats, 1, 0)
        selneg_t = [jnp.where((cnt[r] < N_SEL) & valid[r], 0.0, NEG) for r in range(n_rb)]
        selneg_t.append(jnp.zeros((LANES - 8 * n_rb, tq), F32))
        selneg = jnp.concatenate(selneg_t, axis=0).T.astype(BF16)
        q_aug += [jnp.concatenate([selneg, qh], axis=1) for qh in qs[GROUP_R * g:GROUP_R * (g + 1)]]

    ck = 2 * tq

    def sel_chunk(c0, state, keep):
        out = []
        for g in range(N_KV):
            k_aug = kaug_s[g, :, pl.ds(c0, ck)]
            v_aug = vaug_s[g, :, pl.ds(c0, ck)]
            for hh in range(GROUP_R * g, GROUP_R * (g + 1)):
                s = _dot(q_aug[hh], k_aug)
                if keep is not None:
                    s = jnp.where(keep, s, NEG)
                out.append(_online_update(s, v_aug, None if state is None else state[hh]))
        return tuple(out)

    n_full = i // 2
    last0 = pl.multiple_of(jnp.maximum(i - 1, 0) * tq, tq)
    kpos = last0 + lax.broadcasted_iota(jnp.int32, (1, ck), 1)
    state = sel_chunk(last0, None, (kpos <= qpos) & (kpos >= n_full * ck))
    state = lax.fori_loop(0, n_full, lambda c, st: sel_chunk(pl.multiple_of(c * ck, ck), st, None), state)
    o_sel = [st[:, 0:HEAD_DIM] / st[:, HEAD_DIM:HEAD_DIM + 1] for st in state]

    keep_w = (kpos <= qpos) & (kpos > qpos - WINDOW)
    wstate = []
    for g in range(N_KV):
        k_t = kwt_ref[0, 0, g, :, pl.ds(last0, ck)]
        v_aug = vwaug_s[g, :, pl.ds(last0, ck)]
        for hh in range(GROUP_R * g, GROUP_R * (g + 1)):
            wstate.append(_online_update(jnp.where(keep_w, _dot(qs[hh], k_t), NEG), v_aug, None))
    for hh in range(N_HEADS):
        o_win = wstate[hh][:, 0:HEAD_DIM] / wstate[hh][:, HEAD_DIM:HEAD_DIM + 1]
        o = (o_cmp[hh] * gates[:, 3 * hh:3 * hh + 1] + o_sel[hh] * gates[:, 3 * hh + 1:3 * hh + 2]
             + o_win * gates[:, 3 * hh + 2:3 * hh + 3])
        o_ref[:, HEAD_DIM * hh:HEAD_DIM * (hh + 1)] = o.astype(o_ref.dtype)


def _attn_prompt(q, cmp_t, kvbt, gates, mt, e_mat, nbatch, seq, tq):
    assert tq >= WINDOW and seq >= 2 * tq
    n = q.shape[0]
    nq = seq // tq
    ncmp = cmp_t.shape[3]
    kv_spec = lambda slot: pl.BlockSpec((1, 1, N_KV, HEAD_DIM, seq), lambda b, i: (slot, b, 0, 0, 0))
    return pl.pallas_call(
        functools.partial(_attn_prompt_kernel, tq=tq, nblk=seq // SEL_BLOCK),
        out_shape=jax.ShapeDtypeStruct((n, N_HEADS * HEAD_DIM), BF16),
        grid=(nbatch, nq),
        in_specs=[
            pl.BlockSpec((tq, N_HEADS * HEAD_DIM), lambda b, i: (b * nq + i, 0)),
            pl.BlockSpec((2, 1, LANES, ncmp), lambda b, i: (0, b, 0, 0)),
            kv_spec(0), kv_spec(1), kv_spec(2), kv_spec(3),
            pl.BlockSpec((tq, LANES), lambda b, i: (b * nq + i, 0)),
            pl.BlockSpec(mt.shape, lambda b, i: (0, 0)),
            pl.BlockSpec(e_mat.shape, lambda b, i: (0, 0)),
        ],
        out_specs=pl.BlockSpec((tq, N_HEADS * HEAD_DIM), lambda b, i: (b * nq + i, 0)),
        scratch_shapes=[
            pltpu.VMEM((N_KV, LANES + HEAD_DIM, seq), BF16),
            pltpu.VMEM((N_KV, LANES, seq), BF16),
            pltpu.VMEM((N_KV, LANES, seq), BF16),
        ],
        compiler_params=_cparams("arbitrary", "arbitrary"),
        name="attn_prompt",
    )(q, cmp_t, kvbt, kvbt, kvbt, kvbt, gates, mt, e_mat)


def _imp_matrix(ncmp, nblk_pad):
    per = SEL_BLOCK // CMP_STRIDE
    pad = CMP_BLOCK // CMP_STRIDE - 1
    n = np.arange(ncmp)[:, None]
    j = np.arange(nblk_pad)[None, :]
    return ((n >= per * j - pad) & (n <= per * j + per - 1)).astype(np.float32)


def _attn_sample_cmp_kernel(q_ref, kc_ref, vc_ref, m_ref, oc_ref, idx_ref, *, n_valid_cmp, qpos0, nblk):
    ncmp = kc_ref.shape[2]
    rows = q_ref.shape[2]
    t_row = lax.broadcasted_iota(jnp.int32, (rows, 1), 0) % 8
    c_idx = lax.broadcasted_iota(jnp.int32, (1, ncmp), 1)
    mask_c = (c_idx * CMP_STRIDE + (CMP_BLOCK - 1) <= qpos0 + t_row) & (c_idx < n_valid_cmp)
    p_tok = []
    for g in range(N_KV):
        gs = slice(HEAD_DIM * g, HEAD_DIM * (g + 1))
        s = jnp.where(mask_c, _dot_nt(q_ref[0, g], kc_ref[0, 0, :, gs]), -jnp.inf)
        m = jnp.max(s, axis=-1, keepdims=True)
        m = jnp.where(m == -jnp.inf, 0.0, m)
        e = jnp.exp(s - m)
        d = jnp.sum(e, axis=-1, keepdims=True)
        p = e / jnp.where(d > 0, d, 1.0)
        oc_ref[0, g] = _dot(p.astype(BF16), vc_ref[0, 0, :, gs])
        p_tok.append(jnp.sum(p.reshape(GROUP_R, 8, ncmp), axis=0))
    p_all = jnp.concatenate(p_tok, axis=0)
    p_hi = p_all.astype(BF16)
    p_lo = (p_all - p_hi.astype(F32)).astype(BF16)
    imp = _dot(p_hi, m_ref[...]) + _dot(p_lo, m_ref[...])
    nb_pad = imp.shape[1]
    jb = lax.broadcasted_iota(jnp.int32, (1, nb_pad), 1)
    qpos = qpos0 + lax.broadcasted_iota(jnp.int32, (2 * 8, 1), 0) % 8
    cur = qpos // SEL_BLOCK
    valid = (jb * SEL_BLOCK <= qpos) & (jb < nblk)
    forced = (jb == 0) | (jb == cur) | (jb == cur - 1)
    score = jnp.where(valid, jnp.where(forced, jnp.inf, imp), -jnp.inf)
    avail = jb < nblk
    lane = lax.broadcasted_iota(jnp.int32, (2 * 8, LANES), 1)
    picked = jnp.zeros((2 * 8, LANES), jnp.int32)
    for k in range(N_SEL):
        best = jnp.max(jnp.where(avail, score, -jnp.inf), axis=-1, keepdims=True)
        cand = avail & (score == best)
        idx = jnp.min(jnp.where(cand, jb, nb_pad), axis=-1, keepdims=True)
        picked = jnp.where(lane == k, idx, picked)
        avail = avail & (jb != idx)
    idx_ref[0] = picked


def _attn_sample_cmp(qg, cmp_kv, m_mat, n_valid_cmp, qpos0, nblk):
    nbatch = qg.shape[0]
    ncmp = cmp_kv.shape[2]
    rows = qg.shape[2]
    cmp_spec = lambda slot: pl.BlockSpec((1, 1, ncmp, LANES), lambda b: (slot, b, 0, 0))
    return pl.pallas_call(
        functools.partial(_attn_sample_cmp_kernel, n_valid_cmp=n_valid_cmp, qpos0=qpos0, nblk=nblk),
        out_shape=(
            jax.ShapeDtypeStruct((nbatch, N_KV, rows, HEAD_DIM), F32),
            jax.ShapeDtypeStruct((nbatch, 2 * 8, LANES), jnp.int32),
        ),
        grid=(nbatch,),
        in_specs=[
            pl.BlockSpec((1, N_KV, rows, HEAD_DIM), lambda b: (b, 0, 0, 0)),
            cmp_spec(0), cmp_spec(1),
            pl.BlockSpec(m_mat.shape, lambda b: (0, 0)),
        ],
        out_specs=(
            pl.BlockSpec((1, N_KV, rows, HEAD_DIM), lambda b: (b, 0, 0, 0)),
            pl.BlockSpec((1, 2 * 8, LANES), lambda b: (b, 0, 0)),
        ),
        compiler_params=_cparams("parallel"),
        name="attn_sample_cmp",
    )(qg, cmp_kv, cmp_kv, m_mat)


def _masked_softmax(s, mask):
    s = jnp.where(mask, s, -jnp.inf)
    m = jnp.max(s, axis=-1, keepdims=True)
    m = jnp.where(m == -jnp.inf, 0.0, m)
    e = jnp.exp(s - m)
    d = jnp.sum(e, axis=-1, keepdims=True)
    return e / jnp.where(d > 0, d, 1.0)


def _attn_sample_sel_kernel(idx_ref, pt_ref, *refs, dec_seq, qpos0, n_cache_blk, wb):
    del pt_ref
    nsel = N_KV * N_SEL
    per_page = PAGE_SIZE // SEL_BLOCK
    k_refs = refs[:nsel]
    v_refs = refs[nsel:2 * nsel]
    q_ref, tail_ref, kw_ref, vw_ref, wnew_ref, oc_ref, gate_ref, o_ref, osel_ref = refs[2 * nsel:]
    b = pl.program_id(0)
    t = pl.program_id(1)
    rows = q_ref.shape[2]
    t_row = lax.broadcasted_iota(jnp.int32, (rows, 1), 0) % 8
    qpos = qpos0 + t_row
    colk = lax.broadcasted_iota(jnp.int32, (1, N_SEL * PAGE_SIZE), 1)
    slot_of_col = colk // PAGE_SIZE
    row_in_page = colk % PAGE_SIZE

    @pl.when(t == 0)
    def _():
        osel_ref[...] = jnp.zeros(osel_ref.shape, F32)

    for g in range(N_KV):
        blk_of_col = jnp.zeros_like(colk)
        k_pages, v_pages = [], []
        for k in range(N_SEL):
            blk = idx_ref[((b * dec_seq + t) * N_KV + g) * N_SEL + k]
            is_new = blk >= n_cache_blk
            k_pages.append(jnp.where(is_new, tail_ref[0, 0, g], k_refs[g * N_SEL + k][0, 0, 0]))
            v_pages.append(jnp.where(is_new, tail_ref[1, 0, g], v_refs[g * N_SEL + k][0, 0, 0]))
            blk_of_col = blk_of_col + jnp.where(slot_of_col == k, blk, 0)
        in_block = row_in_page // SEL_BLOCK == blk_of_col % per_page
        kpos = blk_of_col * SEL_BLOCK + row_in_page % SEL_BLOCK
        k_t = jnp.concatenate(k_pages, axis=1).astype(BF16)
        v_t = jnp.concatenate(v_pages, axis=1).astype(BF16)
        p = _masked_softmax(_dot(q_ref[0, g], k_t), in_block & (kpos <= qpos))
        o = _dot_nt(p.astype(BF16), v_t)
        osel_ref[g] = osel_ref[g] + jnp.where(t_row == t, o, 0.0)

    @pl.when(t == dec_seq - 1)
    def _():
        colw = lax.broadcasted_iota(jnp.int32, (1, wb + LANES), 1)
        kwpos = jnp.where(colw < wb, qpos0 - wb + colw, qpos0 + colw - wb)
        dpos = qpos - kwpos
        mask_w = (dpos >= 0) & (dpos < WINDOW) & (kwpos >= 0) & (colw < wb + dec_seq)
        for g in range(N_KV):
            k_t = jnp.concatenate([kw_ref[0, 0, g], wnew_ref[0, 0, g]], axis=1).astype(BF16)
            v_t = jnp.concatenate([vw_ref[0, 0, g], wnew_ref[1, 0, g]], axis=1).astype(BF16)
            p = _masked_softmax(_dot(q_ref[0, g], k_t), mask_w)
            o_win = _dot_nt(p.astype(BF16), v_t)
            gt = gate_ref[0, g]
            o_ref[0, g] = oc_ref[0, g] * gt[:, 0:1] + osel_ref[g] * gt[:, 1:2] + o_win * gt[:, 2:3]


def _attn_sample_sel(idx_flat, pt_flat, cache_t, qg, tail_t, win_t, wnew_t, o_cmp, gates_g, dec_seq, n_pages,
                     qpos0):
    nbatch = qg.shape[0]
    rows = qg.shape[2]
    wb = win_t.shape[4]
    per_page = PAGE_SIZE // SEL_BLOCK
    n_cache_blk = n_pages * per_page

    def page_spec(slot, g, k):
        def imap(b, t, idx, pt):
            blk = jnp.minimum(idx[((b * dec_seq + t) * N_KV + g) * N_SEL + k], n_cache_blk - 1)
            return (slot, pt[b * n_pages + blk // per_page], g, 0, 0)
        return pl.BlockSpec((1, 1, 1, HEAD_DIM, PAGE_SIZE), imap)

    k_specs = [page_spec(2, g, k) for g in range(N_KV) for k in range(N_SEL)]
    v_specs = [page_spec(3, g, k) for g in range(N_KV) for k in range(N_SEL)]
    per_b = lambda shape: pl.BlockSpec((1,) + shape, lambda b, t, idx, pt: (b,) + (0,) * len(shape))
    new_spec = pl.BlockSpec((2, 1, N_KV, HEAD_DIM, LANES), lambda b, t, idx, pt: (0, b, 0, 0, 0))
    win_spec = lambda s: pl.BlockSpec((1, 1, N_KV, HEAD_DIM, wb), lambda b, t, idx, pt: (s, b, 0, 0, 0))
    grid_spec = pltpu.PrefetchScalarGridSpec(
        num_scalar_prefetch=2,
        grid=(nbatch, dec_seq),
        in_specs=k_specs + v_specs + [
            per_b((N_KV, rows, HEAD_DIM)),
            new_spec, win_spec(0), win_spec(1), new_spec,
            per_b((N_KV, rows, HEAD_DIM)),
            per_b((N_KV, rows, LANES)),
        ],
        out_specs=per_b((N_KV, rows, HEAD_DIM)),
        scratch_shapes=[pltpu.VMEM((N_KV, rows, HEAD_DIM), F32)],
    )
    n_page_specs = 2 * N_KV * N_SEL
    return pl.pallas_call(
        functools.partial(_attn_sample_sel_kernel, dec_seq=dec_seq, qpos0=qpos0, n_cache_blk=n_cache_blk, wb=wb),
        out_shape=jax.ShapeDtypeStruct((nbatch, N_KV, rows, HEAD_DIM), F32),
        grid_spec=grid_spec,
        compiler_params=_cparams("parallel", "arbitrary"),
        name="attn_sample_sel",
    )(idx_flat, pt_flat, *([cache_t] * n_page_specs), qg, tail_t, win_t, win_t, wnew_t, o_cmp, gates_g)


FFN_CHUNK = 256


def _post_kernel(*refs, n_a, final):
    y_ref = refs[0]
    a_refs = refs[1:1 + 2 * n_a]
    gmix_ref, g2_ref, sh_ref, sc_ref, gffn_ref, w1_ref, w2_ref = refs[1 + 2 * n_a:8 + 2 * n_a]
    rest = refs[8 + 2 * n_a:]
    out_ref = rest[-1]
    mix = _dot(a_refs[0][...], a_refs[1][...])
    for k in range(1, n_a):
        mix = mix + _dot(a_refs[2 * k][...], a_refs[2 * k + 1][...])
    y1 = y_ref[...] + gmix_ref[0] * mix
    h = _modulate(y1, g2_ref[...], sh_ref[0], sc_ref[0]).astype(BF16)
    d_ff = w2_ref.shape[0]
    acc = jnp.zeros(y1.shape, F32)
    for c in range(d_ff // FFN_CHUNK):
        c0 = c * FFN_CHUNK
        gate = _dot(h, w1_ref[:, c0:c0 + FFN_CHUNK])
        up = _dot(h, w1_ref[:, d_ff + c0:d_ff + c0 + FFN_CHUNK])
        acc = acc + _dot((_silu(gate) * up).astype(BF16), w2_ref[c0:c0 + FFN_CHUNK, :])
    y2 = y1 + gffn_ref[0] * acc
    if final:
        fg_ref = rest[0]
        ms = jnp.mean(y2 * y2, axis=-1, keepdims=True)
        y2 = (y2 * lax.rsqrt(ms + NORM_EPS)) * fg_ref[...]
    out_ref[...] = y2


def _post(y, a_list, wo_list, gmix, g2, shift, scale, gffn, w1, w2, final_g, tm, rows_per_mod):
    n, d = y.shape
    r = shift.shape[1]
    mod_spec = pl.BlockSpec((1, r, d), lambda i: (i // rows_per_mod, 0, 0))
    const = lambda arr: pl.BlockSpec(arr.shape, lambda i: (0,) * arr.ndim, pipeline_mode=pl.Buffered(1))
    in_specs = [pl.BlockSpec((tm, d), lambda i: (i, 0))]
    args = [y]
    for a, wo in zip(a_list, wo_list):
        in_specs += [pl.BlockSpec((tm, a.shape[1]), lambda i: (i, 0)), const(wo)]
        args += [a, wo]
    in_specs += [mod_spec, pl.BlockSpec((1, d), lambda i: (0, 0)), mod_spec, mod_spec, mod_spec, const(w1), const(w2)]
    args += [gmix, g2, shift, scale, gffn, w1, w2]
    if final_g is not None:
        in_specs.append(pl.BlockSpec((1, d), lambda i: (0, 0)))
        args.append(final_g)
    return pl.pallas_call(
        functools.partial(_post_kernel, n_a=len(a_list), final=final_g is not None),
        out_shape=jax.ShapeDtypeStruct((n, d), F32),
        grid=(n // tm,),
        in_specs=in_specs,
        out_specs=pl.BlockSpec((tm, d), lambda i: (i, 0)),
        compiler_params=_cparams("parallel"),
        name="post_proj_ffn",
    )(*args)


def _pre1_kernel(x_ref, g_ref, sh_ref, sc_ref, w_ref, u_ref):
    h = _modulate(x_ref[...], g_ref[...], sh_ref[0], sc_ref[0]).astype(BF16)
    d = u_ref.shape[1]
    a = _dot(h, w_ref[:, 0:d])
    b = _dot(h, w_ref[:, d:2 * d])
    u_ref[...] = a * _sigmoid(b)


def _pre1(x2, g, shift, scale, pw1, tm, rows_per_mod):
    n, d = x2.shape
    r = shift.shape[1]
    mod_spec = pl.BlockSpec((1, r, d), lambda i: (i // rows_per_mod, 0, 0))
    return pl.pallas_call(
        _pre1_kernel,
        out_shape=jax.ShapeDtypeStruct((n, d), F32),
        grid=(n // tm,),
        in_specs=[
            pl.BlockSpec((tm, d), lambda i: (i, 0)),
            pl.BlockSpec((1, d), lambda i: (0, 0)),
            mod_spec, mod_spec,
            pl.BlockSpec(pw1.shape, lambda i: (0, 0)),
        ],
        out_specs=pl.BlockSpec((tm, d), lambda i: (i, 0)),
        compiler_params=_cparams("parallel"),
        name="pre1_pw_glu",
    )(x2, g, shift, scale, pw1)


def _conv_body(ext_ref, t_rows, dw_ref, dwb_ref, lng_ref, lnb_ref, out_ref):
    off = CONV_PAD - CONV_HIST
    acc = None
    for b in range(8):
        span = t_rows + (8 if b else 0)
        part = None
        for k in range(CONV_WIDTH):
            if (off + k) % 8 != b:
                continue
            base = off + k - b
            term = ext_ref[base:base + span, :] * dw_ref[k:k + 1, :]
            part = term if part is None else part + term
        part = part[b:b + t_rows, :]
        acc = part if acc is None else acc + part
    y = acc + dwb_ref[...]
    mu = jnp.mean(y, axis=-1, keepdims=True)
    var = jnp.mean(jnp.square(y - mu), axis=-1, keepdims=True)
    z = (y - mu) * lax.rsqrt(var + NORM_EPS) * lng_ref[...] + lnb_ref[...]
    out_ref[...] = _silu(z).astype(out_ref.dtype)


def _conv_prompt_kernel(x_ref, prev_ref, dw_ref, dwb_ref, lng_ref, lnb_ref, out_ref, ext_ref, *,
                        tiles_per_batch, tm):
    t_in_b = pl.program_id(0) % tiles_per_batch
    ext_ref[0:CONV_PAD, :] = jnp.where(t_in_b > 0, prev_ref[...], 0.0)
    ext_ref[CONV_PAD:, :] = x_ref[...]
    _conv_body(ext_ref, tm, dw_ref, dwb_ref, lng_ref, lnb_ref, out_ref)


def _conv_prompt(u, dw, dwb, lng, lnb, seq, tm):
    n, d = u.shape
    tpb = seq // tm
    hb = tm // CONV_PAD
    vec = pl.BlockSpec((1, d), lambda i: (0, 0))
    return pl.pallas_call(
        functools.partial(_conv_prompt_kernel, tiles_per_batch=tpb, tm=tm),
        out_shape=jax.ShapeDtypeStruct((n, d), BF16),
        grid=(n // tm,),
        in_specs=[
            pl.BlockSpec((tm, d), lambda i: (i, 0)),
            pl.BlockSpec((CONV_PAD, d), lambda i: (jnp.maximum(i * hb - 1, 0), 0)),
            pl.BlockSpec(dw.shape, lambda i: (0, 0)),
            vec, vec, vec,
        ],
        out_specs=pl.BlockSpec((tm, d), lambda i: (i, 0)),
        scratch_shapes=[pltpu.VMEM((CONV_PAD + tm, d), F32)],
        compiler_params=_cparams("parallel"),
        name="conv_prompt",
    )(u, u, dw, dwb, lng, lnb)


def _conv_sample_kernel(ext_ref, dw_ref, dwb_ref, lng_ref, lnb_ref, out_ref, *, t_rows):
    _conv_body(ext_ref.at[0], t_rows, dw_ref, dwb_ref, lng_ref, lnb_ref, out_ref.at[0])


def _conv_sample(ext, dw, dwb, lng, lnb):
    b, rows, d = ext.shape
    t_rows = rows - CONV_PAD
    vec = pl.BlockSpec((1, d), lambda i: (0, 0))
    return pl.pallas_call(
        functools.partial(_conv_sample_kernel, t_rows=t_rows),
        out_shape=jax.ShapeDtypeStruct((b, t_rows, d), BF16),
        grid=(b,),
        in_specs=[
            pl.BlockSpec((1, rows, d), lambda i: (i, 0, 0)),
            pl.BlockSpec(dw.shape, lambda i: (0, 0)),
            vec, vec, vec,
        ],
        out_specs=pl.BlockSpec((1, t_rows, d), lambda i: (i, 0, 0)),
        compiler_params=_cparams("parallel"),
        name="conv_sample",
    )(ext, dw, dwb, lng, lnb)


ROW_TILE = 512
ATTN_TILE = 512


def _group_rows(x, nbatch, dec_seq):
    w = x.shape[1] // N_HEADS
    x = x.reshape(nbatch, dec_seq, N_KV, GROUP_R, w).transpose(0, 2, 3, 1, 4)
    x = jnp.pad(x, ((0, 0), (0, 0), (0, 0), (0, 8 - dec_seq), (0, 0)))
    return x.reshape(nbatch, N_KV, GROUP_R * 8, w)


def _ungroup_rows(x, nbatch, dec_seq):
    w = x.shape[3]
    x = x.reshape(nbatch, N_KV, GROUP_R, 8, w)[:, :, :, :dec_seq]
    return x.transpose(0, 3, 1, 2, 4).reshape(nbatch * dec_seq, N_HEADS * w)


def kernel(x_prompt, x_sample, cache_kv, page_table, cache_win, state_pool, state_conv, c_prompt, c_sample,
           ada_w, ada_b, norm_g, attn_w_in, attn_w_out, pool_w, pool_scale, cmp_pos, cmp_w1, cmp_w2,
           conv_pw1, conv_dw, conv_dw_b, conv_ln_g, conv_ln_b, conv_pw2, ffn_w1, ffn_w2, final_g):
    nb_p, seq, d = x_prompt.shape
    nb_s, dec_seq, _ = x_sample.shape
    n_p, n_s = nb_p * seq, nb_s * dec_seq
    n_pages = page_table.shape[1]
    past = n_pages * PAGE_SIZE
    wb = cache_win.shape[3]
    tm = min(ROW_TILE, seq)
    tq = min(ATTN_TILE, seq)
    pool_width = pool_w.shape[1] * pool_w.shape[2]
    att_width = N_HEADS * HEAD_DIM

    mods = _ada(jnp.concatenate([c_prompt, c_sample], axis=0), ada_w, ada_b)

    def mod_p(layer, k):
        return mods[layer, :nb_p, k * d:(k + 1) * d].reshape(nb_p, 1, d)

    def mod_s(layer, k):
        return jnp.repeat(mods[layer, nb_p:, k * d:(k + 1) * d], dec_seq, axis=0).reshape(1, n_s, d)

    y_p = x_prompt.reshape(n_p, d)
    y_s = x_sample.reshape(n_s, d)
    row = lambda v: v.reshape(1, -1)

    w_in = attn_w_in[0]
    w_in_pad = jnp.pad(w_in, ((0, 0), (0, (-w_in.shape[1]) % LANES))).astype(BF16)
    cos_p, sin_p = _rope_tables(jnp.arange(seq))
    cos_s, sin_s = _rope_tables(past + jnp.arange(n_s) % dec_seq)
    g0 = row(norm_g[0, 0])
    u_p, q_p, gate_p, kvt_p, wint_p, kvbt_p, cmp_in_p = _pre0(
        y_p, g0, mod_p(0, 0), mod_p(0, 1), w_in_pad, cos_p, sin_p, tm, seq // tm, seq // tm, seq=seq)
    u_s, q_s, gate_s, kv_s, win_s = _pre0(y_s, g0, mod_s(0, 0), mod_s(0, 1), w_in_pad, cos_s, sin_s, n_s, 1, 1)

    pool_w_b = pool_w[0].astype(BF16)
    pool_sc = row(pool_scale[0])
    ypool_p = _pool_prompt(u_p, pool_w_b, pool_sc, seq, tm)
    u_s3 = u_s.reshape(nb_s, dec_seq, pool_width)
    pool_ext = jnp.concatenate([
        jnp.zeros((nb_s, POOL_PAD - POOL_HIST, pool_width), F32), state_pool[0], u_s3,
        jnp.zeros((nb_s, 8 - dec_seq, pool_width), F32)], axis=1)
    ypool_s = _pool_sample(pool_ext, pool_w_b, pool_sc, past)[:, :dec_seq].reshape(n_s, pool_width)

    pos_ab, w1_ab, w2_bd = _cmp_weights(cmp_pos[0], cmp_w1[0], cmp_w2[0])
    cmp_in_p4 = cmp_in_p.reshape(2, nb_p, seq, LANES)
    _, cmp_t_p = _cmp_combine(_cmp_rows(cmp_in_p4, pos_ab, w1_ab, nb_p, seq), w2_bd)
    cache_t = cache_kv[0].transpose(0, 1, 3, 4, 2)
    pt_flat = page_table.reshape(-1)
    part_past = _cmp_pages(cache_t, pt_flat, n_pages, pos_ab, w1_ab, nb_s)
    tail_rows = 2 * PAGE_SIZE
    kv_s4 = kv_s.reshape(4, nb_s, dec_seq, LANES)
    tail4 = jnp.pad(kv_s4, ((0, 0), (0, 0), (0, tail_rows - dec_seq), (0, 0)))
    part_tail = _cmp_rows(tail4, pos_ab, w1_ab, nb_s, tail_rows)
    cmp_s, _ = _cmp_combine(jnp.concatenate([part_past, part_tail], axis=2), w2_bd)
    total_len = past + dec_seq
    padded_len = -(-total_len // SEL_BLOCK) * SEL_BLOCK
    n_cmp_s = padded_len // CMP_STRIDE - CMP_BLOCK // CMP_STRIDE + 1
    nblk_s = padded_len // SEL_BLOCK

    assert seq // SEL_BLOCK <= LANES and dec_seq <= SEL_BLOCK
    ncmp_p = cmp_t_p.shape[3]
    mt_p = jnp.asarray(_imp_matrix(ncmp_p, LANES).T, BF16)
    e_mat = jnp.asarray((np.arange(seq)[None, :] // SEL_BLOCK == np.arange(LANES)[:, None]).astype(np.float32),
                        BF16)
    o_p = _attn_prompt(q_p, cmp_t_p, kvbt_p, gate_p, mt_p, e_mat, nb_p, seq, tq)

    qg_s = _group_rows(q_s, nb_s, dec_seq)
    nblk_pad = -(-nblk_s // LANES) * LANES
    m_s = jnp.asarray(_imp_matrix(cmp_s.shape[2], nblk_pad), BF16)
    o_cmp_s, picked = _attn_sample_cmp(qg_s, cmp_s, m_s, n_cmp_s, past, nblk_s)
    idx = picked.reshape(nb_s, N_KV, 8, LANES)[:, :, :dec_seq, :N_SEL].transpose(0, 2, 1, 3).reshape(-1)

    def new_rows_t(rows):
        x = rows.reshape(rows.shape[0], nb_s, dec_seq, N_KV, HEAD_DIM).transpose(0, 1, 3, 4, 2)
        return jnp.pad(x, ((0, 0), (0, 0), (0, 0), (0, 0), (0, LANES - dec_seq)))

    tail_t = new_rows_t(kv_s[2:4])
    wnew_t = new_rows_t(win_s)
    win_t = cache_win[0].transpose(0, 1, 3, 4, 2)
    gates_g = jnp.pad(_group_rows(gate_s[:, :3 * N_HEADS], nb_s, dec_seq), ((0, 0), (0, 0), (0, 0), (0, LANES - 3)))
    o_s = _attn_sample_sel(idx, pt_flat, cache_t, qg_s, tail_t, win_t, wnew_t, o_cmp_s, gates_g,
                           dec_seq, n_pages, past)
    o_s = _ungroup_rows(o_s, nb_s, dec_seq).astype(BF16)

    w_out = attn_w_out[0].astype(BF16)
    wo_list = [w_out[:pool_width], w_out[pool_width:]]
    g1 = row(norm_g[0, 1])
    ffn1_0, ffn2_0 = ffn_w1[0].astype(BF16), ffn_w2[0].astype(BF16)
    y_p = _post(y_p, [ypool_p, o_p], wo_list, mod_p(0, 2), g1, mod_p(0, 3), mod_p(0, 4), mod_p(0, 5),
                ffn1_0, ffn2_0, None, tm, seq // tm)
    y_s = _post(y_s, [ypool_s, o_s], wo_list, mod_s(0, 2), g1, mod_s(0, 3), mod_s(0, 4), mod_s(0, 5),
                ffn1_0, ffn2_0, None, n_s, 1)

    g0 = row(norm_g[1, 0])
    pw1 = conv_pw1[0].astype(BF16)
    uc_p = _pre1(y_p, g0, mod_p(1, 0), mod_p(1, 1), pw1, tm, seq // tm)
    uc_s = _pre1(y_s, g0, mod_s(1, 0), mod_s(1, 1), pw1, n_s, 1)
    dwb, lng, lnb = row(conv_dw_b[0]), row(conv_ln_g[0]), row(conv_ln_b[0])
    cv_p = _conv_prompt(uc_p, conv_dw[0], dwb, lng, lnb, seq, tm)
    uc_s3 = uc_s.reshape(nb_s, dec_seq, d)
    conv_ext = jnp.concatenate([
        jnp.zeros((nb_s, CONV_PAD - CONV_HIST, d), F32), state_conv[0], uc_s3,
        jnp.zeros((nb_s, 8 - dec_seq, d), F32)], axis=1)
    cv_s = _conv_sample(conv_ext, conv_dw[0], dwb, lng, lnb)[:, :dec_seq].reshape(n_s, d)

    pw2 = conv_pw2[0].astype(BF16)
    g1 = row(norm_g[1, 1])
    ffn1_1, ffn2_1 = ffn_w1[1].astype(BF16), ffn_w2[1].astype(BF16)
    fg = row(final_g)
    y_p = _post(y_p, [cv_p], [pw2], mod_p(1, 2), g1, mod_p(1, 3), mod_p(1, 4), mod_p(1, 5),
                ffn1_1, ffn2_1, fg, tm, seq // tm)
    y_s = _post(y_s, [cv_s], [pw2], mod_s(1, 2), g1, mod_s(1, 3), mod_s(1, 4), mod_s(1, 5),
                ffn1_1, ffn2_1, fg, n_s, 1)

    y_prompt = y_p.reshape(nb_p, seq, d)
    y_sample = y_s.reshape(nb_s, dec_seq, d)
    kv_prompt = kvt_p.transpose(0, 1, 4, 2, 3)[None]
    kv_sample = kv_s.reshape(1, 4, nb_s, dec_seq, N_KV, HEAD_DIM)
    if seq >= wb:
        win_prompt_t = wint_p[..., seq - wb:]
    else:
        win_prompt_t = jnp.pad(wint_p, ((0, 0),) * 4 + ((wb - seq, 0),))
    win_prompt = win_prompt_t.transpose(0, 1, 4, 2, 3)
    win_sample_t = jnp.concatenate([win_t, wnew_t[..., :dec_seq]], axis=-1)[..., -wb:]
    win_sample = win_sample_t.transpose(0, 1, 4, 2, 3)

    def last_rows(x3, hist, state):
        full = x3 if state is None and x3.shape[1] >= hist else jnp.concatenate(
            [jnp.zeros((x3.shape[0], hist, x3.shape[2]), F32) if state is None else state, x3], axis=1)
        return full[:, full.shape[1] - hist:]

    pool_prompt = last_rows(u_p.reshape(nb_p, seq, pool_width), POOL_HIST, None)
    pool_sample = last_rows(u_s3, POOL_HIST, state_pool[0])
    conv_prompt = last_rows(uc_p.reshape(nb_p, seq, d), CONV_HIST, None)
    conv_sample = last_rows(uc_s3, CONV_HIST, state_conv[0])
    return (y_prompt, y_sample, kv_prompt, kv_sample, win_prompt[None], win_sample[None],
            pool_prompt[None], pool_sample[None], conv_prompt[None], conv_sample[None])
```

```python
import functools

import numpy as np
import jax
import jax.numpy as jnp
from jax import lax
from jax.experimental import pallas as pl
from jax.experimental.pallas import tpu as pltpu

F32 = jnp.float32
BF16 = jnp.bfloat16

NORM_EPS = 1e-6
N_HEADS = 8
HEAD_DIM = 64
N_KV = 2
GROUP_R = N_HEADS // N_KV
POOL_WINDOWS = (2, 4, 8, 16)
POOL_HIST = 15
POOL_PAD = 16
CMP_BLOCK = 32
CMP_STRIDE = 16
SEL_BLOCK = 64
N_SEL = 16
WINDOW = 512
PAGE_SIZE = 128
ROPE_THETA = 10000.0
CONV_WIDTH = 31
CONV_HIST = CONV_WIDTH - 1
CONV_PAD = 32
LANES = 128
NEG = -1e9
V7X_VMEM_LIMIT = 56 * 1024 * 1024


def _cparams(*sem):
    return pltpu.CompilerParams(dimension_semantics=sem, vmem_limit_bytes=V7X_VMEM_LIMIT)


def _dot(a, b):
    return jnp.dot(a, b, preferred_element_type=F32)


def _dot_nt(a, b):
    return lax.dot_general(a, b, (((1,), (1,)), ((), ())), preferred_element_type=F32)


def _sigmoid(x):
    return 1.0 / (1.0 + jnp.exp(-x))


def _silu(x):
    return x * _sigmoid(x)


def _modulate(x, g, shift, scale):
    ms = jnp.mean(x * x, axis=-1, keepdims=True)
    y = x * lax.rsqrt(ms + NORM_EPS)
    return (y * g) * (1.0 + scale) + shift


def _ada_kernel(c_ref, w_ref, b_ref, o_ref):
    a = _silu(c_ref[...]).astype(BF16)
    o_ref[0] = _dot(a, w_ref[0].astype(BF16)) + b_ref[0]


def _ada(c_all, ada_w, ada_b):
    depth, d, n6 = ada_w.shape
    bc = c_all.shape[0]
    tn = n6 // 4
    return pl.pallas_call(
        _ada_kernel,
        out_shape=jax.ShapeDtypeStruct((depth, bc, n6), F32),
        grid=(depth, n6 // tn),
        in_specs=[
            pl.BlockSpec((bc, d), lambda l, j: (0, 0)),
            pl.BlockSpec((1, d, tn), lambda l, j: (l, 0, j)),
            pl.BlockSpec((1, 1, tn), lambda l, j: (l, 0, j)),
        ],
        out_specs=pl.BlockSpec((1, bc, tn), lambda l, j: (l, 0, j)),
        compiler_params=_cparams("parallel", "parallel"),
        name="ada_mod",
    )(c_all, ada_w, ada_b.reshape(depth, 1, n6))


def _rope(x, cos, sin_signed):
    lane = lax.broadcasted_iota(jnp.int32, x.shape, 1)
    first = (lane % HEAD_DIM) < (HEAD_DIM // 2)
    swapped = jnp.where(first, pltpu.roll(x, LANES - HEAD_DIM // 2, 1), pltpu.roll(x, HEAD_DIM // 2, 1))
    return x * cos + swapped * sin_signed


def _pre0_project(x_ref, g_ref, sh_ref, sc_ref, w_ref, cos_ref, sin_ref, u_ref, q_ref, gate_ref):
    h = _modulate(x_ref[...], g_ref[...], sh_ref[0], sc_ref[0]).astype(BF16)
    cos = cos_ref[...]
    sin = sin_ref[...]

    def proj(c0, width):
        return _dot_nt(h, w_ref[c0:c0 + width, :])

    u_ref[...] = proj(0, 512)
    for j in range(4):
        qj = _rope(proj(512 + LANES * j, LANES), cos, sin)
        q_ref[:, LANES * j:LANES * (j + 1)] = (qj * (HEAD_DIM ** -0.5)).astype(BF16)
    gate_ref[...] = _sigmoid(proj(1792, LANES))
    kc = _rope(proj(1024, LANES), cos, sin)
    vc = proj(1152, LANES)
    ks = _rope(proj(1280, LANES), cos, sin)
    vs = proj(1408, LANES)
    kw = _rope(proj(1536, LANES), cos, sin)
    vw = proj(1664, LANES)
    return kc, vc, ks, vs, kw, vw


def _pre0_rows_kernel(x_ref, g_ref, sh_ref, sc_ref, w_ref, cos_ref, sin_ref,
                      u_ref, q_ref, gate_ref, kv_ref, win_ref):
    kc, vc, ks, vs, kw, vw = _pre0_project(x_ref, g_ref, sh_ref, sc_ref, w_ref, cos_ref, sin_ref,
                                           u_ref, q_ref, gate_ref)
    for k, seg in enumerate((kc, vc, ks, vs)):
        kv_ref[k] = seg
    win_ref[0] = kw
    win_ref[1] = vw


def _store_transposed(ref, lead, x):
    xt = x.T
    for g in range(N_KV):
        ref[lead + (g,)] = xt[HEAD_DIM * g:HEAD_DIM * (g + 1), :].astype(ref.dtype)


def _pre0_seq_kernel(x_ref, g_ref, sh_ref, sc_ref, w_ref, cos_ref, sin_ref,
                     u_ref, q_ref, gate_ref, kvt_ref, wint_ref, kvbt_ref, cmp_ref):
    kc, vc, ks, vs, kw, vw = _pre0_project(x_ref, g_ref, sh_ref, sc_ref, w_ref, cos_ref, sin_ref,
                                           u_ref, q_ref, gate_ref)
    for k, seg in enumerate((kc, vc, ks, vs)):
        _store_transposed(kvt_ref, (k, 0), seg)
    _store_transposed(wint_ref, (0, 0), kw)
    _store_transposed(wint_ref, (1, 0), vw)
    for k, seg in enumerate((ks, vs, kw, vw)):
        _store_transposed(kvbt_ref, (k, 0), seg)
    cmp_ref[0] = kc
    cmp_ref[1] = vc


def _pre0(x2, g, shift, scale, w_pad, cos, sin, tm, rows_per_mod, pos_tiles, seq=None):
    n, d = x2.shape
    r = shift.shape[1]
    mod_spec = pl.BlockSpec((1, r, d), lambda i: (i // rows_per_mod, 0, 0))
    pos_spec = pl.BlockSpec((tm, LANES), lambda i: (i % pos_tiles, 0))
    row_spec = lambda w: pl.BlockSpec((tm, w), lambda i: (i, 0))
    out_shape = [jax.ShapeDtypeStruct((n, 512), F32), jax.ShapeDtypeStruct((n, 512), BF16),
                 jax.ShapeDtypeStruct((n, LANES), F32)]
    out_specs = [row_spec(512), row_spec(512), row_spec(LANES)]
    if seq is None:
        body = _pre0_rows_kernel
        out_shape += [jax.ShapeDtypeStruct((4, n, LANES), F32), jax.ShapeDtypeStruct((2, n, LANES), F32)]
        out_specs += [pl.BlockSpec((4, tm, LANES), lambda i: (0, i, 0)),
                      pl.BlockSpec((2, tm, LANES), lambda i: (0, i, 0))]
    else:
        body = _pre0_seq_kernel
        nb, tpb = n // seq, seq // tm
        t_spec = lambda k: pl.BlockSpec((k, 1, N_KV, HEAD_DIM, tm), lambda i: (0, i // tpb, 0, 0, i % tpb))
        out_shape += [jax.ShapeDtypeStruct((4, nb, N_KV, HEAD_DIM, seq), F32),
                      jax.ShapeDtypeStruct((2, nb, N_KV, HEAD_DIM, seq), F32),
                      jax.ShapeDtypeStruct((4, nb, N_KV, HEAD_DIM, seq), BF16),
                      jax.ShapeDtypeStruct((2, n, LANES), F32)]
        out_specs += [t_spec(4), t_spec(2), t_spec(4), pl.BlockSpec((2, tm, LANES), lambda i: (0, i, 0))]
    return pl.pallas_call(
        body,
        out_shape=tuple(out_shape),
        grid=(n // tm,),
        in_specs=[
            pl.BlockSpec((tm, d), lambda i: (i, 0)),
            pl.BlockSpec((1, d), lambda i: (0, 0)),
            mod_spec, mod_spec,
            pl.BlockSpec(w_pad.shape, lambda i: (0, 0)),
            pos_spec, pos_spec,
        ],
        out_specs=tuple(out_specs),
        compiler_params=_cparams("parallel"),
        name="pre0_in_proj",
    )(x2, g, shift, scale, w_pad, cos, sin)


def _rope_tables(pos):
    half = HEAD_DIM // 2
    inv = ROPE_THETA ** (-np.arange(half, dtype=np.float64) / half)
    ang = np.asarray(pos, np.float64)[:, None] * inv[None, :]
    cos, sin = np.cos(ang), np.sin(ang)
    cos_t = np.tile(np.concatenate([cos, cos], axis=1), (1, LANES // HEAD_DIM))
    sin_t = np.tile(np.concatenate([-sin, sin], axis=1), (1, LANES // HEAD_DIM))
    return jnp.asarray(cos_t, F32), jnp.asarray(sin_t, F32)


def _pool_body(ext_ref, t_rows, pos0, w_ref, scale_ref, out_ref):
    pos = pos0 + lax.broadcasted_iota(jnp.int32, (t_rows, 1), 0)
    for g, w in enumerate(POOL_WINDOWS):
        cols = slice(LANES * g, LANES * (g + 1))
        x = ext_ref[POOL_PAD:POOL_PAD + t_rows, cols]
        s = x
        for j in range(1, w):
            s = s + ext_ref[POOL_PAD - j:POOL_PAD - j + t_rows, cols]
        cnt = jnp.minimum(pos + 1, w).astype(F32)
        dlt = (s / cnt - x).astype(BF16)
        y = _dot(dlt, w_ref[g]) * scale_ref[:, cols]
        out_ref[:, cols] = y.astype(out_ref.dtype)


def _pool_prompt_kernel(x_ref, prev_ref, w_ref, scale_ref, out_ref, ext_ref, *, tiles_per_batch, tm):
    t_in_b = pl.program_id(0) % tiles_per_batch
    ext_ref[0:POOL_PAD, :] = jnp.where(t_in_b > 0, prev_ref[...], 0.0)
    ext_ref[POOL_PAD:, :] = x_ref[...]
    _pool_body(ext_ref, tm, t_in_b * tm, w_ref, scale_ref, out_ref)


def _pool_prompt(u, w_grp, scale, seq, tm):
    n, c = u.shape
    tpb = seq // tm
    hb = tm // POOL_PAD
    return pl.pallas_call(
        functools.partial(_pool_prompt_kernel, tiles_per_batch=tpb, tm=tm),
        out_shape=jax.ShapeDtypeStruct((n, c), BF16),
        grid=(n // tm,),
        in_specs=[
            pl.BlockSpec((tm, c), lambda i: (i, 0)),
            pl.BlockSpec((POOL_PAD, c), lambda i: (jnp.maximum(i * hb - 1, 0), 0)),
            pl.BlockSpec(w_grp.shape, lambda i: (0, 0, 0)),
            pl.BlockSpec((1, c), lambda i: (0, 0)),
        ],
        out_specs=pl.BlockSpec((tm, c), lambda i: (i, 0)),
        scratch_shapes=[pltpu.VMEM((POOL_PAD + tm, c), F32)],
        compiler_params=_cparams("parallel"),
        name="pool_prompt",
    )(u, u, w_grp, scale)


def _pool_sample_kernel(ext_ref, w_ref, scale_ref, out_ref, *, pos0, t_rows):
    _pool_body(ext_ref.at[0], t_rows, pos0, w_ref, scale_ref, out_ref.at[0])


def _pool_sample(ext, w_grp, scale, pos0):
    b, rows, c = ext.shape
    t_rows = rows - POOL_PAD
    return pl.pallas_call(
        functools.partial(_pool_sample_kernel, pos0=pos0, t_rows=t_rows),
        out_shape=jax.ShapeDtypeStruct((b, t_rows, c), BF16),
        grid=(b,),
        in_specs=[
            pl.BlockSpec((1, rows, c), lambda i: (i, 0, 0)),
            pl.BlockSpec(w_grp.shape, lambda i: (0, 0, 0)),
            pl.BlockSpec((1, c), lambda i: (0, 0)),
        ],
        out_specs=pl.BlockSpec((1, t_rows, c), lambda i: (i, 0, 0)),
        compiler_params=_cparams("parallel"),
        name="pool_sample",
    )(ext, w_grp, scale)


def _chunk_rows(ref2d, rows):
    n = rows // CMP_STRIDE
    return jnp.concatenate([ref2d[pl.ds(r, n, stride=CMP_STRIDE), :] for r in range(CMP_STRIDE)], axis=1)


def _cmp_partial(a, pos_ref, w_ref, p_ref):
    hid2 = w_ref.shape[2] // 2
    p_ref[0, 0, :, 0:hid2] = _dot((a + pos_ref[0, 0:1, :]).astype(BF16), w_ref[0, :, 0:hid2])
    p_ref[0, 0, :, hid2:] = _dot((a + pos_ref[0, 1:2, :]).astype(BF16), w_ref[0, :, hid2:])


def _cmp_rows_kernel(x_ref, pos_ref, w_ref, p_ref, *, rows):
    _cmp_partial(_chunk_rows(x_ref.at[0, 0], rows), pos_ref, w_ref, p_ref)


def _cmp_rows(x4, pos_ab, w1_ab, nbatch, rows):
    nch = rows // CMP_STRIDE
    return pl.pallas_call(
        functools.partial(_cmp_rows_kernel, rows=rows),
        out_shape=jax.ShapeDtypeStruct((2, nbatch, nch, w1_ab.shape[2]), F32),
        grid=(2, nbatch),
        in_specs=[
            pl.BlockSpec((1, 1, rows, LANES), lambda s, b: (s, b, 0, 0)),
            pl.BlockSpec((1, 2, pos_ab.shape[2]), lambda s, b: (s, 0, 0)),
            pl.BlockSpec((1,) + w1_ab.shape[1:], lambda s, b: (s, 0, 0)),
        ],
        out_specs=pl.BlockSpec((1, 1, nch, w1_ab.shape[2]), lambda s, b: (s, b, 0, 0)),
        compiler_params=_cparams("parallel", "parallel"),
        name="cmp_rows",
    )(x4, pos_ab, w1_ab)


PAGES_PER_STEP = 64


def _cmp_pages_kernel(pt_ref, *refs, pps):
    del pt_ref
    page_refs = refs[:pps]
    pos_ref, w_ref, p_ref, rows_ref = refs[pps:]
    for k, r in enumerate(page_refs):
        page_t = jnp.concatenate([r[0, 0, g] for g in range(N_KV)], axis=0)
        rows_ref[k * PAGE_SIZE:(k + 1) * PAGE_SIZE, :] = page_t.T
    _cmp_partial(_chunk_rows(rows_ref, pps * PAGE_SIZE), pos_ref, w_ref, p_ref)


def _cmp_pages(cache_t, page_table_flat, n_pages, pos_ab, w1_ab, nbatch):
    pps = min(PAGES_PER_STEP, n_pages)
    assert n_pages % pps == 0
    steps = n_pages // pps
    nch = pps * PAGE_SIZE // CMP_STRIDE

    def page_spec(k):
        return pl.BlockSpec(
            (1, 1, N_KV, HEAD_DIM, PAGE_SIZE),
            lambda s, b, j, pt: (s, pt[b * n_pages + j * pps + k], 0, 0, 0))

    grid_spec = pltpu.PrefetchScalarGridSpec(
        num_scalar_prefetch=1,
        grid=(2, nbatch, steps),
        in_specs=[page_spec(k) for k in range(pps)] + [
            pl.BlockSpec((1, 2, pos_ab.shape[2]), lambda s, b, j, pt: (s, 0, 0)),
            pl.BlockSpec((1,) + w1_ab.shape[1:], lambda s, b, j, pt: (s, 0, 0)),
        ],
        out_specs=pl.BlockSpec((1, 1, nch, w1_ab.shape[2]), lambda s, b, j, pt: (s, b, j, 0)),
        scratch_shapes=[pltpu.VMEM((pps * PAGE_SIZE, LANES), F32)],
    )
    return pl.pallas_call(
        functools.partial(_cmp_pages_kernel, pps=pps),
        out_shape=jax.ShapeDtypeStruct((2, nbatch, steps * nch, w1_ab.shape[2]), F32),
        grid_spec=grid_spec,
        compiler_params=_cparams("parallel", "parallel", "parallel"),
        name="cmp_pages",
    )(page_table_flat, *([cache_t] * pps), pos_ab, w1_ab)


def _gelu_tanh(x):
    return x * (0.5 * (1.0 + jnp.tanh(np.sqrt(2.0 / np.pi).astype(np.float32) * (x + 0.044715 * (x * x * x)))))


def _cmp_combine_kernel(p_ref, w2_ref, w2t_ref, o_ref, ot_ref, pb_ref, *, n_p):
    hid2 = p_ref.shape[3] // 2
    pb_ref[0:n_p, :] = p_ref[0, 0, :, hid2:]
    pb_ref[n_p:n_p + 8, :] = jnp.zeros((8, hid2), F32)
    hsum = p_ref[0, 0, :, 0:hid2] + pb_ref[1:n_p + 1, :]
    act = _gelu_tanh(hsum).astype(BF16)
    o_ref[0, 0] = _dot(act, w2_ref[0]).astype(o_ref.dtype)
    ot_ref[0, 0] = _dot_nt(w2t_ref[0], act).astype(ot_ref.dtype)


def _cmp_combine(p, w2_bd):
    _, nbatch, n_p, width = p.shape
    w2t_bd = w2_bd.transpose(0, 2, 1)
    return pl.pallas_call(
        functools.partial(_cmp_combine_kernel, n_p=n_p),
        out_shape=(jax.ShapeDtypeStruct((2, nbatch, n_p, LANES), BF16),
                   jax.ShapeDtypeStruct((2, nbatch, LANES, n_p), BF16)),
        grid=(2, nbatch),
        in_specs=[
            pl.BlockSpec((1, 1, n_p, width), lambda s, b: (s, b, 0, 0)),
            pl.BlockSpec((1,) + w2_bd.shape[1:], lambda s, b: (s, 0, 0)),
            pl.BlockSpec((1,) + w2t_bd.shape[1:], lambda s, b: (s, 0, 0)),
        ],
        out_specs=(pl.BlockSpec((1, 1, n_p, LANES), lambda s, b: (s, b, 0, 0)),
                   pl.BlockSpec((1, 1, LANES, n_p), lambda s, b: (s, b, 0, 0))),
        scratch_shapes=[pltpu.VMEM((n_p + 8, width // 2), F32)],
        compiler_params=_cparams("parallel", "parallel"),
        name="cmp_combine",
    )(p, w2_bd, w2t_bd)


def _cmp_weights(cmp_pos, cmp_w1, cmp_w2):
    hid = cmp_w1.shape[2]
    half = CMP_STRIDE * HEAD_DIM
    eye = jnp.eye(N_KV, dtype=F32)
    pos_ab = jnp.tile(cmp_pos.reshape(2, 2, CMP_STRIDE, 1, HEAD_DIM), (1, 1, 1, N_KV, 1))
    pos_ab = pos_ab.reshape(2, 2, CMP_STRIDE * N_KV * HEAD_DIM)
    w1 = cmp_w1.reshape(2, 2, CMP_STRIDE, HEAD_DIM, hid)
    w1_bd = jnp.einsum("shrdj,gk->shrgdkj", w1, eye)
    w1_bd = w1_bd.reshape(2, 2, CMP_STRIDE * N_KV * HEAD_DIM, N_KV * hid)
    w1_ab = jnp.concatenate([w1_bd[:, 0], w1_bd[:, 1]], axis=2).astype(BF16)
    w2_bd = jnp.einsum("sjd,gk->sgjkd", cmp_w2, eye).reshape(2, N_KV * hid, N_KV * HEAD_DIM).astype(BF16)
    del half
    return pos_ab, w1_ab, w2_bd


def _online_update(s, v_aug_t, state):
    tq, ck = s.shape
    m_new = jnp.broadcast_to(jnp.max(s, axis=-1, keepdims=True), (tq, LANES))
    if state is not None:
        acc_old, m_old = state
        m_new = jnp.maximum(m_old, m_new)
    p = jnp.exp(s - pltpu.repeat(m_new, ck // LANES, axis=1)).astype(BF16)
    acc = _dot_nt(p, v_aug_t)
    if state is not None:
        acc = jnp.exp(m_old - m_new) * acc_old + acc
    return acc, m_new


def _attn_prompt_kernel(q_ref, cmpt_ref, kst_ref, vst_ref, kwt_ref, vwt_ref, gate_ref, mt_ref, e_ref,
                        o_ref, kaug_s, vaug_s, vwaug_s, *, tq, nblk):
    i = pl.program_id(1)
    s0 = i * tq
    seq = kst_ref.shape[4]
    ncmp = cmpt_ref.shape[3]

    @pl.when(i == 0)
    def _():
        ones_row = (lax.broadcasted_iota(jnp.int32, (HEAD_DIM, seq), 0) == 0).astype(BF16)
        for g in range(N_KV):
            kaug_s[g, 0:LANES, :] = e_ref[...]
            kaug_s[g, LANES:LANES + HEAD_DIM, :] = kst_ref[0, 0, g]
            vaug_s[g, 0:HEAD_DIM, :] = vst_ref[0, 0, g]
            vaug_s[g, HEAD_DIM:, :] = ones_row
            vwaug_s[g, 0:HEAD_DIM, :] = vwt_ref[0, 0, g]
            vwaug_s[g, HEAD_DIM:, :] = ones_row

    row = lax.broadcasted_iota(jnp.int32, (tq, 1), 0)
    col = lax.broadcasted_iota(jnp.int32, (1, tq), 1)
    qpos = s0 + row
    gates = gate_ref[...]
    c_end = lax.broadcasted_iota(jnp.int32, (1, ncmp), 1) * CMP_STRIDE + (CMP_BLOCK - 1)
    mask_c = c_end <= qpos
    qp_l = s0 + col
    cur = qp_l // SEL_BLOCK
    n_rb = -(-nblk // 8)
    qs = [q_ref[:, HEAD_DIM * hh:HEAD_DIM * (hh + 1)] for hh in range(N_HEADS)]

    o_cmp, q_aug = [], []
    for g in range(N_KV):
        kct_g = cmpt_ref[0, 0, HEAD_DIM * g:HEAD_DIM * (g + 1), :]
        vct_g = cmpt_ref[1, 0, HEAD_DIM * g:HEAD_DIM * (g + 1), :]
        p_sum = jnp.zeros((tq, ncmp), F32)
        for qh in qs[GROUP_R * g:GROUP_R * (g + 1)]:
            s = jnp.where(mask_c, _dot(qh, kct_g), -jnp.inf)
            m = jnp.max(s, axis=-1, keepdims=True)
            m = jnp.where(m == -jnp.inf, 0.0, m)
            e = jnp.exp(s - m)
            d = jnp.sum(e, axis=-1, keepdims=True)
            p = e / jnp.where(d > 0, d, 1.0)
            p_sum = p_sum + p
            o_cmp.append(_dot_nt(p.astype(BF16), vct_g))

        p_hi = p_sum.astype(BF16)
        p_lo = (p_sum - p_hi.astype(F32)).astype(BF16)
        imp = _dot_nt(mt_ref[...], p_hi) + _dot_nt(mt_ref[...], p_lo)
        score, valid = [], []
        for r in range(n_rb):
            jb = 8 * r + lax.broadcasted_iota(jnp.int32, (8, 1), 0)
            ok = (jb * SEL_BLOCK <= qp_l) & (jb < nblk)
            forced = (jb == 0) | (jb == cur) | (jb == cur - 1)
            valid.append(ok)
            score.append(jnp.where(ok, jnp.where(forced, jnp.inf, imp[8 * r:8 * r + 8, :]), -jnp.inf))
        cnt = [jnp.zeros((8, tq), jnp.int32) for _ in range(n_rb)]
        for j in range(nblk):
            rj = jnp.broadcast_to(score[j // 8][j % 8:j % 8 + 1, :], (8, tq))
            for r in range(n_rb):
                if 8 * r > j:
                    beats = rj >= score[r]
                elif 8 * r + 7 < j:
                    beats = rj > score[r]
                else:
                    later = 8 * r + lax.broadcasted_iota(jnp.int32, (8, 1), 0) > j
                    beats = (rj > score[r]) | ((rj == score[r]) & later)
                cnt[r] = cnt[r] + jnp.where(beats, 1, 0)
        selneg_t = [jnp.where((cnt[r] < N_SEL) & valid[r], 0.0, NEG) for r in range(n_rb)]
        selneg_t.append(jnp.zeros((LANES - 8 * n_rb, tq), F32))
        selneg = jnp.concatenate(selneg_t, axis=0).T.astype(BF16)
        q_aug += [jnp.concatenate([selneg, qh], axis=1) for qh in qs[GROUP_R * g:GROUP_R * (g + 1)]]

    ck = 2 * tq

    def sel_chunk(c0, state, keep):
        out = []
        for g in range(N_KV):
            k_aug = kaug_s[g, :, pl.ds(c0, ck)]
            v_aug = vaug_s[g, :, pl.ds(c0, ck)]
            for hh in range(GROUP_R * g, GROUP_R * (g + 1)):
                s = _dot(q_aug[hh], k_aug)
                if keep is not None:
                    s = jnp.where(keep, s, NEG)
                out.append(_online_update(s, v_aug, None if state is None else state[hh]))
        return tuple(out)

    n_full = i // 2
    last0 = pl.multiple_of(jnp.maximum(i - 1, 0) * tq, tq)
    kpos = last0 + lax.broadcasted_iota(jnp.int32, (1, ck), 1)
    state = sel_chunk(last0, None, (kpos <= qpos) & (kpos >= n_full * ck))
    state = lax.fori_loop(0, n_full, lambda c, st: sel_chunk(pl.multiple_of(c * ck, ck), st, None), state)
    o_sel = [acc[:, 0:HEAD_DIM] / acc[:, HEAD_DIM:HEAD_DIM + 1] for acc, _ in state]

    keep_w = (kpos <= qpos) & (kpos > qpos - WINDOW)
    wstate = []
    for g in range(N_KV):
        k_t = kwt_ref[0, 0, g, :, pl.ds(last0, ck)]
        v_aug = vwaug_s[g, :, pl.ds(last0, ck)]
        for hh in range(GROUP_R * g, GROUP_R * (g + 1)):
            wstate.append(_online_update(jnp.where(keep_w, _dot(qs[hh], k_t), NEG), v_aug, None))
    for hh in range(N_HEADS):
        acc_w = wstate[hh][0]
        o_win = acc_w[:, 0:HEAD_DIM] / acc_w[:, HEAD_DIM:HEAD_DIM + 1]
        o = (o_cmp[hh] * gates[:, 3 * hh:3 * hh + 1] + o_sel[hh] * gates[:, 3 * hh + 1:3 * hh + 2]
             + o_win * gates[:, 3 * hh + 2:3 * hh + 3])
        o_ref[:, HEAD_DIM * hh:HEAD_DIM * (hh + 1)] = o.astype(o_ref.dtype)


def _attn_prompt(q, cmp_t, kvbt, gates, mt, e_mat, nbatch, seq, tq):
    assert tq >= WINDOW and seq >= 2 * tq
    n = q.shape[0]
    nq = seq // tq
    ncmp = cmp_t.shape[3]
    kv_spec = lambda slot: pl.BlockSpec((1, 1, N_KV, HEAD_DIM, seq), lambda b, i: (slot, b, 0, 0, 0))
    return pl.pallas_call(
        functools.partial(_attn_prompt_kernel, tq=tq, nblk=seq // SEL_BLOCK),
        out_shape=jax.ShapeDtypeStruct((n, N_HEADS * HEAD_DIM), BF16),
        grid=(nbatch, nq),
        in_specs=[
            pl.BlockSpec((tq, N_HEADS * HEAD_DIM), lambda b, i: (b * nq + i, 0)),
            pl.BlockSpec((2, 1, LANES, ncmp), lambda b, i: (0, b, 0, 0)),
            kv_spec(0), kv_spec(1), kv_spec(2), kv_spec(3),
            pl.BlockSpec((tq, LANES), lambda b, i: (b * nq + i, 0)),
            pl.BlockSpec(mt.shape, lambda b, i: (0, 0)),
            pl.BlockSpec(e_mat.shape, lambda b, i: (0, 0)),
        ],
        out_specs=pl.BlockSpec((tq, N_HEADS * HEAD_DIM), lambda b, i: (b * nq + i, 0)),
        scratch_shapes=[
            pltpu.VMEM((N_KV, LANES + HEAD_DIM, seq), BF16),
            pltpu.VMEM((N_KV, LANES, seq), BF16),
            pltpu.VMEM((N_KV, LANES, seq), BF16),
        ],
        compiler_params=_cparams("arbitrary", "arbitrary"),
        name="attn_prompt",
    )(q, cmp_t, kvbt, kvbt, kvbt, kvbt, gates, mt, e_mat)


def _imp_matrix(ncmp, nblk_pad):
    per = SEL_BLOCK // CMP_STRIDE
    pad = CMP_BLOCK // CMP_STRIDE - 1
    n = np.arange(ncmp)[:, None]
    j = np.arange(nblk_pad)[None, :]
    return ((n >= per * j - pad) & (n <= per * j + per - 1)).astype(np.float32)


def _attn_sample_cmp_kernel(q_ref, kc_ref, vc_ref, m_ref, oc_ref, idx_ref, *, n_valid_cmp, qpos0, nblk):
    ncmp = kc_ref.shape[2]
    rows = q_ref.shape[2]
    t_row = lax.broadcasted_iota(jnp.int32, (rows, 1), 0) % 8
    c_idx = lax.broadcasted_iota(jnp.int32, (1, ncmp), 1)
    mask_c = (c_idx * CMP_STRIDE + (CMP_BLOCK - 1) <= qpos0 + t_row) & (c_idx < n_valid_cmp)
    p_tok = []
    for g in range(N_KV):
        gs = slice(HEAD_DIM * g, HEAD_DIM * (g + 1))
        s = jnp.where(mask_c, _dot_nt(q_ref[0, g], kc_ref[0, 0, :, gs]), -jnp.inf)
        m = jnp.max(s, axis=-1, keepdims=True)
        m = jnp.where(m == -jnp.inf, 0.0, m)
        e = jnp.exp(s - m)
        d = jnp.sum(e, axis=-1, keepdims=True)
        p = e / jnp.where(d > 0, d, 1.0)
        oc_ref[0, g] = _dot(p.astype(BF16), vc_ref[0, 0, :, gs])
        p_tok.append(jnp.sum(p.reshape(GROUP_R, 8, ncmp), axis=0))
    p_all = jnp.concatenate(p_tok, axis=0)
    p_hi = p_all.astype(BF16)
    p_lo = (p_all - p_hi.astype(F32)).astype(BF16)
    imp = _dot(p_hi, m_ref[...]) + _dot(p_lo, m_ref[...])
    nb_pad = imp.shape[1]
    jb = lax.broadcasted_iota(jnp.int32, (1, nb_pad), 1)
    qpos = qpos0 + lax.broadcasted_iota(jnp.int32, (2 * 8, 1), 0) % 8
    cur = qpos // SEL_BLOCK
    valid = (jb * SEL_BLOCK <= qpos) & (jb < nblk)
    forced = (jb == 0) | (jb == cur) | (jb == cur - 1)
    score = jnp.where(valid, jnp.where(forced, jnp.inf, imp), -jnp.inf)
    avail = jb < nblk
    lane = lax.broadcasted_iota(jnp.int32, (2 * 8, LANES), 1)
    picked = jnp.zeros((2 * 8, LANES), jnp.int32)
    for k in range(N_SEL):
        best = jnp.max(jnp.where(avail, score, -jnp.inf), axis=-1, keepdims=True)
        cand = avail & (score == best)
        idx = jnp.min(jnp.where(cand, jb, nb_pad), axis=-1, keepdims=True)
        picked = jnp.where(lane == k, idx, picked)
        avail = avail & (jb != idx)
    idx_ref[0] = picked


def _attn_sample_cmp(qg, cmp_kv, m_mat, n_valid_cmp, qpos0, nblk):
    nbatch = qg.shape[0]
    ncmp = cmp_kv.shape[2]
    rows = qg.shape[2]
    cmp_spec = lambda slot: pl.BlockSpec((1, 1, ncmp, LANES), lambda b: (slot, b, 0, 0))
    return pl.pallas_call(
        functools.partial(_attn_sample_cmp_kernel, n_valid_cmp=n_valid_cmp, qpos0=qpos0, nblk=nblk),
        out_shape=(
            jax.ShapeDtypeStruct((nbatch, N_KV, rows, HEAD_DIM), F32),
            jax.ShapeDtypeStruct((nbatch, 2 * 8, LANES), jnp.int32),
        ),
        grid=(nbatch,),
        in_specs=[
            pl.BlockSpec((1, N_KV, rows, HEAD_DIM), lambda b: (b, 0, 0, 0)),
            cmp_spec(0), cmp_spec(1),
            pl.BlockSpec(m_mat.shape, lambda b: (0, 0)),
        ],
        out_specs=(
            pl.BlockSpec((1, N_KV, rows, HEAD_DIM), lambda b: (b, 0, 0, 0)),
            pl.BlockSpec((1, 2 * 8, LANES), lambda b: (b, 0, 0)),
        ),
        compiler_params=_cparams("parallel"),
        name="attn_sample_cmp",
    )(qg, cmp_kv, cmp_kv, m_mat)


def _masked_softmax(s, mask):
    s = jnp.where(mask, s, -jnp.inf)
    m = jnp.max(s, axis=-1, keepdims=True)
    m = jnp.where(m == -jnp.inf, 0.0, m)
    e = jnp.exp(s - m)
    d = jnp.sum(e, axis=-1, keepdims=True)
    return e / jnp.where(d > 0, d, 1.0)


def _attn_sample_sel_kernel(idx_ref, pt_ref, *refs, dec_seq, qpos0, n_cache_blk, wb):
    del pt_ref
    nsel = N_KV * N_SEL
    per_page = PAGE_SIZE // SEL_BLOCK
    k_refs = refs[:nsel]
    v_refs = refs[nsel:2 * nsel]
    q_ref, tail_ref, kw_ref, vw_ref, wnew_ref, oc_ref, gate_ref, o_ref, osel_ref = refs[2 * nsel:]
    b = pl.program_id(0)
    t = pl.program_id(1)
    rows = q_ref.shape[2]
    t_row = lax.broadcasted_iota(jnp.int32, (rows, 1), 0) % 8
    qpos = qpos0 + t_row
    colk = lax.broadcasted_iota(jnp.int32, (1, N_SEL * PAGE_SIZE), 1)
    slot_of_col = colk // PAGE_SIZE
    row_in_page = colk % PAGE_SIZE

    @pl.when(t == 0)
    def _():
        osel_ref[...] = jnp.zeros(osel_ref.shape, F32)

    for g in range(N_KV):
        blk_of_col = jnp.zeros_like(colk)
        k_pages, v_pages = [], []
        for k in range(N_SEL):
            blk = idx_ref[((b * dec_seq + t) * N_KV + g) * N_SEL + k]
            is_new = blk >= n_cache_blk
            k_pages.append(jnp.where(is_new, tail_ref[0, 0, g], k_refs[g * N_SEL + k][0, 0, 0]))
            v_pages.append(jnp.where(is_new, tail_ref[1, 0, g], v_refs[g * N_SEL + k][0, 0, 0]))
            blk_of_col = blk_of_col + jnp.where(slot_of_col == k, blk, 0)
        in_block = row_in_page // SEL_BLOCK == blk_of_col % per_page
        kpos = blk_of_col * SEL_BLOCK + row_in_page % SEL_BLOCK
        k_t = jnp.concatenate(k_pages, axis=1).astype(BF16)
        v_t = jnp.concatenate(v_pages, axis=1).astype(BF16)
        p = _masked_softmax(_dot(q_ref[0, g], k_t), in_block & (kpos <= qpos))
        o = _dot_nt(p.astype(BF16), v_t)
        osel_ref[g] = osel_ref[g] + jnp.where(t_row == t, o, 0.0)

    @pl.when(t == dec_seq - 1)
    def _():
        colw = lax.broadcasted_iota(jnp.int32, (1, wb + LANES), 1)
        kwpos = jnp.where(colw < wb, qpos0 - wb + colw, qpos0 + colw - wb)
        dpos = qpos - kwpos
        mask_w = (dpos >= 0) & (dpos < WINDOW) & (kwpos >= 0) & (colw < wb + dec_seq)
        for g in range(N_KV):
            k_t = jnp.concatenate([kw_ref[0, 0, g], wnew_ref[0, 0, g]], axis=1).astype(BF16)
            v_t = jnp.concatenate([vw_ref[0, 0, g], wnew_ref[1, 0, g]], axis=1).astype(BF16)
            p = _masked_softmax(_dot(q_ref[0, g], k_t), mask_w)
            o_win = _dot_nt(p.astype(BF16), v_t)
            gt = gate_ref[0, g]
            o_ref[0, g] = oc_ref[0, g] * gt[:, 0:1] + osel_ref[g] * gt[:, 1:2] + o_win * gt[:, 2:3]


def _attn_sample_sel(idx_flat, pt_flat, cache_t, qg, tail_t, win_t, wnew_t, o_cmp, gates_g, dec_seq, n_pages,
                     qpos0):
    nbatch = qg.shape[0]
    rows = qg.shape[2]
    wb = win_t.shape[4]
    per_page = PAGE_SIZE // SEL_BLOCK
    n_cache_blk = n_pages * per_page

    def page_spec(slot, g, k):
        def imap(b, t, idx, pt):
            blk = jnp.minimum(idx[((b * dec_seq + t) * N_KV + g) * N_SEL + k], n_cache_blk - 1)
            return (slot, pt[b * n_pages + blk // per_page], g, 0, 0)
        return pl.BlockSpec((1, 1, 1, HEAD_DIM, PAGE_SIZE), imap)

    k_specs = [page_spec(2, g, k) for g in range(N_KV) for k in range(N_SEL)]
    v_specs = [page_spec(3, g, k) for g in range(N_KV) for k in range(N_SEL)]
    per_b = lambda shape: pl.BlockSpec((1,) + shape, lambda b, t, idx, pt: (b,) + (0,) * len(shape))
    new_spec = pl.BlockSpec((2, 1, N_KV, HEAD_DIM, LANES), lambda b, t, idx, pt: (0, b, 0, 0, 0))
    win_spec = lambda s: pl.BlockSpec((1, 1, N_KV, HEAD_DIM, wb), lambda b, t, idx, pt: (s, b, 0, 0, 0))
    grid_spec = pltpu.PrefetchScalarGridSpec(
        num_scalar_prefetch=2,
        grid=(nbatch, dec_seq),
        in_specs=k_specs + v_specs + [
            per_b((N_KV, rows, HEAD_DIM)),
            new_spec, win_spec(0), win_spec(1), new_spec,
            per_b((N_KV, rows, HEAD_DIM)),
            per_b((N_KV, rows, LANES)),
        ],
        out_specs=per_b((N_KV, rows, HEAD_DIM)),
        scratch_shapes=[pltpu.VMEM((N_KV, rows, HEAD_DIM), F32)],
    )
    n_page_specs = 2 * N_KV * N_SEL
    return pl.pallas_call(
        functools.partial(_attn_sample_sel_kernel, dec_seq=dec_seq, qpos0=qpos0, n_cache_blk=n_cache_blk, wb=wb),
        out_shape=jax.ShapeDtypeStruct((nbatch, N_KV, rows, HEAD_DIM), F32),
        grid_spec=grid_spec,
        compiler_params=_cparams("parallel", "arbitrary"),
        name="attn_sample_sel",
    )(idx_flat, pt_flat, *([cache_t] * n_page_specs), qg, tail_t, win_t, win_t, wnew_t, o_cmp, gates_g)


FFN_CHUNK = 256


def _post_kernel(*refs, n_a, final):
    y_ref = refs[0]
    a_refs = refs[1:1 + 2 * n_a]
    gmix_ref, g2_ref, sh_ref, sc_ref, gffn_ref, w1_ref, w2_ref = refs[1 + 2 * n_a:8 + 2 * n_a]
    rest = refs[8 + 2 * n_a:]
    out_ref = rest[-1]
    mix = _dot(a_refs[0][...], a_refs[1][...])
    for k in range(1, n_a):
        mix = mix + _dot(a_refs[2 * k][...], a_refs[2 * k + 1][...])
    y1 = y_ref[...] + gmix_ref[0] * mix
    h = _modulate(y1, g2_ref[...], sh_ref[0], sc_ref[0]).astype(BF16)
    d_ff = w2_ref.shape[0]
    acc = jnp.zeros(y1.shape, F32)
    for c in range(d_ff // FFN_CHUNK):
        c0 = c * FFN_CHUNK
        gate = _dot(h, w1_ref[:, c0:c0 + FFN_CHUNK])
        up = _dot(h, w1_ref[:, d_ff + c0:d_ff + c0 + FFN_CHUNK])
        acc = acc + _dot((_silu(gate) * up).astype(BF16), w2_ref[c0:c0 + FFN_CHUNK, :])
    y2 = y1 + gffn_ref[0] * acc
    if final:
        fg_ref = rest[0]
        ms = jnp.mean(y2 * y2, axis=-1, keepdims=True)
        y2 = (y2 * lax.rsqrt(ms + NORM_EPS)) * fg_ref[...]
    out_ref[...] = y2


def _post(y, a_list, wo_list, gmix, g2, shift, scale, gffn, w1, w2, final_g, tm, rows_per_mod):
    n, d = y.shape
    r = shift.shape[1]
    mod_spec = pl.BlockSpec((1, r, d), lambda i: (i // rows_per_mod, 0, 0))
    const = lambda arr: pl.BlockSpec(arr.shape, lambda i: (0,) * arr.ndim, pipeline_mode=pl.Buffered(1))
    in_specs = [pl.BlockSpec((tm, d), lambda i: (i, 0))]
    args = [y]
    for a, wo in zip(a_list, wo_list):
        in_specs += [pl.BlockSpec((tm, a.shape[1]), lambda i: (i, 0)), const(wo)]
        args += [a, wo]
    in_specs += [mod_spec, pl.BlockSpec((1, d), lambda i: (0, 0)), mod_spec, mod_spec, mod_spec, const(w1), const(w2)]
    args += [gmix, g2, shift, scale, gffn, w1, w2]
    if final_g is not None:
        in_specs.append(pl.BlockSpec((1, d), lambda i: (0, 0)))
        args.append(final_g)
    return pl.pallas_call(
        functools.partial(_post_kernel, n_a=len(a_list), final=final_g is not None),
        out_shape=jax.ShapeDtypeStruct((n, d), F32),
        grid=(n // tm,),
        in_specs=in_specs,
        out_specs=pl.BlockSpec((tm, d), lambda i: (i, 0)),
        compiler_params=_cparams("parallel"),
        name="post_proj_ffn",
    )(*args)


def _pre1_kernel(x_ref, g_ref, sh_ref, sc_ref, w_ref, u_ref):
    h = _modulate(x_ref[...], g_ref[...], sh_ref[0], sc_ref[0]).astype(BF16)
    d = u_ref.shape[1]
    a = _dot(h, w_ref[:, 0:d])
    b = _dot(h, w_ref[:, d:2 * d])
    u_ref[...] = a * _sigmoid(b)


def _pre1(x2, g, shift, scale, pw1, tm, rows_per_mod):
    n, d = x2.shape
    r = shift.shape[1]
    mod_spec = pl.BlockSpec((1, r, d), lambda i: (i // rows_per_mod, 0, 0))
    return pl.pallas_call(
        _pre1_kernel,
        out_shape=jax.ShapeDtypeStruct((n, d), F32),
        grid=(n // tm,),
        in_specs=[
            pl.BlockSpec((tm, d), lambda i: (i, 0)),
            pl.BlockSpec((1, d), lambda i: (0, 0)),
            mod_spec, mod_spec,
            pl.BlockSpec(pw1.shape, lambda i: (0, 0)),
        ],
        out_specs=pl.BlockSpec((tm, d), lambda i: (i, 0)),
        compiler_params=_cparams("parallel"),
        name="pre1_pw_glu",
    )(x2, g, shift, scale, pw1)


def _conv_body(ext_ref, t_rows, dw_ref, dwb_ref, lng_ref, lnb_ref, out_ref):
    off = CONV_PAD - CONV_HIST
    acc = None
    for b in range(8):
        span = t_rows + (8 if b else 0)
        part = None
        for k in range(CONV_WIDTH):
            if (off + k) % 8 != b:
                continue
            base = off + k - b
            term = ext_ref[base:base + span, :] * dw_ref[k:k + 1, :]
            part = term if part is None else part + term
        part = part[b:b + t_rows, :]
        acc = part if acc is None else acc + part
    y = acc + dwb_ref[...]
    mu = jnp.mean(y, axis=-1, keepdims=True)
    var = jnp.mean(jnp.square(y - mu), axis=-1, keepdims=True)
    z = (y - mu) * lax.rsqrt(var + NORM_EPS) * lng_ref[...] + lnb_ref[...]
    out_ref[...] = _silu(z).astype(out_ref.dtype)


def _conv_prompt_kernel(x_ref, prev_ref, dw_ref, dwb_ref, lng_ref, lnb_ref, out_ref, ext_ref, *,
                        tiles_per_batch, tm):
    t_in_b = pl.program_id(0) % tiles_per_batch
    ext_ref[0:CONV_PAD, :] = jnp.where(t_in_b > 0, prev_ref[...], 0.0)
    ext_ref[CONV_PAD:, :] = x_ref[...]
    _conv_body(ext_ref, tm, dw_ref, dwb_ref, lng_ref, lnb_ref, out_ref)


def _conv_prompt(u, dw, dwb, lng, lnb, seq, tm):
    n, d = u.shape
    tpb = seq // tm
    hb = tm // CONV_PAD
    vec = pl.BlockSpec((1, d), lambda i: (0, 0))
    return pl.pallas_call(
        functools.partial(_conv_prompt_kernel, tiles_per_batch=tpb, tm=tm),
        out_shape=jax.ShapeDtypeStruct((n, d), BF16),
        grid=(n // tm,),
        in_specs=[
            pl.BlockSpec((tm, d), lambda i: (i, 0)),
            pl.BlockSpec((CONV_PAD, d), lambda i: (jnp.maximum(i * hb - 1, 0), 0)),
            pl.BlockSpec(dw.shape, lambda i: (0, 0)),
            vec, vec, vec,
        ],
        out_specs=pl.BlockSpec((tm, d), lambda i: (i, 0)),
        scratch_shapes=[pltpu.VMEM((CONV_PAD + tm, d), F32)],
        compiler_params=_cparams("parallel"),
        name="conv_prompt",
    )(u, u, dw, dwb, lng, lnb)


def _conv_sample_kernel(ext_ref, dw_ref, dwb_ref, lng_ref, lnb_ref, out_ref, *, t_rows):
    _conv_body(ext_ref.at[0], t_rows, dw_ref, dwb_ref, lng_ref, lnb_ref, out_ref.at[0])


def _conv_sample(ext, dw, dwb, lng, lnb):
    b, rows, d = ext.shape
    t_rows = rows - CONV_PAD
    vec = pl.BlockSpec((1, d), lambda i: (0, 0))
    return pl.pallas_call(
        functools.partial(_conv_sample_kernel, t_rows=t_rows),
        out_shape=jax.ShapeDtypeStruct((b, t_rows, d), BF16),
        grid=(b,),
        in_specs=[
            pl.BlockSpec((1, rows, d), lambda i: (i, 0, 0)),
            pl.BlockSpec(dw.shape, lambda i: (0, 0)),
            vec, vec, vec,
        ],
        out_specs=pl.BlockSpec((1, t_rows, d), lambda i: (i, 0, 0)),
        compiler_params=_cparams("parallel"),
        name="conv_sample",
    )(ext, dw, dwb, lng, lnb)


ROW_TILE = 512
ATTN_TILE = 512


def _group_rows(x, nbatch, dec_seq):
    w = x.shape[1] // N_HEADS
    x = x.reshape(nbatch, dec_seq, N_KV, GROUP_R, w).transpose(0, 2, 3, 1, 4)
    x = jnp.pad(x, ((0, 0), (0, 0), (0, 0), (0, 8 - dec_seq), (0, 0)))
    return x.reshape(nbatch, N_KV, GROUP_R * 8, w)


def _ungroup_rows(x, nbatch, dec_seq):
    w = x.shape[3]
    x = x.reshape(nbatch, N_KV, GROUP_R, 8, w)[:, :, :, :dec_seq]
    return x.transpose(0, 3, 1, 2, 4).reshape(nbatch * dec_seq, N_HEADS * w)


def kernel(x_prompt, x_sample, cache_kv, page_table, cache_win, state_pool, state_conv, c_prompt, c_sample,
           ada_w, ada_b, norm_g, attn_w_in, attn_w_out, pool_w, pool_scale, cmp_pos, cmp_w1, cmp_w2,
           conv_pw1, conv_dw, conv_dw_b, conv_ln_g, conv_ln_b, conv_pw2, ffn_w1, ffn_w2, final_g):
    nb_p, seq, d = x_prompt.shape
    nb_s, dec_seq, _ = x_sample.shape
    n_p, n_s = nb_p * seq, nb_s * dec_seq
    n_pages = page_table.shape[1]
    past = n_pages * PAGE_SIZE
    wb = cache_win.shape[3]
    tm = min(ROW_TILE, seq)
    tq = min(ATTN_TILE, seq)
    pool_width = pool_w.shape[1] * pool_w.shape[2]
    att_width = N_HEADS * HEAD_DIM

    mods = _ada(jnp.concatenate([c_prompt, c_sample], axis=0), ada_w, ada_b)

    def mod_p(layer, k):
        return mods[layer, :nb_p, k * d:(k + 1) * d].reshape(nb_p, 1, d)

    def mod_s(layer, k):
        return jnp.repeat(mods[layer, nb_p:, k * d:(k + 1) * d], dec_seq, axis=0).reshape(1, n_s, d)

    y_p = x_prompt.reshape(n_p, d)
    y_s = x_sample.reshape(n_s, d)
    row = lambda v: v.reshape(1, -1)

    w_in = attn_w_in[0]
    w_in_pad = jnp.pad(w_in.T, ((0, (-w_in.shape[1]) % LANES), (0, 0))).astype(BF16)
    cos_p, sin_p = _rope_tables(np.arange(seq))
    cos_s, sin_s = _rope_tables(past + np.arange(n_s) % dec_seq)
    g0 = row(norm_g[0, 0])
    u_p, q_p, gate_p, kvt_p, wint_p, kvbt_p, cmp_in_p = _pre0(
        y_p, g0, mod_p(0, 0), mod_p(0, 1), w_in_pad, cos_p, sin_p, tm, seq // tm, seq // tm, seq=seq)
    u_s, q_s, gate_s, kv_s, win_s = _pre0(y_s, g0, mod_s(0, 0), mod_s(0, 1), w_in_pad, cos_s, sin_s, n_s, 1, 1)

    pool_w_b = pool_w[0].astype(BF16)
    pool_sc = row(pool_scale[0])
    ypool_p = _pool_prompt(u_p, pool_w_b, pool_sc, seq, tm)
    u_s3 = u_s.reshape(nb_s, dec_seq, pool_width)
    pool_ext = jnp.concatenate([
        jnp.zeros((nb_s, POOL_PAD - POOL_HIST, pool_width), F32), state_pool[0], u_s3,
        jnp.zeros((nb_s, 8 - dec_seq, pool_width), F32)], axis=1)
    ypool_s = _pool_sample(pool_ext, pool_w_b, pool_sc, past)[:, :dec_seq].reshape(n_s, pool_width)

    pos_ab, w1_ab, w2_bd = _cmp_weights(cmp_pos[0], cmp_w1[0], cmp_w2[0])
    cmp_in_p4 = cmp_in_p.reshape(2, nb_p, seq, LANES)
    _, cmp_t_p = _cmp_combine(_cmp_rows(cmp_in_p4, pos_ab, w1_ab, nb_p, seq), w2_bd)
    cache_t = cache_kv[0].transpose(0, 1, 3, 4, 2)
    pt_flat = page_table.reshape(-1)
    part_past = _cmp_pages(cache_t, pt_flat, n_pages, pos_ab, w1_ab, nb_s)
    tail_rows = 2 * PAGE_SIZE
    kv_s4 = kv_s.reshape(4, nb_s, dec_seq, LANES)
    tail4 = jnp.pad(kv_s4, ((0, 0), (0, 0), (0, tail_rows - dec_seq), (0, 0)))
    part_tail = _cmp_rows(tail4, pos_ab, w1_ab, nb_s, tail_rows)
    cmp_s, _ = _cmp_combine(jnp.concatenate([part_past, part_tail], axis=2), w2_bd)
    total_len = past + dec_seq
    padded_len = -(-total_len // SEL_BLOCK) * SEL_BLOCK
    n_cmp_s = padded_len // CMP_STRIDE - CMP_BLOCK // CMP_STRIDE + 1
    nblk_s = padded_len // SEL_BLOCK

    assert seq // SEL_BLOCK <= LANES and dec_seq <= SEL_BLOCK
    ncmp_p = cmp_t_p.shape[3]
    mt_p = jnp.asarray(_imp_matrix(ncmp_p, LANES).T, BF16)
    e_mat = jnp.asarray((np.arange(seq)[None, :] // SEL_BLOCK == np.arange(LANES)[:, None]).astype(np.float32),
                        BF16)
    o_p = _attn_prompt(q_p, cmp_t_p, kvbt_p, gate_p, mt_p, e_mat, nb_p, seq, tq)

    qg_s = _group_rows(q_s, nb_s, dec_seq)
    nblk_pad = -(-nblk_s // LANES) * LANES
    m_s = jnp.asarray(_imp_matrix(cmp_s.shape[2], nblk_pad), BF16)
    o_cmp_s, picked = _attn_sample_cmp(qg_s, cmp_s, m_s, n_cmp_s, past, nblk_s)
    idx = picked.reshape(nb_s, N_KV, 8, LANES)[:, :, :dec_seq, :N_SEL].transpose(0, 2, 1, 3).reshape(-1)

    def new_rows_t(rows):
        x = rows.reshape(rows.shape[0], nb_s, dec_seq, N_KV, HEAD_DIM).transpose(0, 1, 3, 4, 2)
        return jnp.pad(x, ((0, 0), (0, 0), (0, 0), (0, 0), (0, LANES - dec_seq)))

    tail_t = new_rows_t(kv_s[2:4])
    wnew_t = new_rows_t(win_s)
    win_t = cache_win[0].transpose(0, 1, 3, 4, 2)
    gates_g = jnp.pad(_group_rows(gate_s[:, :3 * N_HEADS], nb_s, dec_seq), ((0, 0), (0, 0), (0, 0), (0, LANES - 3)))
    o_s = _attn_sample_sel(idx, pt_flat, cache_t, qg_s, tail_t, win_t, wnew_t, o_cmp_s, gates_g,
                           dec_seq, n_pages, past)
    o_s = _ungroup_rows(o_s, nb_s, dec_seq).astype(BF16)

    w_out = attn_w_out[0].astype(BF16)
    wo_list = [w_out[:pool_width], w_out[pool_width:]]
    g1 = row(norm_g[0, 1])
    ffn1_0, ffn2_0 = ffn_w1[0].astype(BF16), ffn_w2[0].astype(BF16)
    y_p = _post(y_p, [ypool_p, o_p], wo_list, mod_p(0, 2), g1, mod_p(0, 3), mod_p(0, 4), mod_p(0, 5),
                ffn1_0, ffn2_0, None, tm, seq // tm)
    y_s = _post(y_s, [ypool_s, o_s], wo_list, mod_s(0, 2), g1, mod_s(0, 3), mod_s(0, 4), mod_s(0, 5),
                ffn1_0, ffn2_0, None, n_s, 1)

    g0 = row(norm_g[1, 0])
    pw1 = conv_pw1[0].astype(BF16)
    uc_p = _pre1(y_p, g0, mod_p(1, 0), mod_p(1, 1), pw1, tm, seq // tm)
    uc_s = _pre1(y_s, g0, mod_s(1, 0), mod_s(1, 1), pw1, n_s, 1)
    dwb, lng, lnb = row(conv_dw_b[0]), row(conv_ln_g[0]), row(conv_ln_b[0])
    cv_p = _conv_prompt(uc_p, conv_dw[0], dwb, lng, lnb, seq, tm)
    uc_s3 = uc_s.reshape(nb_s, dec_seq, d)
    conv_ext = jnp.concatenate([
        jnp.zeros((nb_s, CONV_PAD - CONV_HIST, d), F32), state_conv[0], uc_s3,
        jnp.zeros((nb_s, 8 - dec_seq, d), F32)], axis=1)
    cv_s = _conv_sample(conv_ext, conv_dw[0], dwb, lng, lnb)[:, :dec_seq].reshape(n_s, d)

    pw2 = conv_pw2[0].astype(BF16)
    g1 = row(norm_g[1, 1])
    ffn1_1, ffn2_1 = ffn_w1[1].astype(BF16), ffn_w2[1].astype(BF16)
    fg = row(final_g)
    y_p = _post(y_p, [cv_p], [pw2], mod_p(1, 2), g1, mod_p(1, 3), mod_p(1, 4), mod_p(1, 5),
                ffn1_1, ffn2_1, fg, tm, seq // tm)
    y_s = _post(y_s, [cv_s], [pw2], mod_s(1, 2), g1, mod_s(1, 3), mod_s(1, 4), mod_s(1, 5),
                ffn1_1, ffn2_1, fg, n_s, 1)

    y_prompt = y_p.reshape(nb_p, seq, d)
    y_sample = y_s.reshape(nb_s, dec_seq, d)
    kv_prompt = kvt_p.transpose(0, 1, 4, 2, 3)[None]
    kv_sample = kv_s.reshape(1, 4, nb_s, dec_seq, N_KV, HEAD_DIM)
    if seq >= wb:
        win_prompt_t = wint_p[..., seq - wb:]
    else:
        win_prompt_t = jnp.pad(wint_p, ((0, 0),) * 4 + ((wb - seq, 0),))
    win_prompt = win_prompt_t.transpose(0, 1, 4, 2, 3)
    win_sample_t = jnp.concatenate([win_t, wnew_t[..., :dec_seq]], axis=-1)[..., -wb:]
    win_sample = win_sample_t.transpose(0, 1, 4, 2, 3)

    def last_rows(x3, hist, state):
        full = x3 if state is None and x3.shape[1] >= hist else jnp.concatenate(
            [jnp.zeros((x3.shape[0], hist, x3.shape[2]), F32) if state is None else state, x3], axis=1)
        return full[:, full.shape[1] - hist:]

    pool_prompt = last_rows(u_p.reshape(nb_p, seq, pool_width), POOL_HIST, None)
    pool_sample = last_rows(u_s3, POOL_HIST, state_pool[0])
    conv_prompt = last_rows(uc_p.reshape(nb_p, seq, d), CONV_HIST, None)
    conv_sample = last_rows(uc_s3, CONV_HIST, state_conv[0])
    return (y_prompt, y_sample, kv_prompt, kv_sample, win_prompt[None], win_sample[None],
            pool_prompt[None], pool_sample[None], conv_prompt[None], conv_sample[None])
```

```python
import functools

import numpy as np
import jax
import jax.numpy as jnp
from jax import lax
from jax.experimental import pallas as pl
from jax.experimental.pallas import tpu as pltpu

F32 = jnp.float32
BF16 = jnp.bfloat16

NORM_EPS = 1e-6
N_HEADS = 8
HEAD_DIM = 64
N_KV = 2
GROUP_R = N_HEADS // N_KV
POOL_WINDOWS = (2, 4, 8, 16)
POOL_HIST = 15
POOL_PAD = 16
CMP_BLOCK = 32
CMP_STRIDE = 16
SEL_BLOCK = 64
N_SEL = 16
WINDOW = 512
PAGE_SIZE = 128
ROPE_THETA = 10000.0
CONV_WIDTH = 31
CONV_HIST = CONV_WIDTH - 1
CONV_PAD = 32
LANES = 128
NEG = -1e9
V7X_VMEM_LIMIT = 56 * 1024 * 1024


def _cparams(*sem):
    return pltpu.CompilerParams(dimension_semantics=sem, vmem_limit_bytes=V7X_VMEM_LIMIT)


def _dot(a, b):
    return jnp.dot(a, b, preferred_element_type=F32)


def _dot_nt(a, b):
    return lax.dot_general(a, b, (((1,), (1,)), ((), ())), preferred_element_type=F32)


def _sigmoid(x):
    return 1.0 / (1.0 + jnp.exp(-x))


def _silu(x):
    return x * _sigmoid(x)


def _modulate(x, g, shift, scale):
    ms = jnp.mean(x * x, axis=-1, keepdims=True)
    y = x * lax.rsqrt(ms + NORM_EPS)
    return (y * g) * (1.0 + scale) + shift


def _ada_kernel(c_ref, w_ref, b_ref, o_ref):
    a = _silu(c_ref[...]).astype(BF16)
    o_ref[0] = _dot(a, w_ref[0].astype(BF16)) + b_ref[0]


def _ada(c_all, ada_w, ada_b):
    depth, d, n6 = ada_w.shape
    bc = c_all.shape[0]
    tn = n6 // 4
    return pl.pallas_call(
        _ada_kernel,
        out_shape=jax.ShapeDtypeStruct((depth, bc, n6), F32),
        grid=(depth, n6 // tn),
        in_specs=[
            pl.BlockSpec((bc, d), lambda l, j: (0, 0)),
            pl.BlockSpec((1, d, tn), lambda l, j: (l, 0, j)),
            pl.BlockSpec((1, 1, tn), lambda l, j: (l, 0, j)),
        ],
        out_specs=pl.BlockSpec((1, bc, tn), lambda l, j: (l, 0, j)),
        compiler_params=_cparams("parallel", "parallel"),
        name="ada_mod",
    )(c_all, ada_w, ada_b.reshape(depth, 1, n6))


def _rope(x, cos, sin_signed):
    lane = lax.broadcasted_iota(jnp.int32, x.shape, 1)
    first = (lane % HEAD_DIM) < (HEAD_DIM // 2)
    swapped = jnp.where(first, pltpu.roll(x, LANES - HEAD_DIM // 2, 1), pltpu.roll(x, HEAD_DIM // 2, 1))
    return x * cos + swapped * sin_signed


def _pre0_project(x_ref, g_ref, sh_ref, sc_ref, w_ref, cos_ref, sin_ref, u_ref, q_ref, gate_ref):
    h = _modulate(x_ref[...], g_ref[...], sh_ref[0], sc_ref[0]).astype(BF16)
    cos = cos_ref[...]
    sin = sin_ref[...]

    def proj(c0, width):
        return _dot_nt(h, w_ref[c0:c0 + width, :])

    u_ref[...] = proj(0, 512)
    for j in range(4):
        qj = _rope(proj(512 + LANES * j, LANES), cos, sin)
        q_ref[:, LANES * j:LANES * (j + 1)] = (qj * (HEAD_DIM ** -0.5)).astype(BF16)
    gate_ref[...] = _sigmoid(proj(1792, LANES))
    kc = _rope(proj(1024, LANES), cos, sin)
    vc = proj(1152, LANES)
    ks = _rope(proj(1280, LANES), cos, sin)
    vs = proj(1408, LANES)
    kw = _rope(proj(1536, LANES), cos, sin)
    vw = proj(1664, LANES)
    return kc, vc, ks, vs, kw, vw


def _pre0_rows_kernel(x_ref, g_ref, sh_ref, sc_ref, w_ref, cos_ref, sin_ref,
                      u_ref, q_ref, gate_ref, kv_ref, win_ref):
    kc, vc, ks, vs, kw, vw = _pre0_project(x_ref, g_ref, sh_ref, sc_ref, w_ref, cos_ref, sin_ref,
                                           u_ref, q_ref, gate_ref)
    for k, seg in enumerate((kc, vc, ks, vs)):
        kv_ref[k] = seg
    win_ref[0] = kw
    win_ref[1] = vw


def _store_transposed(ref, lead, x):
    xt = x.T
    for g in range(N_KV):
        ref[lead + (g,)] = xt[HEAD_DIM * g:HEAD_DIM * (g + 1), :].astype(ref.dtype)


def _pre0_seq_kernel(x_ref, g_ref, sh_ref, sc_ref, w_ref, cos_ref, sin_ref,
                     u_ref, q_ref, gate_ref, kvt_ref, wint_ref, kvbt_ref, cmp_ref):
    kc, vc, ks, vs, kw, vw = _pre0_project(x_ref, g_ref, sh_ref, sc_ref, w_ref, cos_ref, sin_ref,
                                           u_ref, q_ref, gate_ref)
    for k, seg in enumerate((kc, vc, ks, vs)):
        _store_transposed(kvt_ref, (k, 0), seg)
    _store_transposed(wint_ref, (0, 0), kw)
    _store_transposed(wint_ref, (1, 0), vw)
    for k, seg in enumerate((ks, vs, kw, vw)):
        _store_transposed(kvbt_ref, (k, 0), seg)
    cmp_ref[0] = kc
    cmp_ref[1] = vc


def _pre0(x2, g, shift, scale, w_pad, cos, sin, tm, rows_per_mod, pos_tiles, seq=None):
    n, d = x2.shape
    r = shift.shape[1]
    mod_spec = pl.BlockSpec((1, r, d), lambda i: (i // rows_per_mod, 0, 0))
    pos_spec = pl.BlockSpec((tm, LANES), lambda i: (i % pos_tiles, 0))
    row_spec = lambda w: pl.BlockSpec((tm, w), lambda i: (i, 0))
    out_shape = [jax.ShapeDtypeStruct((n, 512), F32), jax.ShapeDtypeStruct((n, 512), BF16),
                 jax.ShapeDtypeStruct((n, LANES), F32)]
    out_specs = [row_spec(512), row_spec(512), row_spec(LANES)]
    if seq is None:
        body = _pre0_rows_kernel
        out_shape += [jax.ShapeDtypeStruct((4, n, LANES), F32), jax.ShapeDtypeStruct((2, n, LANES), F32)]
        out_specs += [pl.BlockSpec((4, tm, LANES), lambda i: (0, i, 0)),
                      pl.BlockSpec((2, tm, LANES), lambda i: (0, i, 0))]
    else:
        body = _pre0_seq_kernel
        nb, tpb = n // seq, seq // tm
        t_spec = lambda k: pl.BlockSpec((k, 1, N_KV, HEAD_DIM, tm), lambda i: (0, i // tpb, 0, 0, i % tpb))
        out_shape += [jax.ShapeDtypeStruct((4, nb, N_KV, HEAD_DIM, seq), F32),
                      jax.ShapeDtypeStruct((2, nb, N_KV, HEAD_DIM, seq), F32),
                      jax.ShapeDtypeStruct((4, nb, N_KV, HEAD_DIM, seq), BF16),
                      jax.ShapeDtypeStruct((2, n, LANES), F32)]
        out_specs += [t_spec(4), t_spec(2), t_spec(4), pl.BlockSpec((2, tm, LANES), lambda i: (0, i, 0))]
    return pl.pallas_call(
        body,
        out_shape=tuple(out_shape),
        grid=(n // tm,),
        in_specs=[
            pl.BlockSpec((tm, d), lambda i: (i, 0)),
            pl.BlockSpec((1, d), lambda i: (0, 0)),
            mod_spec, mod_spec,
            pl.BlockSpec(w_pad.shape, lambda i: (0, 0)),
            pos_spec, pos_spec,
        ],
        out_specs=tuple(out_specs),
        compiler_params=_cparams("parallel"),
        name="pre0_in_proj",
    )(x2, g, shift, scale, w_pad, cos, sin)


def _rope_tables(pos):
    half = HEAD_DIM // 2
    inv = ROPE_THETA ** (-np.arange(half, dtype=np.float64) / half)
    ang = np.asarray(pos, np.float64)[:, None] * inv[None, :]
    cos, sin = np.cos(ang), np.sin(ang)
    cos_t = np.tile(np.concatenate([cos, cos], axis=1), (1, LANES // HEAD_DIM))
    sin_t = np.tile(np.concatenate([-sin, sin], axis=1), (1, LANES // HEAD_DIM))
    return jnp.asarray(cos_t, F32), jnp.asarray(sin_t, F32)


def _pool_body(ext_ref, t_rows, pos0, w_ref, scale_ref, out_ref):
    pos = pos0 + lax.broadcasted_iota(jnp.int32, (t_rows, 1), 0)
    for g, w in enumerate(POOL_WINDOWS):
        cols = slice(LANES * g, LANES * (g + 1))
        x = ext_ref[POOL_PAD:POOL_PAD + t_rows, cols]
        s = x
        for j in range(1, w):
            s = s + ext_ref[POOL_PAD - j:POOL_PAD - j + t_rows, cols]
        cnt = jnp.minimum(pos + 1, w).astype(F32)
        dlt = (s / cnt - x).astype(BF16)
        y = _dot(dlt, w_ref[g]) * scale_ref[:, cols]
        out_ref[:, cols] = y.astype(out_ref.dtype)


def _pool_prompt_kernel(x_ref, prev_ref, w_ref, scale_ref, out_ref, ext_ref, *, tiles_per_batch, tm):
    t_in_b = pl.program_id(0) % tiles_per_batch
    ext_ref[0:POOL_PAD, :] = jnp.where(t_in_b > 0, prev_ref[...], 0.0)
    ext_ref[POOL_PAD:, :] = x_ref[...]
    _pool_body(ext_ref, tm, t_in_b * tm, w_ref, scale_ref, out_ref)


def _pool_prompt(u, w_grp, scale, seq, tm):
    n, c = u.shape
    tpb = seq // tm
    hb = tm // POOL_PAD
    return pl.pallas_call(
        functools.partial(_pool_prompt_kernel, tiles_per_batch=tpb, tm=tm),
        out_shape=jax.ShapeDtypeStruct((n, c), BF16),
        grid=(n // tm,),
        in_specs=[
            pl.BlockSpec((tm, c), lambda i: (i, 0)),
            pl.BlockSpec((POOL_PAD, c), lambda i: (jnp.maximum(i * hb - 1, 0), 0)),
            pl.BlockSpec(w_grp.shape, lambda i: (0, 0, 0)),
            pl.BlockSpec((1, c), lambda i: (0, 0)),
        ],
        out_specs=pl.BlockSpec((tm, c), lambda i: (i, 0)),
        scratch_shapes=[pltpu.VMEM((POOL_PAD + tm, c), F32)],
        compiler_params=_cparams("parallel"),
        name="pool_prompt",
    )(u, u, w_grp, scale)


def _pool_sample_kernel(ext_ref, w_ref, scale_ref, out_ref, *, pos0, t_rows):
    _pool_body(ext_ref.at[0], t_rows, pos0, w_ref, scale_ref, out_ref.at[0])


def _pool_sample(ext, w_grp, scale, pos0):
    b, rows, c = ext.shape
    t_rows = rows - POOL_PAD
    return pl.pallas_call(
        functools.partial(_pool_sample_kernel, pos0=pos0, t_rows=t_rows),
        out_shape=jax.ShapeDtypeStruct((b, t_rows, c), BF16),
        grid=(b,),
        in_specs=[
            pl.BlockSpec((1, rows, c), lambda i: (i, 0, 0)),
            pl.BlockSpec(w_grp.shape, lambda i: (0, 0, 0)),
            pl.BlockSpec((1, c), lambda i: (0, 0)),
        ],
        out_specs=pl.BlockSpec((1, t_rows, c), lambda i: (i, 0, 0)),
        compiler_params=_cparams("parallel"),
        name="pool_sample",
    )(ext, w_grp, scale)


def _chunk_rows(ref2d, rows):
    n = rows // CMP_STRIDE
    return jnp.concatenate([ref2d[pl.ds(r, n, stride=CMP_STRIDE), :] for r in range(CMP_STRIDE)], axis=1)


def _cmp_partial(a, pos_ref, w_ref, p_ref):
    hid2 = w_ref.shape[2] // 2
    p_ref[0, 0, :, 0:hid2] = _dot((a + pos_ref[0, 0:1, :]).astype(BF16), w_ref[0, :, 0:hid2])
    p_ref[0, 0, :, hid2:] = _dot((a + pos_ref[0, 1:2, :]).astype(BF16), w_ref[0, :, hid2:])


def _cmp_rows_kernel(x_ref, pos_ref, w_ref, p_ref, *, rows):
    _cmp_partial(_chunk_rows(x_ref.at[0, 0], rows), pos_ref, w_ref, p_ref)


def _cmp_rows(x4, pos_ab, w1_ab, nbatch, rows):
    nch = rows // CMP_STRIDE
    return pl.pallas_call(
        functools.partial(_cmp_rows_kernel, rows=rows),
        out_shape=jax.ShapeDtypeStruct((2, nbatch, nch, w1_ab.shape[2]), F32),
        grid=(2, nbatch),
        in_specs=[
            pl.BlockSpec((1, 1, rows, LANES), lambda s, b: (s, b, 0, 0)),
            pl.BlockSpec((1, 2, pos_ab.shape[2]), lambda s, b: (s, 0, 0)),
            pl.BlockSpec((1,) + w1_ab.shape[1:], lambda s, b: (s, 0, 0)),
        ],
        out_specs=pl.BlockSpec((1, 1, nch, w1_ab.shape[2]), lambda s, b: (s, b, 0, 0)),
        compiler_params=_cparams("parallel", "parallel"),
        name="cmp_rows",
    )(x4, pos_ab, w1_ab)


PAGES_PER_STEP = 64


def _cmp_pages_kernel(pt_ref, *refs, pps):
    del pt_ref
    page_refs = refs[:pps]
    pos_ref, w_ref, p_ref, rows_ref = refs[pps:]
    for k, r in enumerate(page_refs):
        page_t = jnp.concatenate([r[0, 0, g] for g in range(N_KV)], axis=0)
        rows_ref[k * PAGE_SIZE:(k + 1) * PAGE_SIZE, :] = page_t.T
    _cmp_partial(_chunk_rows(rows_ref, pps * PAGE_SIZE), pos_ref, w_ref, p_ref)


def _cmp_pages(cache_t, page_table_flat, n_pages, pos_ab, w1_ab, nbatch):
    pps = min(PAGES_PER_STEP, n_pages)
    assert n_pages % pps == 0
    steps = n_pages // pps
    nch = pps * PAGE_SIZE // CMP_STRIDE

    def page_spec(k):
        return pl.BlockSpec(
            (1, 1, N_KV, HEAD_DIM, PAGE_SIZE),
            lambda s, b, j, pt: (s, pt[b * n_pages + j * pps + k], 0, 0, 0))

    grid_spec = pltpu.PrefetchScalarGridSpec(
        num_scalar_prefetch=1,
        grid=(2, nbatch, steps),
        in_specs=[page_spec(k) for k in range(pps)] + [
            pl.BlockSpec((1, 2, pos_ab.shape[2]), lambda s, b, j, pt: (s, 0, 0)),
            pl.BlockSpec((1,) + w1_ab.shape[1:], lambda s, b, j, pt: (s, 0, 0)),
        ],
        out_specs=pl.BlockSpec((1, 1, nch, w1_ab.shape[2]), lambda s, b, j, pt: (s, b, j, 0)),
        scratch_shapes=[pltpu.VMEM((pps * PAGE_SIZE, LANES), F32)],
    )
    return pl.pallas_call(
        functools.partial(_cmp_pages_kernel, pps=pps),
        out_shape=jax.ShapeDtypeStruct((2, nbatch, steps * nch, w1_ab.shape[2]), F32),
        grid_spec=grid_spec,
        compiler_params=_cparams("parallel", "parallel", "parallel"),
        name="cmp_pages",
    )(page_table_flat, *([cache_t] * pps), pos_ab, w1_ab)


def _gelu_tanh(x):
    return x * (0.5 * (1.0 + jnp.tanh(np.sqrt(2.0 / np.pi).astype(np.float32) * (x + 0.044715 * (x * x * x)))))


def _cmp_combine_kernel(*refs, n_parts):
    p_refs = refs[:n_parts]
    w2_ref, w2t_ref, o_ref, ot_ref, pa_ref, pb_ref = refs[n_parts:]
    hid2 = pa_ref.shape[1]
    n_p = pa_ref.shape[0]
    r0 = 0
    for p_ref in p_refs:
        rows = p_ref.shape[2]
        pa_ref[r0:r0 + rows, :] = p_ref[0, 0, :, 0:hid2]
        pb_ref[r0:r0 + rows, :] = p_ref[0, 0, :, hid2:]
        r0 += rows
    pb_ref[n_p:n_p + 8, :] = jnp.zeros((8, hid2), F32)
    hsum = pa_ref[...] + pb_ref[1:n_p + 1, :]
    act = _gelu_tanh(hsum).astype(BF16)
    o_ref[0, 0] = _dot(act, w2_ref[0]).astype(o_ref.dtype)
    ot_ref[0, 0] = _dot_nt(w2t_ref[0], act).astype(ot_ref.dtype)


def _cmp_combine(parts, w2_bd):
    nbatch, width = parts[0].shape[1], parts[0].shape[3]
    n_p = sum(p.shape[2] for p in parts)
    w2t_bd = w2_bd.transpose(0, 2, 1)
    return pl.pallas_call(
        functools.partial(_cmp_combine_kernel, n_parts=len(parts)),
        out_shape=(jax.ShapeDtypeStruct((2, nbatch, n_p, LANES), BF16),
                   jax.ShapeDtypeStruct((2, nbatch, LANES, n_p), BF16)),
        grid=(2, nbatch),
        in_specs=[pl.BlockSpec((1, 1, p.shape[2], width), lambda s, b: (s, b, 0, 0)) for p in parts] + [
            pl.BlockSpec((1,) + w2_bd.shape[1:], lambda s, b: (s, 0, 0)),
            pl.BlockSpec((1,) + w2t_bd.shape[1:], lambda s, b: (s, 0, 0)),
        ],
        out_specs=(pl.BlockSpec((1, 1, n_p, LANES), lambda s, b: (s, b, 0, 0)),
                   pl.BlockSpec((1, 1, LANES, n_p), lambda s, b: (s, b, 0, 0))),
        scratch_shapes=[pltpu.VMEM((n_p, width // 2), F32), pltpu.VMEM((n_p + 8, width // 2), F32)],
        compiler_params=_cparams("parallel", "parallel"),
        name="cmp_combine",
    )(*parts, w2_bd, w2t_bd)


def _cmp_weights(cmp_pos, cmp_w1, cmp_w2):
    hid = cmp_w1.shape[2]
    half = CMP_STRIDE * HEAD_DIM
    eye = jnp.eye(N_KV, dtype=F32)
    pos_ab = jnp.tile(cmp_pos.reshape(2, 2, CMP_STRIDE, 1, HEAD_DIM), (1, 1, 1, N_KV, 1))
    pos_ab = pos_ab.reshape(2, 2, CMP_STRIDE * N_KV * HEAD_DIM)
    w1 = cmp_w1.reshape(2, 2, CMP_STRIDE, HEAD_DIM, hid)
    w1_bd = jnp.einsum("shrdj,gk->shrgdkj", w1, eye)
    w1_bd = w1_bd.reshape(2, 2, CMP_STRIDE * N_KV * HEAD_DIM, N_KV * hid)
    w1_ab = jnp.concatenate([w1_bd[:, 0], w1_bd[:, 1]], axis=2).astype(BF16)
    w2_bd = jnp.einsum("sjd,gk->sgjkd", cmp_w2, eye).reshape(2, N_KV * hid, N_KV * HEAD_DIM).astype(BF16)
    del half
    return pos_ab, w1_ab, w2_bd


def _online_update(s, v_aug_t, state):
    tq, ck = s.shape
    m_new = jnp.broadcast_to(jnp.max(s, axis=-1, keepdims=True), (tq, LANES))
    if state is not None:
        acc_old, m_old = state
        m_new = jnp.maximum(m_old, m_new)
    p = jnp.exp(s - pltpu.repeat(m_new, ck // LANES, axis=1)).astype(BF16)
    acc = _dot_nt(p, v_aug_t)
    if state is not None:
        acc = jnp.exp(m_old - m_new) * acc_old + acc
    return acc, m_new


def _attn_prompt_kernel(q_ref, cmpt_ref, kst_ref, vst_ref, kwt_ref, vwt_ref, gate_ref, mt_ref, e_ref,
                        o_ref, kaug_s, vaug_s, vwaug_s, *, tq, nblk):
    i = pl.program_id(1)
    s0 = i * tq
    seq = kst_ref.shape[4]
    ncmp = cmpt_ref.shape[3]

    @pl.when(i == 0)
    def _():
        ones_row = (lax.broadcasted_iota(jnp.int32, (HEAD_DIM, seq), 0) == 0).astype(BF16)
        for g in range(N_KV):
            kaug_s[g, 0:LANES, :] = e_ref[...]
            kaug_s[g, LANES:LANES + HEAD_DIM, :] = kst_ref[0, 0, g]
            vaug_s[g, 0:HEAD_DIM, :] = vst_ref[0, 0, g]
            vaug_s[g, HEAD_DIM:, :] = ones_row
            vwaug_s[g, 0:HEAD_DIM, :] = vwt_ref[0, 0, g]
            vwaug_s[g, HEAD_DIM:, :] = ones_row

    row = lax.broadcasted_iota(jnp.int32, (tq, 1), 0)
    col = lax.broadcasted_iota(jnp.int32, (1, tq), 1)
    qpos = s0 + row
    gates = gate_ref[...]
    c_end = lax.broadcasted_iota(jnp.int32, (1, ncmp), 1) * CMP_STRIDE + (CMP_BLOCK - 1)
    mask_c = c_end <= qpos
    qp_l = s0 + col
    cur = qp_l // SEL_BLOCK
    n_rb = -(-nblk // 8)
    qs = [q_ref[:, HEAD_DIM * hh:HEAD_DIM * (hh + 1)] for hh in range(N_HEADS)]

    o_cmp, q_aug = [], []
    for g in range(N_KV):
        kct_g = cmpt_ref[0, 0, HEAD_DIM * g:HEAD_DIM * (g + 1), :]
        vct_g = cmpt_ref[1, 0, HEAD_DIM * g:HEAD_DIM * (g + 1), :]
        p_sum = jnp.zeros((tq, ncmp), F32)
        for qh in qs[GROUP_R * g:GROUP_R * (g + 1)]:
            s = jnp.where(mask_c, _dot(qh, kct_g), -jnp.inf)
            m = jnp.max(s, axis=-1, keepdims=True)
            m = jnp.where(m == -jnp.inf, 0.0, m)
            e = jnp.exp(s - m)
            d = jnp.sum(e, axis=-1, keepdims=True)
            p = e / jnp.where(d > 0, d, 1.0)
            p_sum = p_sum + p
            o_cmp.append(_dot_nt(p.astype(BF16), vct_g))

        p_hi = p_sum.astype(BF16)
        p_lo = (p_sum - p_hi.astype(F32)).astype(BF16)
        imp = _dot_nt(mt_ref[...], p_hi) + _dot_nt(mt_ref[...], p_lo)
        score, valid = [], []
        for r in range(n_rb):
            jb = 8 * r + lax.broadcasted_iota(jnp.int32, (8, 1), 0)
            ok = (jb * SEL_BLOCK <= qp_l) & (jb < nblk)
            forced = (jb == 0) | (jb == cur) | (jb == cur - 1)
            valid.append(ok)
            score.append(jnp.where(ok, jnp.where(forced, jnp.inf, imp[8 * r:8 * r + 8, :]), -jnp.inf))
        cnt = [jnp.zeros((8, tq), jnp.int32) for _ in range(n_rb)]
        for j in range(nblk):
            rj = jnp.broadcast_to(score[j // 8][j % 8:j % 8 + 1, :], (8, tq))
            for r in range(n_rb):
                if 8 * r > j:
                    beats = rj >= score[r]
                elif 8 * r + 7 < j:
                    beats = rj > score[r]
                else:
                    later = 8 * r + lax.broadcasted_iota(jnp.int32, (8, 1), 0) > j
                    beats = (rj > score[r]) | ((rj == score[r]) & later)
                cnt[r] = cnt[r] + jnp.where(beats, 1, 0)
        selneg_t = [jnp.where((cnt[r] < N_SEL) & valid[r], 0.0, NEG) for r in range(n_rb)]
        selneg_t.append(jnp.zeros((LANES - 8 * n_rb, tq), F32))
        selneg = jnp.concatenate(selneg_t, axis=0).T.astype(BF16)
        q_aug += [jnp.concatenate([selneg, qh], axis=1) for qh in qs[GROUP_R * g:GROUP_R * (g + 1)]]

    ck = 2 * tq

    def sel_chunk(c0, state, keep):
        out = []
        for g in range(N_KV):
            k_aug = kaug_s[g, :, pl.ds(c0, ck)]
            v_aug = vaug_s[g, :, pl.ds(c0, ck)]
            for hh in range(GROUP_R * g, GROUP_R * (g + 1)):
                s = _dot(q_aug[hh], k_aug)
                if keep is not None:
                    s = jnp.where(keep, s, NEG)
                out.append(_online_update(s, v_aug, None if state is None else state[hh]))
        return tuple(out)

    n_full = i // 2
    last0 = pl.multiple_of(jnp.maximum(i - 1, 0) * tq, tq)
    kpos = last0 + lax.broadcasted_iota(jnp.int32, (1, ck), 1)
    state = sel_chunk(last0, None, (kpos <= qpos) & (kpos >= n_full * ck))
    state = lax.fori_loop(0, n_full, lambda c, st: sel_chunk(pl.multiple_of(c * ck, ck), st, None), state)
    o_sel = [acc[:, 0:HEAD_DIM] / acc[:, HEAD_DIM:HEAD_DIM + 1] for acc, _ in state]

    keep_w = (kpos <= qpos) & (kpos > qpos - WINDOW)
    wstate = []
    for g in range(N_KV):
        k_t = kwt_ref[0, 0, g, :, pl.ds(last0, ck)]
        v_aug = vwaug_s[g, :, pl.ds(last0, ck)]
        for hh in range(GROUP_R * g, GROUP_R * (g + 1)):
            wstate.append(_online_update(jnp.where(keep_w, _dot(qs[hh], k_t), NEG), v_aug, None))
    for hh in range(N_HEADS):
        acc_w = wstate[hh][0]
        o_win = acc_w[:, 0:HEAD_DIM] / acc_w[:, HEAD_DIM:HEAD_DIM + 1]
        o = (o_cmp[hh] * gates[:, 3 * hh:3 * hh + 1] + o_sel[hh] * gates[:, 3 * hh + 1:3 * hh + 2]
             + o_win * gates[:, 3 * hh + 2:3 * hh + 3])
        o_ref[:, HEAD_DIM * hh:HEAD_DIM * (hh + 1)] = o.astype(o_ref.dtype)


def _attn_prompt(q, cmp_t, kvbt, gates, mt, e_mat, nbatch, seq, tq):
    assert tq >= WINDOW and seq >= 2 * tq
    n = q.shape[0]
    nq = seq // tq
    ncmp = cmp_t.shape[3]
    kv_spec = lambda slot: pl.BlockSpec((1, 1, N_KV, HEAD_DIM, seq), lambda b, i: (slot, b, 0, 0, 0))
    return pl.pallas_call(
        functools.partial(_attn_prompt_kernel, tq=tq, nblk=seq // SEL_BLOCK),
        out_shape=jax.ShapeDtypeStruct((n, N_HEADS * HEAD_DIM), BF16),
        grid=(nbatch, nq),
        in_specs=[
            pl.BlockSpec((tq, N_HEADS * HEAD_DIM), lambda b, i: (b * nq + i, 0)),
            pl.BlockSpec((2, 1, LANES, ncmp), lambda b, i: (0, b, 0, 0)),
            kv_spec(0), kv_spec(1), kv_spec(2), kv_spec(3),
            pl.BlockSpec((tq, LANES), lambda b, i: (b * nq + i, 0)),
            pl.BlockSpec(mt.shape, lambda b, i: (0, 0)),
            pl.BlockSpec(e_mat.shape, lambda b, i: (0, 0)),
        ],
        out_specs=pl.BlockSpec((tq, N_HEADS * HEAD_DIM), lambda b, i: (b * nq + i, 0)),
        scratch_shapes=[
            pltpu.VMEM((N_KV, LANES + HEAD_DIM, seq), BF16),
            pltpu.VMEM((N_KV, LANES, seq), BF16),
            pltpu.VMEM((N_KV, LANES, seq), BF16),
        ],
        compiler_params=_cparams("arbitrary", "arbitrary"),
        name="attn_prompt",
    )(q, cmp_t, kvbt, kvbt, kvbt, kvbt, gates, mt, e_mat)


def _imp_matrix(ncmp, nblk_pad):
    per = SEL_BLOCK // CMP_STRIDE
    pad = CMP_BLOCK // CMP_STRIDE - 1
    n = np.arange(ncmp)[:, None]
    j = np.arange(nblk_pad)[None, :]
    return ((n >= per * j - pad) & (n <= per * j + per - 1)).astype(np.float32)


def _attn_sample_cmp_kernel(q_ref, kc_ref, vc_ref, m_ref, oc_ref, idx_ref, *, n_valid_cmp, qpos0, nblk):
    ncmp = kc_ref.shape[2]
    rows = q_ref.shape[2]
    t_row = lax.broadcasted_iota(jnp.int32, (rows, 1), 0) % 8
    c_idx = lax.broadcasted_iota(jnp.int32, (1, ncmp), 1)
    mask_c = (c_idx * CMP_STRIDE + (CMP_BLOCK - 1) <= qpos0 + t_row) & (c_idx < n_valid_cmp)
    p_tok = []
    for g in range(N_KV):
        gs = slice(HEAD_DIM * g, HEAD_DIM * (g + 1))
        s = jnp.where(mask_c, _dot_nt(q_ref[0, g], kc_ref[0, 0, :, gs]), -jnp.inf)
        m = jnp.max(s, axis=-1, keepdims=True)
        m = jnp.where(m == -jnp.inf, 0.0, m)
        e = jnp.exp(s - m)
        d = jnp.sum(e, axis=-1, keepdims=True)
        p = e / jnp.where(d > 0, d, 1.0)
        oc_ref[0, g] = _dot(p.astype(BF16), vc_ref[0, 0, :, gs])
        p_tok.append(jnp.sum(p.reshape(GROUP_R, 8, ncmp), axis=0))
    p_all = jnp.concatenate(p_tok, axis=0)
    p_hi = p_all.astype(BF16)
    p_lo = (p_all - p_hi.astype(F32)).astype(BF16)
    imp = _dot(p_hi, m_ref[...]) + _dot(p_lo, m_ref[...])
    nb_pad = imp.shape[1]
    jb = lax.broadcasted_iota(jnp.int32, (1, nb_pad), 1)
    qpos = qpos0 + lax.broadcasted_iota(jnp.int32, (2 * 8, 1), 0) % 8
    cur = qpos // SEL_BLOCK
    valid = (jb * SEL_BLOCK <= qpos) & (jb < nblk)
    forced = (jb == 0) | (jb == cur) | (jb == cur - 1)
    score = jnp.where(valid, jnp.where(forced, jnp.inf, imp), -jnp.inf)
    avail = jb < nblk
    lane = lax.broadcasted_iota(jnp.int32, (2 * 8, LANES), 1)
    picked = jnp.zeros((2 * 8, LANES), jnp.int32)
    for k in range(N_SEL):
        best = jnp.max(jnp.where(avail, score, -jnp.inf), axis=-1, keepdims=True)
        cand = avail & (score == best)
        idx = jnp.min(jnp.where(cand, jb, nb_pad), axis=-1, keepdims=True)
        picked = jnp.where(lane == k, idx, picked)
        avail = avail & (jb != idx)
    idx_ref[0] = picked


def _attn_sample_cmp(qg, cmp_kv, m_mat, n_valid_cmp, qpos0, nblk):
    nbatch = qg.shape[0]
    ncmp = cmp_kv.shape[2]
    rows = qg.shape[2]
    cmp_spec = lambda slot: pl.BlockSpec((1, 1, ncmp, LANES), lambda b: (slot, b, 0, 0))
    return pl.pallas_call(
        functools.partial(_attn_sample_cmp_kernel, n_valid_cmp=n_valid_cmp, qpos0=qpos0, nblk=nblk),
        out_shape=(
            jax.ShapeDtypeStruct((nbatch, N_KV, rows, HEAD_DIM), F32),
            jax.ShapeDtypeStruct((nbatch, 2 * 8, LANES), jnp.int32),
        ),
        grid=(nbatch,),
        in_specs=[
            pl.BlockSpec((1, N_KV, rows, HEAD_DIM), lambda b: (b, 0, 0, 0)),
            cmp_spec(0), cmp_spec(1),
            pl.BlockSpec(m_mat.shape, lambda b: (0, 0)),
        ],
        out_specs=(
            pl.BlockSpec((1, N_KV, rows, HEAD_DIM), lambda b: (b, 0, 0, 0)),
            pl.BlockSpec((1, 2 * 8, LANES), lambda b: (b, 0, 0)),
        ),
        compiler_params=_cparams("parallel"),
        name="attn_sample_cmp",
    )(qg, cmp_kv, cmp_kv, m_mat)


def _masked_softmax(s, mask):
    s = jnp.where(mask, s, -jnp.inf)
    m = jnp.max(s, axis=-1, keepdims=True)
    m = jnp.where(m == -jnp.inf, 0.0, m)
    e = jnp.exp(s - m)
    d = jnp.sum(e, axis=-1, keepdims=True)
    return e / jnp.where(d > 0, d, 1.0)


def _attn_sample_sel_kernel(idx_ref, pt_ref, *refs, dec_seq, qpos0, n_cache_blk, wb):
    del pt_ref
    nsel = N_KV * N_SEL
    per_page = PAGE_SIZE // SEL_BLOCK
    kv_refs = refs[:nsel]
    q_ref, tail_ref, kw_ref, vw_ref, wnew_ref, oc_ref, gate_ref, o_ref, osel_ref = refs[nsel:]
    b = pl.program_id(0)
    t = pl.program_id(1)
    rows = q_ref.shape[2]
    t_row = lax.broadcasted_iota(jnp.int32, (rows, 1), 0) % 8
    qpos = qpos0 + t_row
    colk = lax.broadcasted_iota(jnp.int32, (1, N_SEL * PAGE_SIZE), 1)
    slot_of_col = colk // PAGE_SIZE
    row_in_page = colk % PAGE_SIZE

    @pl.when(t == 0)
    def _():
        osel_ref[...] = jnp.zeros(osel_ref.shape, F32)

    for g in range(N_KV):
        blk_of_col = jnp.zeros_like(colk)
        k_pages, v_pages = [], []
        for k in range(N_SEL):
            blk = idx_ref[((b * dec_seq + t) * N_KV + g) * N_SEL + k]
            is_new = blk >= n_cache_blk
            k_pages.append(jnp.where(is_new, tail_ref[0, 0, g], kv_refs[g * N_SEL + k][0, 0, 0]))
            v_pages.append(jnp.where(is_new, tail_ref[1, 0, g], kv_refs[g * N_SEL + k][1, 0, 0]))
            blk_of_col = blk_of_col + jnp.where(slot_of_col == k, blk, 0)
        in_block = row_in_page // SEL_BLOCK == blk_of_col % per_page
        kpos = blk_of_col * SEL_BLOCK + row_in_page % SEL_BLOCK
        k_t = jnp.concatenate(k_pages, axis=1).astype(BF16)
        v_t = jnp.concatenate(v_pages, axis=1).astype(BF16)
        p = _masked_softmax(_dot(q_ref[0, g], k_t), in_block & (kpos <= qpos))
        o = _dot_nt(p.astype(BF16), v_t)
        osel_ref[g] = osel_ref[g] + jnp.where(t_row == t, o, 0.0)

    @pl.when(t == dec_seq - 1)
    def _():
        colw = lax.broadcasted_iota(jnp.int32, (1, wb + LANES), 1)
        kwpos = jnp.where(colw < wb, qpos0 - wb + colw, qpos0 + colw - wb)
        dpos = qpos - kwpos
        mask_w = (dpos >= 0) & (dpos < WINDOW) & (kwpos >= 0) & (colw < wb + dec_seq)
        for g in range(N_KV):
            k_t = jnp.concatenate([kw_ref[0, 0, g], wnew_ref[0, 0, g]], axis=1).astype(BF16)
            v_t = jnp.concatenate([vw_ref[0, 0, g], wnew_ref[1, 0, g]], axis=1).astype(BF16)
            p = _masked_softmax(_dot(q_ref[0, g], k_t), mask_w)
            o_win = _dot_nt(p.astype(BF16), v_t)
            gt = gate_ref[0, g]
            o_ref[0, g] = oc_ref[0, g] * gt[:, 0:1] + osel_ref[g] * gt[:, 1:2] + o_win * gt[:, 2:3]


def _attn_sample_sel(idx_flat, pt_flat, cache_t, qg, tail_t, win_t, wnew_t, o_cmp, gates_g, dec_seq, n_pages,
                     qpos0):
    nbatch = qg.shape[0]
    rows = qg.shape[2]
    wb = win_t.shape[4]
    per_page = PAGE_SIZE // SEL_BLOCK
    n_cache_blk = n_pages * per_page

    def page_spec(g, k):
        def imap(b, t, idx, pt):
            blk = jnp.minimum(idx[((b * dec_seq + t) * N_KV + g) * N_SEL + k], n_cache_blk - 1)
            return (1, pt[b * n_pages + blk // per_page], g, 0, 0)
        return pl.BlockSpec((2, 1, 1, HEAD_DIM, PAGE_SIZE), imap)

    kv_specs = [page_spec(g, k) for g in range(N_KV) for k in range(N_SEL)]
    per_b = lambda shape: pl.BlockSpec((1,) + shape, lambda b, t, idx, pt: (b,) + (0,) * len(shape))
    new_spec = pl.BlockSpec((2, 1, N_KV, HEAD_DIM, LANES), lambda b, t, idx, pt: (0, b, 0, 0, 0))
    win_spec = lambda s: pl.BlockSpec((1, 1, N_KV, HEAD_DIM, wb), lambda b, t, idx, pt: (s, b, 0, 0, 0))
    grid_spec = pltpu.PrefetchScalarGridSpec(
        num_scalar_prefetch=2,
        grid=(nbatch, dec_seq),
        in_specs=kv_specs + [
            per_b((N_KV, rows, HEAD_DIM)),
            new_spec, win_spec(0), win_spec(1), new_spec,
            per_b((N_KV, rows, HEAD_DIM)),
            per_b((N_KV, rows, LANES)),
        ],
        out_specs=per_b((N_KV, rows, HEAD_DIM)),
        scratch_shapes=[pltpu.VMEM((N_KV, rows, HEAD_DIM), F32)],
    )
    n_page_specs = N_KV * N_SEL
    return pl.pallas_call(
        functools.partial(_attn_sample_sel_kernel, dec_seq=dec_seq, qpos0=qpos0, n_cache_blk=n_cache_blk, wb=wb),
        out_shape=jax.ShapeDtypeStruct((nbatch, N_KV, rows, HEAD_DIM), F32),
        grid_spec=grid_spec,
        compiler_params=_cparams("parallel", "arbitrary"),
        name="attn_sample_sel",
    )(idx_flat, pt_flat, *([cache_t] * n_page_specs), qg, tail_t, win_t, win_t, wnew_t, o_cmp, gates_g)


FFN_CHUNK = 256


def _post_kernel(*refs, n_a, final):
    y_ref = refs[0]
    a_refs = refs[1:1 + 2 * n_a]
    gmix_ref, g2_ref, sh_ref, sc_ref, gffn_ref, w1_ref, w2_ref = refs[1 + 2 * n_a:8 + 2 * n_a]
    rest = refs[8 + 2 * n_a:]
    out_ref = rest[-1]
    mix = _dot(a_refs[0][...], a_refs[1][...])
    for k in range(1, n_a):
        mix = mix + _dot(a_refs[2 * k][...], a_refs[2 * k + 1][...])
    y1 = y_ref[...] + gmix_ref[0] * mix
    h = _modulate(y1, g2_ref[...], sh_ref[0], sc_ref[0]).astype(BF16)
    d_ff = w2_ref.shape[0]
    acc = jnp.zeros(y1.shape, F32)
    for c in range(d_ff // FFN_CHUNK):
        c0 = c * FFN_CHUNK
        gate = _dot(h, w1_ref[:, c0:c0 + FFN_CHUNK])
        up = _dot(h, w1_ref[:, d_ff + c0:d_ff + c0 + FFN_CHUNK])
        acc = acc + _dot((_silu(gate) * up).astype(BF16), w2_ref[c0:c0 + FFN_CHUNK, :])
    y2 = y1 + gffn_ref[0] * acc
    if final:
        fg_ref = rest[0]
        ms = jnp.mean(y2 * y2, axis=-1, keepdims=True)
        y2 = (y2 * lax.rsqrt(ms + NORM_EPS)) * fg_ref[...]
    out_ref[...] = y2


def _post(y, a_list, wo_list, gmix, g2, shift, scale, gffn, w1, w2, final_g, tm, rows_per_mod):
    n, d = y.shape
    r = shift.shape[1]
    mod_spec = pl.BlockSpec((1, r, d), lambda i: (i // rows_per_mod, 0, 0))
    const = lambda arr: pl.BlockSpec(arr.shape, lambda i: (0,) * arr.ndim, pipeline_mode=pl.Buffered(1))
    in_specs = [pl.BlockSpec((tm, d), lambda i: (i, 0))]
    args = [y]
    for a, wo in zip(a_list, wo_list):
        in_specs += [pl.BlockSpec((tm, a.shape[1]), lambda i: (i, 0)), const(wo)]
        args += [a, wo]
    in_specs += [mod_spec, pl.BlockSpec((1, d), lambda i: (0, 0)), mod_spec, mod_spec, mod_spec, const(w1), const(w2)]
    args += [gmix, g2, shift, scale, gffn, w1, w2]
    if final_g is not None:
        in_specs.append(pl.BlockSpec((1, d), lambda i: (0, 0)))
        args.append(final_g)
    return pl.pallas_call(
        functools.partial(_post_kernel, n_a=len(a_list), final=final_g is not None),
        out_shape=jax.ShapeDtypeStruct((n, d), F32),
        grid=(n // tm,),
        in_specs=in_specs,
        out_specs=pl.BlockSpec((tm, d), lambda i: (i, 0)),
        compiler_params=_cparams("parallel"),
        name="post_proj_ffn",
    )(*args)


def _pre1_kernel(x_ref, g_ref, sh_ref, sc_ref, w_ref, u_ref):
    h = _modulate(x_ref[...], g_ref[...], sh_ref[0], sc_ref[0]).astype(BF16)
    d = u_ref.shape[1]
    a = _dot(h, w_ref[:, 0:d])
    b = _dot(h, w_ref[:, d:2 * d])
    u_ref[...] = a * _sigmoid(b)


def _pre1(x2, g, shift, scale, pw1, tm, rows_per_mod):
    n, d = x2.shape
    r = shift.shape[1]
    mod_spec = pl.BlockSpec((1, r, d), lambda i: (i // rows_per_mod, 0, 0))
    return pl.pallas_call(
        _pre1_kernel,
        out_shape=jax.ShapeDtypeStruct((n, d), F32),
        grid=(n // tm,),
        in_specs=[
            pl.BlockSpec((tm, d), lambda i: (i, 0)),
            pl.BlockSpec((1, d), lambda i: (0, 0)),
            mod_spec, mod_spec,
            pl.BlockSpec(pw1.shape, lambda i: (0, 0)),
        ],
        out_specs=pl.BlockSpec((tm, d), lambda i: (i, 0)),
        compiler_params=_cparams("parallel"),
        name="pre1_pw_glu",
    )(x2, g, shift, scale, pw1)


def _conv_body(ext_ref, t_rows, dw_ref, dwb_ref, lng_ref, lnb_ref, out_ref):
    off = CONV_PAD - CONV_HIST
    acc = None
    for b in range(8):
        span = t_rows + (8 if b else 0)
        part = None
        for k in range(CONV_WIDTH):
            if (off + k) % 8 != b:
                continue
            base = off + k - b
            term = ext_ref[base:base + span, :] * dw_ref[k:k + 1, :]
            part = term if part is None else part + term
        part = part[b:b + t_rows, :]
        acc = part if acc is None else acc + part
    y = acc + dwb_ref[...]
    mu = jnp.mean(y, axis=-1, keepdims=True)
    var = jnp.mean(jnp.square(y - mu), axis=-1, keepdims=True)
    z = (y - mu) * lax.rsqrt(var + NORM_EPS) * lng_ref[...] + lnb_ref[...]
    out_ref[...] = _silu(z).astype(out_ref.dtype)


def _pre1_conv_kernel(x_ref, g_ref, sh_ref, sc_ref, w_ref, dw_ref, dwb_ref, lng_ref, lnb_ref,
                      out_ref, state_ref, ext_ref, *, tiles_per_batch, tm):
    t_in_b = pl.program_id(0) % tiles_per_batch

    @pl.when(t_in_b == 0)
    def _():
        ext_ref[0:CONV_PAD, :] = jnp.zeros((CONV_PAD, ext_ref.shape[1]), F32)

    @pl.when(t_in_b > 0)
    def _():
        ext_ref[0:CONV_PAD, :] = ext_ref[tm:tm + CONV_PAD, :]

    h = _modulate(x_ref[...], g_ref[...], sh_ref[0], sc_ref[0]).astype(BF16)
    d = out_ref.shape[1]
    a = _dot(h, w_ref[:, 0:d])
    b = _dot(h, w_ref[:, d:2 * d])
    ext_ref[CONV_PAD:, :] = a * _sigmoid(b)
    _conv_body(ext_ref, tm, dw_ref, dwb_ref, lng_ref, lnb_ref, out_ref)
    state_ref[0] = ext_ref[tm:tm + CONV_PAD, :]


def _pre1_conv_prompt(x2, g, shift, scale, pw1, dw, dwb, lng, lnb, seq, tm):
    assert tm >= CONV_PAD
    n, d = x2.shape
    tpb = seq // tm
    mod_spec = pl.BlockSpec((1, 1, d), lambda i: (i // tpb, 0, 0))
    vec = pl.BlockSpec((1, d), lambda i: (0, 0))
    return pl.pallas_call(
        functools.partial(_pre1_conv_kernel, tiles_per_batch=tpb, tm=tm),
        out_shape=(jax.ShapeDtypeStruct((n, d), BF16), jax.ShapeDtypeStruct((n // seq, CONV_PAD, d), F32)),
        grid=(n // tm,),
        in_specs=[
            pl.BlockSpec((tm, d), lambda i: (i, 0)),
            vec, mod_spec, mod_spec,
            pl.BlockSpec(pw1.shape, lambda i: (0, 0)),
            pl.BlockSpec(dw.shape, lambda i: (0, 0)),
            vec, vec, vec,
        ],
        out_specs=(pl.BlockSpec((tm, d), lambda i: (i, 0)),
                   pl.BlockSpec((1, CONV_PAD, d), lambda i: (i // tpb, 0, 0))),
        scratch_shapes=[pltpu.VMEM((CONV_PAD + tm, d), F32)],
        compiler_params=_cparams("arbitrary"),
        name="pre1_conv_prompt",
    )(x2, g, shift, scale, pw1, dw, dwb, lng, lnb)


def _conv_sample_kernel(ext_ref, dw_ref, dwb_ref, lng_ref, lnb_ref, out_ref, *, t_rows):
    _conv_body(ext_ref.at[0], t_rows, dw_ref, dwb_ref, lng_ref, lnb_ref, out_ref.at[0])


def _conv_sample(ext, dw, dwb, lng, lnb):
    b, rows, d = ext.shape
    t_rows = rows - CONV_PAD
    vec = pl.BlockSpec((1, d), lambda i: (0, 0))
    return pl.pallas_call(
        functools.partial(_conv_sample_kernel, t_rows=t_rows),
        out_shape=jax.ShapeDtypeStruct((b, t_rows, d), BF16),
        grid=(b,),
        in_specs=[
            pl.BlockSpec((1, rows, d), lambda i: (i, 0, 0)),
            pl.BlockSpec(dw.shape, lambda i: (0, 0)),
            vec, vec, vec,
        ],
        out_specs=pl.BlockSpec((1, t_rows, d), lambda i: (i, 0, 0)),
        compiler_params=_cparams("parallel"),
        name="conv_sample",
    )(ext, dw, dwb, lng, lnb)


ROW_TILE = 512
ATTN_TILE = 512


def _group_rows(x, nbatch, dec_seq):
    w = x.shape[1] // N_HEADS
    x = x.reshape(nbatch, dec_seq, N_KV, GROUP_R, w).transpose(0, 2, 3, 1, 4)
    x = jnp.pad(x, ((0, 0), (0, 0), (0, 0), (0, 8 - dec_seq), (0, 0)))
    return x.reshape(nbatch, N_KV, GROUP_R * 8, w)


def _ungroup_rows(x, nbatch, dec_seq):
    w = x.shape[3]
    x = x.reshape(nbatch, N_KV, GROUP_R, 8, w)[:, :, :, :dec_seq]
    return x.transpose(0, 3, 1, 2, 4).reshape(nbatch * dec_seq, N_HEADS * w)


def kernel(x_prompt, x_sample, cache_kv, page_table, cache_win, state_pool, state_conv, c_prompt, c_sample,
           ada_w, ada_b, norm_g, attn_w_in, attn_w_out, pool_w, pool_scale, cmp_pos, cmp_w1, cmp_w2,
           conv_pw1, conv_dw, conv_dw_b, conv_ln_g, conv_ln_b, conv_pw2, ffn_w1, ffn_w2, final_g):
    nb_p, seq, d = x_prompt.shape
    nb_s, dec_seq, _ = x_sample.shape
    n_p, n_s = nb_p * seq, nb_s * dec_seq
    n_pages = page_table.shape[1]
    past = n_pages * PAGE_SIZE
    wb = cache_win.shape[3]
    tm = min(ROW_TILE, seq)
    tq = min(ATTN_TILE, seq)
    pool_width = pool_w.shape[1] * pool_w.shape[2]
    att_width = N_HEADS * HEAD_DIM

    mods = _ada(jnp.concatenate([c_prompt, c_sample], axis=0), ada_w, ada_b)

    def mod_p(layer, k):
        return mods[layer, :nb_p, k * d:(k + 1) * d].reshape(nb_p, 1, d)

    def mod_s(layer, k):
        return jnp.repeat(mods[layer, nb_p:, k * d:(k + 1) * d], dec_seq, axis=0).reshape(1, n_s, d)

    y_p = x_prompt.reshape(n_p, d)
    y_s = x_sample.reshape(n_s, d)
    row = lambda v: v.reshape(1, -1)

    w_in = attn_w_in[0]
    w_in_pad = jnp.pad(w_in.T, ((0, (-w_in.shape[1]) % LANES), (0, 0))).astype(BF16)
    cos_p, sin_p = _rope_tables(np.arange(seq))
    cos_s, sin_s = _rope_tables(past + np.arange(n_s) % dec_seq)
    g0 = row(norm_g[0, 0])
    u_p, q_p, gate_p, kvt_p, wint_p, kvbt_p, cmp_in_p = _pre0(
        y_p, g0, mod_p(0, 0), mod_p(0, 1), w_in_pad, cos_p, sin_p, tm, seq // tm, seq // tm, seq=seq)
    u_s, q_s, gate_s, kv_s, win_s = _pre0(y_s, g0, mod_s(0, 0), mod_s(0, 1), w_in_pad, cos_s, sin_s, n_s, 1, 1)

    pool_w_b = pool_w[0].astype(BF16)
    pool_sc = row(pool_scale[0])
    ypool_p = _pool_prompt(u_p, pool_w_b, pool_sc, seq, tm)
    u_s3 = u_s.reshape(nb_s, dec_seq, pool_width)
    pool_ext = jnp.concatenate([
        jnp.zeros((nb_s, POOL_PAD - POOL_HIST, pool_width), F32), state_pool[0], u_s3,
        jnp.zeros((nb_s, 8 - dec_seq, pool_width), F32)], axis=1)
    ypool_s = _pool_sample(pool_ext, pool_w_b, pool_sc, past)[:, :dec_seq].reshape(n_s, pool_width)

    pos_ab, w1_ab, w2_bd = _cmp_weights(cmp_pos[0], cmp_w1[0], cmp_w2[0])
    cmp_in_p4 = cmp_in_p.reshape(2, nb_p, seq, LANES)
    _, cmp_t_p = _cmp_combine([_cmp_rows(cmp_in_p4, pos_ab, w1_ab, nb_p, seq)], w2_bd)
    cache_t = cache_kv[0].transpose(0, 1, 3, 4, 2)
    pt_flat = page_table.reshape(-1)
    part_past = _cmp_pages(cache_t, pt_flat, n_pages, pos_ab, w1_ab, nb_s)
    tail_rows = 2 * PAGE_SIZE
    kv_s4 = kv_s.reshape(4, nb_s, dec_seq, LANES)
    tail4 = jnp.pad(kv_s4, ((0, 0), (0, 0), (0, tail_rows - dec_seq), (0, 0)))
    part_tail = _cmp_rows(tail4, pos_ab, w1_ab, nb_s, tail_rows)
    cmp_s, _ = _cmp_combine([part_past, part_tail], w2_bd)
    total_len = past + dec_seq
    padded_len = -(-total_len // SEL_BLOCK) * SEL_BLOCK
    n_cmp_s = padded_len // CMP_STRIDE - CMP_BLOCK // CMP_STRIDE + 1
    nblk_s = padded_len // SEL_BLOCK

    assert seq // SEL_BLOCK <= LANES and dec_seq <= SEL_BLOCK
    ncmp_p = cmp_t_p.shape[3]
    mt_p = jnp.asarray(_imp_matrix(ncmp_p, LANES).T, BF16)
    e_mat = jnp.asarray((np.arange(seq)[None, :] // SEL_BLOCK == np.arange(LANES)[:, None]).astype(np.float32),
                        BF16)
    o_p = _attn_prompt(q_p, cmp_t_p, kvbt_p, gate_p, mt_p, e_mat, nb_p, seq, tq)

    qg_s = _group_rows(q_s, nb_s, dec_seq)
    nblk_pad = -(-nblk_s // LANES) * LANES
    m_s = jnp.asarray(_imp_matrix(cmp_s.shape[2], nblk_pad), BF16)
    o_cmp_s, picked = _attn_sample_cmp(qg_s, cmp_s, m_s, n_cmp_s, past, nblk_s)
    idx = picked.reshape(nb_s, N_KV, 8, LANES)[:, :, :dec_seq, :N_SEL].transpose(0, 2, 1, 3).reshape(-1)

    def new_rows_t(rows):
        x = rows.reshape(rows.shape[0], nb_s, dec_seq, N_KV, HEAD_DIM).transpose(0, 1, 3, 4, 2)
        return jnp.pad(x, ((0, 0), (0, 0), (0, 0), (0, 0), (0, LANES - dec_seq)))

    tail_t = new_rows_t(kv_s[2:4])
    wnew_t = new_rows_t(win_s)
    win_t = cache_win[0].transpose(0, 1, 3, 4, 2)
    gates_g = jnp.pad(_group_rows(gate_s[:, :3 * N_HEADS], nb_s, dec_seq), ((0, 0), (0, 0), (0, 0), (0, LANES - 3)))
    o_s = _attn_sample_sel(idx, pt_flat, cache_t, qg_s, tail_t, win_t, wnew_t, o_cmp_s, gates_g,
                           dec_seq, n_pages, past)
    o_s = _ungroup_rows(o_s, nb_s, dec_seq).astype(BF16)

    w_out = attn_w_out[0].astype(BF16)
    wo_list = [w_out[:pool_width], w_out[pool_width:]]
    g1 = row(norm_g[0, 1])
    ffn1_0, ffn2_0 = ffn_w1[0].astype(BF16), ffn_w2[0].astype(BF16)
    y_p = _post(y_p, [ypool_p, o_p], wo_list, mod_p(0, 2), g1, mod_p(0, 3), mod_p(0, 4), mod_p(0, 5),
                ffn1_0, ffn2_0, None, tm, seq // tm)
    y_s = _post(y_s, [ypool_s, o_s], wo_list, mod_s(0, 2), g1, mod_s(0, 3), mod_s(0, 4), mod_s(0, 5),
                ffn1_0, ffn2_0, None, n_s, 1)

    g0 = row(norm_g[1, 0])
    pw1 = conv_pw1[0].astype(BF16)
    dwb, lng, lnb = row(conv_dw_b[0]), row(conv_ln_g[0]), row(conv_ln_b[0])
    cv_p, uc_tail_p = _pre1_conv_prompt(y_p, g0, mod_p(1, 0), mod_p(1, 1), pw1, conv_dw[0], dwb, lng, lnb, seq, tm)
    uc_s = _pre1(y_s, g0, mod_s(1, 0), mod_s(1, 1), pw1, n_s, 1)
    uc_s3 = uc_s.reshape(nb_s, dec_seq, d)
    conv_ext = jnp.concatenate([
        jnp.zeros((nb_s, CONV_PAD - CONV_HIST, d), F32), state_conv[0], uc_s3,
        jnp.zeros((nb_s, 8 - dec_seq, d), F32)], axis=1)
    cv_s = _conv_sample(conv_ext, conv_dw[0], dwb, lng, lnb)[:, :dec_seq].reshape(n_s, d)

    pw2 = conv_pw2[0].astype(BF16)
    g1 = row(norm_g[1, 1])
    ffn1_1, ffn2_1 = ffn_w1[1].astype(BF16), ffn_w2[1].astype(BF16)
    fg = row(final_g)
    y_p = _post(y_p, [cv_p], [pw2], mod_p(1, 2), g1, mod_p(1, 3), mod_p(1, 4), mod_p(1, 5),
                ffn1_1, ffn2_1, fg, tm, seq // tm)
    y_s = _post(y_s, [cv_s], [pw2], mod_s(1, 2), g1, mod_s(1, 3), mod_s(1, 4), mod_s(1, 5),
                ffn1_1, ffn2_1, fg, n_s, 1)

    y_prompt = y_p.reshape(nb_p, seq, d)
    y_sample = y_s.reshape(nb_s, dec_seq, d)
    kv_prompt = kvt_p.transpose(0, 1, 4, 2, 3)[None]
    kv_sample = kv_s.reshape(1, 4, nb_s, dec_seq, N_KV, HEAD_DIM)
    if seq >= wb:
        win_prompt_t = wint_p[..., seq - wb:]
    else:
        win_prompt_t = jnp.pad(wint_p, ((0, 0),) * 4 + ((wb - seq, 0),))
    win_prompt = win_prompt_t.transpose(0, 1, 4, 2, 3)
    win_sample_t = jnp.concatenate([win_t, wnew_t[..., :dec_seq]], axis=-1)[..., -wb:]
    win_sample = win_sample_t.transpose(0, 1, 4, 2, 3)

    def last_rows(x3, hist, state):
        full = x3 if state is None and x3.shape[1] >= hist else jnp.concatenate(
            [jnp.zeros((x3.shape[0], hist, x3.shape[2]), F32) if state is None else state, x3], axis=1)
        return full[:, full.shape[1] - hist:]

    pool_prompt = last_rows(u_p.reshape(nb_p, seq, pool_width), POOL_HIST, None)
    pool_sample = last_rows(u_s3, POOL_HIST, state_pool[0])
    conv_prompt = uc_tail_p[:, CONV_PAD - CONV_HIST:]
    conv_sample = last_rows(uc_s3, CONV_HIST, state_conv[0])
    return (y_prompt, y_sample, kv_prompt, kv_sample, win_prompt[None], win_sample[None],
            pool_prompt[None], pool_sample[None], conv_prompt[None], conv_sample[None])
```

```python
import functools

import numpy as np
import jax
import jax.numpy as jnp
from jax import lax
from jax.experimental import pallas as pl
from jax.experimental.pallas import tpu as pltpu

F32 = jnp.float32
BF16 = jnp.bfloat16

NORM_EPS = 1e-6
N_HEADS = 8
HEAD_DIM = 64
N_KV = 2
GROUP_R = N_HEADS // N_KV
POOL_WINDOWS = (2, 4, 8, 16)
POOL_HIST = 15
POOL_PAD = 16
CMP_BLOCK = 32
CMP_STRIDE = 16
SEL_BLOCK = 64
N_SEL = 16
WINDOW = 512
PAGE_SIZE = 128
ROPE_THETA = 10000.0
CONV_WIDTH = 31
CONV_HIST = CONV_WIDTH - 1
CONV_PAD = 32
LANES = 128
NEG = -1e9
V7X_VMEM_LIMIT = 56 * 1024 * 1024


def _cparams(*sem):
    return pltpu.CompilerParams(dimension_semantics=sem, vmem_limit_bytes=V7X_VMEM_LIMIT)


def _dot(a, b):
    return jnp.dot(a, b, preferred_element_type=F32)


def _dot_nt(a, b):
    return lax.dot_general(a, b, (((1,), (1,)), ((), ())), preferred_element_type=F32)


def _sigmoid(x):
    return 1.0 / (1.0 + jnp.exp(-x))


def _silu(x):
    return x * _sigmoid(x)


def _modulate(x, g, shift, scale):
    ms = jnp.mean(x * x, axis=-1, keepdims=True)
    y = x * lax.rsqrt(ms + NORM_EPS)
    return (y * g) * (1.0 + scale) + shift


def _ada_kernel(c_ref, w_ref, b_ref, o_ref):
    a = _silu(c_ref[...]).astype(BF16)
    o_ref[0] = _dot(a, w_ref[0].astype(BF16)) + b_ref[0]


def _ada(c_all, ada_w, ada_b):
    depth, d, n6 = ada_w.shape
    bc = c_all.shape[0]
    tn = n6 // 4
    return pl.pallas_call(
        _ada_kernel,
        out_shape=jax.ShapeDtypeStruct((depth, bc, n6), F32),
        grid=(depth, n6 // tn),
        in_specs=[
            pl.BlockSpec((bc, d), lambda l, j: (0, 0)),
            pl.BlockSpec((1, d, tn), lambda l, j: (l, 0, j)),
            pl.BlockSpec((1, 1, tn), lambda l, j: (l, 0, j)),
        ],
        out_specs=pl.BlockSpec((1, bc, tn), lambda l, j: (l, 0, j)),
        compiler_params=_cparams("parallel", "parallel"),
        name="ada_mod",
    )(c_all, ada_w, ada_b.reshape(depth, 1, n6))


def _rope(x, cos, sin_signed):
    lane = lax.broadcasted_iota(jnp.int32, x.shape, 1)
    first = (lane % HEAD_DIM) < (HEAD_DIM // 2)
    swapped = jnp.where(first, pltpu.roll(x, LANES - HEAD_DIM // 2, 1), pltpu.roll(x, HEAD_DIM // 2, 1))
    return x * cos + swapped * sin_signed


def _pre0_project(x_ref, g_ref, sh_ref, sc_ref, w_ref, cos_ref, sin_ref, u_ref, q_ref, gate_ref):
    h = _modulate(x_ref[...], g_ref[...], sh_ref[0], sc_ref[0]).astype(BF16)
    cos = cos_ref[...]
    sin = sin_ref[...]
    full = _dot_nt(h, w_ref[...])

    def proj(c0, width):
        return full[:, c0:c0 + width]

    u_ref[...] = proj(0, 512)
    for j in range(4):
        qj = _rope(proj(512 + LANES * j, LANES), cos, sin)
        q_ref[:, LANES * j:LANES * (j + 1)] = (qj * (HEAD_DIM ** -0.5)).astype(BF16)
    gate_ref[...] = _sigmoid(proj(1792, LANES))
    kc = _rope(proj(1024, LANES), cos, sin)
    vc = proj(1152, LANES)
    ks = _rope(proj(1280, LANES), cos, sin)
    vs = proj(1408, LANES)
    kw = _rope(proj(1536, LANES), cos, sin)
    vw = proj(1664, LANES)
    return kc, vc, ks, vs, kw, vw


def _pre0_rows_kernel(x_ref, g_ref, sh_ref, sc_ref, w_ref, cos_ref, sin_ref,
                      u_ref, q_ref, gate_ref, kv_ref, win_ref):
    kc, vc, ks, vs, kw, vw = _pre0_project(x_ref, g_ref, sh_ref, sc_ref, w_ref, cos_ref, sin_ref,
                                           u_ref, q_ref, gate_ref)
    for k, seg in enumerate((kc, vc, ks, vs)):
        kv_ref[k] = seg
    win_ref[0] = kw
    win_ref[1] = vw


def _store_transposed(x, *targets):
    xt = x.T
    for ref, lead in targets:
        for g in range(N_KV):
            ref[lead + (g,)] = xt[HEAD_DIM * g:HEAD_DIM * (g + 1), :].astype(ref.dtype)


def _pre0_seq_kernel(x_ref, g_ref, sh_ref, sc_ref, w_ref, cos_ref, sin_ref,
                     u_ref, q_ref, gate_ref, kvt_ref, wint_ref, kvbt_ref, cmp_ref):
    kc, vc, ks, vs, kw, vw = _pre0_project(x_ref, g_ref, sh_ref, sc_ref, w_ref, cos_ref, sin_ref,
                                           u_ref, q_ref, gate_ref)
    _store_transposed(kc, (kvt_ref, (0, 0)))
    _store_transposed(vc, (kvt_ref, (1, 0)))
    _store_transposed(ks, (kvt_ref, (2, 0)), (kvbt_ref, (0, 0)))
    _store_transposed(vs, (kvt_ref, (3, 0)), (kvbt_ref, (1, 0)))
    _store_transposed(kw, (wint_ref, (0, 0)), (kvbt_ref, (2, 0)))
    _store_transposed(vw, (wint_ref, (1, 0)), (kvbt_ref, (3, 0)))
    cmp_ref[0] = kc
    cmp_ref[1] = vc


def _pre0(x2, g, shift, scale, w_pad, cos, sin, tm, rows_per_mod, pos_tiles, seq=None):
    n, d = x2.shape
    r = shift.shape[1]
    mod_spec = pl.BlockSpec((1, r, d), lambda i: (i // rows_per_mod, 0, 0))
    pos_spec = pl.BlockSpec((tm, LANES), lambda i: (i % pos_tiles, 0))
    row_spec = lambda w: pl.BlockSpec((tm, w), lambda i: (i, 0))
    out_shape = [jax.ShapeDtypeStruct((n, 512), F32), jax.ShapeDtypeStruct((n, 512), BF16),
                 jax.ShapeDtypeStruct((n, LANES), F32)]
    out_specs = [row_spec(512), row_spec(512), row_spec(LANES)]
    if seq is None:
        body = _pre0_rows_kernel
        out_shape += [jax.ShapeDtypeStruct((4, n, LANES), F32), jax.ShapeDtypeStruct((2, n, LANES), F32)]
        out_specs += [pl.BlockSpec((4, tm, LANES), lambda i: (0, i, 0)),
                      pl.BlockSpec((2, tm, LANES), lambda i: (0, i, 0))]
    else:
        body = _pre0_seq_kernel
        nb, tpb = n // seq, seq // tm
        t_spec = lambda k: pl.BlockSpec((k, 1, N_KV, HEAD_DIM, tm), lambda i: (0, i // tpb, 0, 0, i % tpb))
        out_shape += [jax.ShapeDtypeStruct((4, nb, N_KV, HEAD_DIM, seq), F32),
                      jax.ShapeDtypeStruct((2, nb, N_KV, HEAD_DIM, seq), F32),
                      jax.ShapeDtypeStruct((4, nb, N_KV, HEAD_DIM, seq), BF16),
                      jax.ShapeDtypeStruct((2, n, LANES), F32)]
        out_specs += [t_spec(4), t_spec(2), t_spec(4), pl.BlockSpec((2, tm, LANES), lambda i: (0, i, 0))]
    return pl.pallas_call(
        body,
        out_shape=tuple(out_shape),
        grid=(n // tm,),
        in_specs=[
            pl.BlockSpec((tm, d), lambda i: (i, 0)),
            pl.BlockSpec((1, d), lambda i: (0, 0)),
            mod_spec, mod_spec,
            pl.BlockSpec(w_pad.shape, lambda i: (0, 0)),
            pos_spec, pos_spec,
        ],
        out_specs=tuple(out_specs),
        compiler_params=_cparams("parallel"),
        name="pre0_in_proj",
    )(x2, g, shift, scale, w_pad, cos, sin)


def _rope_tables(pos):
    half = HEAD_DIM // 2
    inv = ROPE_THETA ** (-np.arange(half, dtype=np.float64) / half)
    ang = np.asarray(pos, np.float64)[:, None] * inv[None, :]
    cos, sin = np.cos(ang), np.sin(ang)
    cos_t = np.tile(np.concatenate([cos, cos], axis=1), (1, LANES // HEAD_DIM))
    sin_t = np.tile(np.concatenate([-sin, sin], axis=1), (1, LANES // HEAD_DIM))
    return jnp.asarray(cos_t, F32), jnp.asarray(sin_t, F32)


def _pool_body(ext_ref, t_rows, pos0, w_ref, scale_ref, out_ref):
    pos = pos0 + lax.broadcasted_iota(jnp.int32, (t_rows, 1), 0)
    for g, w in enumerate(POOL_WINDOWS):
        cols = slice(LANES * g, LANES * (g + 1))
        x = ext_ref[POOL_PAD:POOL_PAD + t_rows, cols]
        s = x
        for j in range(1, w):
            s = s + ext_ref[POOL_PAD - j:POOL_PAD - j + t_rows, cols]
        cnt = jnp.minimum(pos + 1, w).astype(F32)
        dlt = (s / cnt - x).astype(BF16)
        y = _dot(dlt, w_ref[g]) * scale_ref[:, cols]
        out_ref[:, cols] = y.astype(out_ref.dtype)


def _pool_prompt_kernel(x_ref, prev_ref, w_ref, scale_ref, out_ref, ext_ref, *, tiles_per_batch, tm):
    t_in_b = pl.program_id(0) % tiles_per_batch
    ext_ref[0:POOL_PAD, :] = jnp.where(t_in_b > 0, prev_ref[...], 0.0)
    ext_ref[POOL_PAD:, :] = x_ref[...]
    _pool_body(ext_ref, tm, t_in_b * tm, w_ref, scale_ref, out_ref)


def _pool_prompt(u, w_grp, scale, seq, tm):
    n, c = u.shape
    tpb = seq // tm
    hb = tm // POOL_PAD
    return pl.pallas_call(
        functools.partial(_pool_prompt_kernel, tiles_per_batch=tpb, tm=tm),
        out_shape=jax.ShapeDtypeStruct((n, c), BF16),
        grid=(n // tm,),
        in_specs=[
            pl.BlockSpec((tm, c), lambda i: (i, 0)),
            pl.BlockSpec((POOL_PAD, c), lambda i: (jnp.maximum(i * hb - 1, 0), 0)),
            pl.BlockSpec(w_grp.shape, lambda i: (0, 0, 0)),
            pl.BlockSpec((1, c), lambda i: (0, 0)),
        ],
        out_specs=pl.BlockSpec((tm, c), lambda i: (i, 0)),
        scratch_shapes=[pltpu.VMEM((POOL_PAD + tm, c), F32)],
        compiler_params=_cparams("parallel"),
        name="pool_prompt",
    )(u, u, w_grp, scale)


def _pool_sample_kernel(ext_ref, w_ref, scale_ref, out_ref, *, pos0, t_rows):
    _pool_body(ext_ref.at[0], t_rows, pos0, w_ref, scale_ref, out_ref.at[0])


def _pool_sample(ext, w_grp, scale, pos0):
    b, rows, c = ext.shape
    t_rows = rows - POOL_PAD
    return pl.pallas_call(
        functools.partial(_pool_sample_kernel, pos0=pos0, t_rows=t_rows),
        out_shape=jax.ShapeDtypeStruct((b, t_rows, c), BF16),
        grid=(b,),
        in_specs=[
            pl.BlockSpec((1, rows, c), lambda i: (i, 0, 0)),
            pl.BlockSpec(w_grp.shape, lambda i: (0, 0, 0)),
            pl.BlockSpec((1, c), lambda i: (0, 0)),
        ],
        out_specs=pl.BlockSpec((1, t_rows, c), lambda i: (i, 0, 0)),
        compiler_params=_cparams("parallel"),
        name="pool_sample",
    )(ext, w_grp, scale)


def _chunk_rows(ref2d, rows):
    n = rows // CMP_STRIDE
    return jnp.concatenate([ref2d[pl.ds(r, n, stride=CMP_STRIDE), :] for r in range(CMP_STRIDE)], axis=1)


def _cmp_partial(a, pos_ref, w_ref, p_ref):
    hid2 = w_ref.shape[2] // 2
    p_ref[0, 0, :, 0:hid2] = _dot((a + pos_ref[0, 0:1, :]).astype(BF16), w_ref[0, :, 0:hid2])
    p_ref[0, 0, :, hid2:] = _dot((a + pos_ref[0, 1:2, :]).astype(BF16), w_ref[0, :, hid2:])


def _cmp_rows_kernel(x_ref, pos_ref, w_ref, p_ref, *, rows):
    _cmp_partial(_chunk_rows(x_ref.at[0, 0], rows), pos_ref, w_ref, p_ref)


def _cmp_rows(x4, pos_ab, w1_ab, nbatch, rows):
    nch = rows // CMP_STRIDE
    return pl.pallas_call(
        functools.partial(_cmp_rows_kernel, rows=rows),
        out_shape=jax.ShapeDtypeStruct((2, nbatch, nch, w1_ab.shape[2]), F32),
        grid=(2, nbatch),
        in_specs=[
            pl.BlockSpec((1, 1, rows, LANES), lambda s, b: (s, b, 0, 0)),
            pl.BlockSpec((1, 2, pos_ab.shape[2]), lambda s, b: (s, 0, 0)),
            pl.BlockSpec((1,) + w1_ab.shape[1:], lambda s, b: (s, 0, 0)),
        ],
        out_specs=pl.BlockSpec((1, 1, nch, w1_ab.shape[2]), lambda s, b: (s, b, 0, 0)),
        compiler_params=_cparams("parallel", "parallel"),
        name="cmp_rows",
    )(x4, pos_ab, w1_ab)


PAGES_PER_STEP = 64


def _cmp_pages_kernel(pt_ref, *refs, pps):
    del pt_ref
    page_refs = refs[:pps]
    pos_ref, w_ref, p_ref, rows_ref = refs[pps:]
    for k, r in enumerate(page_refs):
        page_t = jnp.concatenate([r[0, 0, g] for g in range(N_KV)], axis=0)
        rows_ref[k * PAGE_SIZE:(k + 1) * PAGE_SIZE, :] = page_t.T
    _cmp_partial(_chunk_rows(rows_ref, pps * PAGE_SIZE), pos_ref, w_ref, p_ref)


def _cmp_pages(cache_t, page_table_flat, n_pages, pos_ab, w1_ab, nbatch):
    pps = min(PAGES_PER_STEP, n_pages)
    assert n_pages % pps == 0
    steps = n_pages // pps
    nch = pps * PAGE_SIZE // CMP_STRIDE

    def page_spec(k):
        return pl.BlockSpec(
            (1, 1, N_KV, HEAD_DIM, PAGE_SIZE),
            lambda s, b, j, pt: (s, pt[b * n_pages + j * pps + k], 0, 0, 0))

    grid_spec = pltpu.PrefetchScalarGridSpec(
        num_scalar_prefetch=1,
        grid=(2, nbatch, steps),
        in_specs=[page_spec(k) for k in range(pps)] + [
            pl.BlockSpec((1, 2, pos_ab.shape[2]), lambda s, b, j, pt: (s, 0, 0)),
            pl.BlockSpec((1,) + w1_ab.shape[1:], lambda s, b, j, pt: (s, 0, 0)),
        ],
        out_specs=pl.BlockSpec((1, 1, nch, w1_ab.shape[2]), lambda s, b, j, pt: (s, b, j, 0)),
        scratch_shapes=[pltpu.VMEM((pps * PAGE_SIZE, LANES), F32)],
    )
    return pl.pallas_call(
        functools.partial(_cmp_pages_kernel, pps=pps),
        out_shape=jax.ShapeDtypeStruct((2, nbatch, steps * nch, w1_ab.shape[2]), F32),
        grid_spec=grid_spec,
        compiler_params=_cparams("parallel", "parallel", "parallel"),
        name="cmp_pages",
    )(page_table_flat, *([cache_t] * pps), pos_ab, w1_ab)


def _gelu_tanh(x):
    return x * (0.5 * (1.0 + jnp.tanh(np.sqrt(2.0 / np.pi).astype(np.float32) * (x + 0.044715 * (x * x * x)))))


def _cmp_combine_kernel(*refs, n_parts):
    p_refs = refs[:n_parts]
    w2_ref, w2t_ref, o_ref, ot_ref, pa_ref, pb_ref = refs[n_parts:]
    hid2 = pa_ref.shape[1]
    n_p = pa_ref.shape[0]
    r0 = 0
    for p_ref in p_refs:
        rows = p_ref.shape[2]
        pa_ref[r0:r0 + rows, :] = p_ref[0, 0, :, 0:hid2]
        pb_ref[r0:r0 + rows, :] = p_ref[0, 0, :, hid2:]
        r0 += rows
    pb_ref[n_p:n_p + 8, :] = jnp.zeros((8, hid2), F32)
    hsum = pa_ref[...] + pb_ref[1:n_p + 1, :]
    act = _gelu_tanh(hsum).astype(BF16)
    o_ref[0, 0] = _dot(act, w2_ref[0]).astype(o_ref.dtype)
    ot_ref[0, 0] = _dot_nt(w2t_ref[0], act).astype(ot_ref.dtype)


def _cmp_combine(parts, w2_bd):
    nbatch, width = parts[0].shape[1], parts[0].shape[3]
    n_p = sum(p.shape[2] for p in parts)
    w2t_bd = w2_bd.transpose(0, 2, 1)
    return pl.pallas_call(
        functools.partial(_cmp_combine_kernel, n_parts=len(parts)),
        out_shape=(jax.ShapeDtypeStruct((2, nbatch, n_p, LANES), BF16),
                   jax.ShapeDtypeStruct((2, nbatch, LANES, n_p), BF16)),
        grid=(2, nbatch),
        in_specs=[pl.BlockSpec((1, 1, p.shape[2], width), lambda s, b: (s, b, 0, 0)) for p in parts] + [
            pl.BlockSpec((1,) + w2_bd.shape[1:], lambda s, b: (s, 0, 0)),
            pl.BlockSpec((1,) + w2t_bd.shape[1:], lambda s, b: (s, 0, 0)),
        ],
        out_specs=(pl.BlockSpec((1, 1, n_p, LANES), lambda s, b: (s, b, 0, 0)),
                   pl.BlockSpec((1, 1, LANES, n_p), lambda s, b: (s, b, 0, 0))),
        scratch_shapes=[pltpu.VMEM((n_p, width // 2), F32), pltpu.VMEM((n_p + 8, width // 2), F32)],
        compiler_params=_cparams("parallel", "parallel"),
        name="cmp_combine",
    )(*parts, w2_bd, w2t_bd)


def _cmp_weights(cmp_pos, cmp_w1, cmp_w2):
    hid = cmp_w1.shape[2]
    half = CMP_STRIDE * HEAD_DIM
    eye = jnp.eye(N_KV, dtype=F32)
    pos_ab = jnp.tile(cmp_pos.reshape(2, 2, CMP_STRIDE, 1, HEAD_DIM), (1, 1, 1, N_KV, 1))
    pos_ab = pos_ab.reshape(2, 2, CMP_STRIDE * N_KV * HEAD_DIM)
    w1 = cmp_w1.reshape(2, 2, CMP_STRIDE, HEAD_DIM, hid)
    w1_bd = jnp.einsum("shrdj,gk->shrgdkj", w1, eye)
    w1_bd = w1_bd.reshape(2, 2, CMP_STRIDE * N_KV * HEAD_DIM, N_KV * hid)
    w1_ab = jnp.concatenate([w1_bd[:, 0], w1_bd[:, 1]], axis=2).astype(BF16)
    w2_bd = jnp.einsum("sjd,gk->sgjkd", cmp_w2, eye).reshape(2, N_KV * hid, N_KV * HEAD_DIM).astype(BF16)
    del half
    return pos_ab, w1_ab, w2_bd


def _online_update(s, v_aug_t, state):
    tq, ck = s.shape
    m_new = jnp.broadcast_to(jnp.max(s, axis=-1, keepdims=True), (tq, LANES))
    if state is not None:
        acc_old, m_old = state
        m_new = jnp.maximum(m_old, m_new)
    p = jnp.exp(s - pltpu.repeat(m_new, ck // LANES, axis=1)).astype(BF16)
    acc = _dot_nt(p, v_aug_t)
    if state is not None:
        acc = jnp.exp(m_old - m_new) * acc_old + acc
    return acc, m_new


def _attn_prompt_kernel(q_ref, cmpt_ref, kst_ref, vst_ref, kwt_ref, vwt_ref, gate_ref, mt_ref, e_ref,
                        o_ref, kaug_s, vaug_s, vwaug_s, score_s, cnt_s, *, tq, nblk):
    i = pl.program_id(1)
    s0 = i * tq
    seq = kst_ref.shape[4]
    ncmp = cmpt_ref.shape[3]

    @pl.when(i == 0)
    def _():
        ones_row = (lax.broadcasted_iota(jnp.int32, (HEAD_DIM, seq), 0) == 0).astype(BF16)
        for g in range(N_KV):
            kaug_s[g, 0:LANES, :] = e_ref[...]
            kaug_s[g, LANES:LANES + HEAD_DIM, :] = kst_ref[0, 0, g]
            vaug_s[g, 0:HEAD_DIM, :] = vst_ref[0, 0, g]
            vaug_s[g, HEAD_DIM:, :] = ones_row
            vwaug_s[g, 0:HEAD_DIM, :] = vwt_ref[0, 0, g]
            vwaug_s[g, HEAD_DIM:, :] = ones_row

    row = lax.broadcasted_iota(jnp.int32, (tq, 1), 0)
    col = lax.broadcasted_iota(jnp.int32, (1, tq), 1)
    qpos = s0 + row
    gates = gate_ref[...]
    c_end = lax.broadcasted_iota(jnp.int32, (1, ncmp), 1) * CMP_STRIDE + (CMP_BLOCK - 1)
    mask_c = c_end <= qpos
    qp_l = s0 + col
    cur = qp_l // SEL_BLOCK
    n_rb = -(-nblk // 8)
    qs = [q_ref[:, HEAD_DIM * hh:HEAD_DIM * (hh + 1)] for hh in range(N_HEADS)]

    o_cmp, q_aug = [], []
    for g in range(N_KV):
        kct_g = cmpt_ref[0, 0, HEAD_DIM * g:HEAD_DIM * (g + 1), :]
        vct_g = cmpt_ref[1, 0, HEAD_DIM * g:HEAD_DIM * (g + 1), :]
        p_sum = jnp.zeros((tq, ncmp), F32)
        for qh in qs[GROUP_R * g:GROUP_R * (g + 1)]:
            s = jnp.where(mask_c, _dot(qh, kct_g), -jnp.inf)
            m = jnp.max(s, axis=-1, keepdims=True)
            m = jnp.where(m == -jnp.inf, 0.0, m)
            e = jnp.exp(s - m)
            d = jnp.sum(e, axis=-1, keepdims=True)
            p = e / jnp.where(d > 0, d, 1.0)
            p_sum = p_sum + p
            o_cmp.append(_dot_nt(p.astype(BF16), vct_g))

        p_hi = p_sum.astype(BF16)
        p_lo = (p_sum - p_hi.astype(F32)).astype(BF16)
        imp = _dot_nt(mt_ref[...], p_hi) + _dot_nt(mt_ref[...], p_lo)
        valid = []
        for r in range(n_rb):
            jb = 8 * r + lax.broadcasted_iota(jnp.int32, (8, 1), 0)
            ok = (jb * SEL_BLOCK <= qp_l) & (jb < nblk)
            forced = (jb == 0) | (jb == cur) | (jb == cur - 1)
            valid.append(ok)
            score_s[8 * r:8 * r + 8, :] = jnp.where(ok, jnp.where(forced, jnp.inf, imp[8 * r:8 * r + 8, :]), -jnp.inf)
        cnt_s[...] = jnp.zeros(cnt_s.shape, jnp.int32)
        rb_last = (s0 + tq - 1) // (8 * SEL_BLOCK)
        for jr in range(n_rb):
            for r in range(n_rb):
                @pl.when((jr <= rb_last) & (r <= rb_last))
                def _(jr=jr, r=r):
                    tgt = score_s[8 * r:8 * r + 8, :]
                    acc_cnt = cnt_s[8 * r:8 * r + 8, :]
                    for j in range(8 * jr, min(8 * jr + 8, nblk)):
                        rj = jnp.broadcast_to(score_s[j:j + 1, :], (8, tq))
                        if 8 * r > j:
                            beats = rj >= tgt
                        elif 8 * r + 7 < j:
                            beats = rj > tgt
                        else:
                            later = 8 * r + lax.broadcasted_iota(jnp.int32, (8, 1), 0) > j
                            beats = (rj > tgt) | ((rj == tgt) & later)
                        acc_cnt = acc_cnt + jnp.where(beats, 1, 0)
                    cnt_s[8 * r:8 * r + 8, :] = acc_cnt
        cnt = [cnt_s[8 * r:8 * r + 8, :] for r in range(n_rb)]
        selneg_t = [jnp.where((cnt[r] < N_SEL) & valid[r], 0.0, NEG) for r in range(n_rb)]
        selneg_t.append(jnp.zeros((LANES - 8 * n_rb, tq), F32))
        selneg = jnp.concatenate(selneg_t, axis=0).T.astype(BF16)
        q_aug += [jnp.concatenate([selneg, qh], axis=1) for qh in qs[GROUP_R * g:GROUP_R * (g + 1)]]

    ck = 2 * tq

    def sel_chunk(c0, state, keep):
        out = []
        for g in range(N_KV):
            k_aug = kaug_s[g, :, pl.ds(c0, ck)]
            v_aug = vaug_s[g, :, pl.ds(c0, ck)]
            for hh in range(GROUP_R * g, GROUP_R * (g + 1)):
                s = _dot(q_aug[hh], k_aug)
                if keep is not None:
                    s = jnp.where(keep, s, NEG)
                out.append(_online_update(s, v_aug, None if state is None else state[hh]))
        return tuple(out)

    n_full = i // 2
    last0 = pl.multiple_of(jnp.maximum(i - 1, 0) * tq, tq)
    kpos = last0 + lax.broadcasted_iota(jnp.int32, (1, ck), 1)
    state = sel_chunk(last0, None, (kpos <= qpos) & (kpos >= n_full * ck))
    state = lax.fori_loop(0, n_full, lambda c, st: sel_chunk(pl.multiple_of(c * ck, ck), st, None), state)
    o_sel = [acc[:, 0:HEAD_DIM] / acc[:, HEAD_DIM:HEAD_DIM + 1] for acc, _ in state]

    keep_w = (kpos <= qpos) & (kpos > qpos - WINDOW)
    wstate = []
    for g in range(N_KV):
        k_t = kwt_ref[0, 0, g, :, pl.ds(last0, ck)]
        v_aug = vwaug_s[g, :, pl.ds(last0, ck)]
        for hh in range(GROUP_R * g, GROUP_R * (g + 1)):
            wstate.append(_online_update(jnp.where(keep_w, _dot(qs[hh], k_t), NEG), v_aug, None))
    for hh in range(N_HEADS):
        acc_w = wstate[hh][0]
        o_win = acc_w[:, 0:HEAD_DIM] / acc_w[:, HEAD_DIM:HEAD_DIM + 1]
        o = (o_cmp[hh] * gates[:, 3 * hh:3 * hh + 1] + o_sel[hh] * gates[:, 3 * hh + 1:3 * hh + 2]
             + o_win * gates[:, 3 * hh + 2:3 * hh + 3])
        o_ref[:, HEAD_DIM * hh:HEAD_DIM * (hh + 1)] = o.astype(o_ref.dtype)


def _attn_prompt(q, cmp_t, kvbt, gates, mt, e_mat, nbatch, seq, tq):
    assert tq >= WINDOW and seq >= 2 * tq
    n = q.shape[0]
    nq = seq // tq
    ncmp = cmp_t.shape[3]
    kv_spec = lambda slot: pl.BlockSpec((1, 1, N_KV, HEAD_DIM, seq), lambda b, i: (slot, b, 0, 0, 0))
    return pl.pallas_call(
        functools.partial(_attn_prompt_kernel, tq=tq, nblk=seq // SEL_BLOCK),
        out_shape=jax.ShapeDtypeStruct((n, N_HEADS * HEAD_DIM), BF16),
        grid=(nbatch, nq),
        in_specs=[
            pl.BlockSpec((tq, N_HEADS * HEAD_DIM), lambda b, i: (b * nq + i, 0)),
            pl.BlockSpec((2, 1, LANES, ncmp), lambda b, i: (0, b, 0, 0)),
            kv_spec(0), kv_spec(1), kv_spec(2), kv_spec(3),
            pl.BlockSpec((tq, LANES), lambda b, i: (b * nq + i, 0)),
            pl.BlockSpec(mt.shape, lambda b, i: (0, 0)),
            pl.BlockSpec(e_mat.shape, lambda b, i: (0, 0)),
        ],
        out_specs=pl.BlockSpec((tq, N_HEADS * HEAD_DIM), lambda b, i: (b * nq + i, 0)),
        scratch_shapes=[
            pltpu.VMEM((N_KV, LANES + HEAD_DIM, seq), BF16),
            pltpu.VMEM((N_KV, LANES, seq), BF16),
            pltpu.VMEM((N_KV, LANES, seq), BF16),
            pltpu.VMEM((8 * (-(-(seq // SEL_BLOCK) // 8)), tq), F32),
            pltpu.VMEM((8 * (-(-(seq // SEL_BLOCK) // 8)), tq), jnp.int32),
        ],
        compiler_params=_cparams("arbitrary", "arbitrary"),
        name="attn_prompt",
    )(q, cmp_t, kvbt, kvbt, kvbt, kvbt, gates, mt, e_mat)


def _imp_matrix(ncmp, nblk_pad):
    per = SEL_BLOCK // CMP_STRIDE
    pad = CMP_BLOCK // CMP_STRIDE - 1
    n = np.arange(ncmp)[:, None]
    j = np.arange(nblk_pad)[None, :]
    return ((n >= per * j - pad) & (n <= per * j + per - 1)).astype(np.float32)


def _attn_sample_cmp_kernel(q_ref, kc_ref, vc_ref, m_ref, oc_ref, idx_ref, *, n_valid_cmp, qpos0, nblk):
    ncmp = kc_ref.shape[2]
    rows = q_ref.shape[2]
    t_row = lax.broadcasted_iota(jnp.int32, (rows, 1), 0) % 8
    c_idx = lax.broadcasted_iota(jnp.int32, (1, ncmp), 1)
    mask_c = (c_idx * CMP_STRIDE + (CMP_BLOCK - 1) <= qpos0 + t_row) & (c_idx < n_valid_cmp)
    p_tok = []
    for g in range(N_KV):
        gs = slice(HEAD_DIM * g, HEAD_DIM * (g + 1))
        s = jnp.where(mask_c, _dot_nt(q_ref[0, g], kc_ref[0, 0, :, gs]), -jnp.inf)
        m = jnp.max(s, axis=-1, keepdims=True)
        m = jnp.where(m == -jnp.inf, 0.0, m)
        e = jnp.exp(s - m)
        d = jnp.sum(e, axis=-1, keepdims=True)
        p = e / jnp.where(d > 0, d, 1.0)
        oc_ref[0, g] = _dot(p.astype(BF16), vc_ref[0, 0, :, gs])
        p_tok.append(jnp.sum(p.reshape(GROUP_R, 8, ncmp), axis=0))
    p_all = jnp.concatenate(p_tok, axis=0)
    p_hi = p_all.astype(BF16)
    p_lo = (p_all - p_hi.astype(F32)).astype(BF16)
    imp = _dot(p_hi, m_ref[...]) + _dot(p_lo, m_ref[...])
    nb_pad = imp.shape[1]
    jb = lax.broadcasted_iota(jnp.int32, (1, nb_pad), 1)
    qpos = qpos0 + lax.broadcasted_iota(jnp.int32, (2 * 8, 1), 0) % 8
    cur = qpos // SEL_BLOCK
    valid = (jb * SEL_BLOCK <= qpos) & (jb < nblk)
    forced = (jb == 0) | (jb == cur) | (jb == cur - 1)
    score = jnp.where(valid, jnp.where(forced, jnp.inf, imp), -jnp.inf)
    avail = jb < nblk
    lane = lax.broadcasted_iota(jnp.int32, (2 * 8, LANES), 1)
    picked = jnp.zeros((2 * 8, LANES), jnp.int32)
    for k in range(N_SEL):
        best = jnp.max(jnp.where(avail, score, -jnp.inf), axis=-1, keepdims=True)
        cand = avail & (score == best)
        idx = jnp.min(jnp.where(cand, jb, nb_pad), axis=-1, keepdims=True)
        picked = jnp.where(lane == k, idx, picked)
        avail = avail & (jb != idx)
    idx_ref[0] = picked


def _attn_sample_cmp(qg, cmp_kv, m_mat, n_valid_cmp, qpos0, nblk):
    nbatch = qg.shape[0]
    ncmp = cmp_kv.shape[2]
    rows = qg.shape[2]
    cmp_spec = lambda slot: pl.BlockSpec((1, 1, ncmp, LANES), lambda b: (slot, b, 0, 0))
    return pl.pallas_call(
        functools.partial(_attn_sample_cmp_kernel, n_valid_cmp=n_valid_cmp, qpos0=qpos0, nblk=nblk),
        out_shape=(
            jax.ShapeDtypeStruct((nbatch, N_KV, rows, HEAD_DIM), F32),
            jax.ShapeDtypeStruct((nbatch, 2 * 8, LANES), jnp.int32),
        ),
        grid=(nbatch,),
        in_specs=[
            pl.BlockSpec((1, N_KV, rows, HEAD_DIM), lambda b: (b, 0, 0, 0)),
            cmp_spec(0), cmp_spec(1),
            pl.BlockSpec(m_mat.shape, lambda b: (0, 0)),
        ],
        out_specs=(
            pl.BlockSpec((1, N_KV, rows, HEAD_DIM), lambda b: (b, 0, 0, 0)),
            pl.BlockSpec((1, 2 * 8, LANES), lambda b: (b, 0, 0)),
        ),
        compiler_params=_cparams("parallel"),
        name="attn_sample_cmp",
    )(qg, cmp_kv, cmp_kv, m_mat)


def _masked_softmax(s, mask):
    s = jnp.where(mask, s, -jnp.inf)
    m = jnp.max(s, axis=-1, keepdims=True)
    m = jnp.where(m == -jnp.inf, 0.0, m)
    e = jnp.exp(s - m)
    d = jnp.sum(e, axis=-1, keepdims=True)
    return e / jnp.where(d > 0, d, 1.0)


def _attn_sample_sel_kernel(idx_ref, pt_ref, *refs, dec_seq, qpos0, n_cache_blk, wb):
    del pt_ref
    nsel = N_KV * N_SEL
    per_page = PAGE_SIZE // SEL_BLOCK
    kv_refs = refs[:nsel]
    q_ref, tail_ref, kw_ref, vw_ref, wnew_ref, oc_ref, gate_ref, o_ref, osel_ref = refs[nsel:]
    b = pl.program_id(0)
    t = pl.program_id(1)
    rows = q_ref.shape[2]
    t_row = lax.broadcasted_iota(jnp.int32, (rows, 1), 0) % 8
    qpos = qpos0 + t_row
    colk = lax.broadcasted_iota(jnp.int32, (1, N_SEL * PAGE_SIZE), 1)
    slot_of_col = colk // PAGE_SIZE
    row_in_page = colk % PAGE_SIZE

    @pl.when(t == 0)
    def _():
        osel_ref[...] = jnp.zeros(osel_ref.shape, F32)

    for g in range(N_KV):
        blk_of_col = jnp.zeros_like(colk)
        k_pages, v_pages = [], []
        for k in range(N_SEL):
            blk = idx_ref[((b * dec_seq + t) * N_KV + g) * N_SEL + k]
            is_new = blk >= n_cache_blk
            k_pages.append(jnp.where(is_new, tail_ref[0, 0, g], kv_refs[g * N_SEL + k][0, 0, 0]))
            v_pages.append(jnp.where(is_new, tail_ref[1, 0, g], kv_refs[g * N_SEL + k][1, 0, 0]))
            blk_of_col = blk_of_col + jnp.where(slot_of_col == k, blk, 0)
        in_block = row_in_page // SEL_BLOCK == blk_of_col % per_page
        kpos = blk_of_col * SEL_BLOCK + row_in_page % SEL_BLOCK
        k_t = jnp.concatenate(k_pages, axis=1).astype(BF16)
        v_t = jnp.concatenate(v_pages, axis=1).astype(BF16)
        p = _masked_softmax(_dot(q_ref[0, g], k_t), in_block & (kpos <= qpos))
        o = _dot_nt(p.astype(BF16), v_t)
        osel_ref[g] = osel_ref[g] + jnp.where(t_row == t, o, 0.0)

    @pl.when(t == dec_seq - 1)
    def _():
        colw = lax.broadcasted_iota(jnp.int32, (1, wb + LANES), 1)
        kwpos = jnp.where(colw < wb, qpos0 - wb + colw, qpos0 + colw - wb)
        dpos = qpos - kwpos
        mask_w = (dpos >= 0) & (dpos < WINDOW) & (kwpos >= 0) & (colw < wb + dec_seq)
        for g in range(N_KV):
            k_t = jnp.concatenate([kw_ref[0, 0, g], wnew_ref[0, 0, g]], axis=1).astype(BF16)
            v_t = jnp.concatenate([vw_ref[0, 0, g], wnew_ref[1, 0, g]], axis=1).astype(BF16)
            p = _masked_softmax(_dot(q_ref[0, g], k_t), mask_w)
            o_win = _dot_nt(p.astype(BF16), v_t)
            gt = gate_ref[0, g]
            o_ref[0, g] = oc_ref[0, g] * gt[:, 0:1] + osel_ref[g] * gt[:, 1:2] + o_win * gt[:, 2:3]


def _attn_sample_sel(idx_flat, pt_flat, cache_t, qg, tail_t, win_t, wnew_t, o_cmp, gates_g, dec_seq, n_pages,
                     qpos0):
    nbatch = qg.shape[0]
    rows = qg.shape[2]
    wb = win_t.shape[4]
    per_page = PAGE_SIZE // SEL_BLOCK
    n_cache_blk = n_pages * per_page

    def page_spec(g, k):
        def imap(b, t, idx, pt):
            blk = jnp.minimum(idx[((b * dec_seq + t) * N_KV + g) * N_SEL + k], n_cache_blk - 1)
            return (1, pt[b * n_pages + blk // per_page], g, 0, 0)
        return pl.BlockSpec((2, 1, 1, HEAD_DIM, PAGE_SIZE), imap)

    kv_specs = [page_spec(g, k) for g in range(N_KV) for k in range(N_SEL)]
    per_b = lambda shape: pl.BlockSpec((1,) + shape, lambda b, t, idx, pt: (b,) + (0,) * len(shape))
    new_spec = pl.BlockSpec((2, 1, N_KV, HEAD_DIM, LANES), lambda b, t, idx, pt: (0, b, 0, 0, 0))
    win_spec = lambda s: pl.BlockSpec((1, 1, N_KV, HEAD_DIM, wb), lambda b, t, idx, pt: (s, b, 0, 0, 0))
    grid_spec = pltpu.PrefetchScalarGridSpec(
        num_scalar_prefetch=2,
        grid=(nbatch, dec_seq),
        in_specs=kv_specs + [
            per_b((N_KV, rows, HEAD_DIM)),
            new_spec, win_spec(0), win_spec(1), new_spec,
            per_b((N_KV, rows, HEAD_DIM)),
            per_b((N_KV, rows, LANES)),
        ],
        out_specs=per_b((N_KV, rows, HEAD_DIM)),
        scratch_shapes=[pltpu.VMEM((N_KV, rows, HEAD_DIM), F32)],
    )
    n_page_specs = N_KV * N_SEL
    return pl.pallas_call(
        functools.partial(_attn_sample_sel_kernel, dec_seq=dec_seq, qpos0=qpos0, n_cache_blk=n_cache_blk, wb=wb),
        out_shape=jax.ShapeDtypeStruct((nbatch, N_KV, rows, HEAD_DIM), F32),
        grid_spec=grid_spec,
        compiler_params=_cparams("parallel", "arbitrary"),
        name="attn_sample_sel",
    )(idx_flat, pt_flat, *([cache_t] * n_page_specs), qg, tail_t, win_t, win_t, wnew_t, o_cmp, gates_g)


FFN_CHUNK = 256


def _post_kernel(*refs, n_a, final):
    y_ref = refs[0]
    a_refs = refs[1:1 + 2 * n_a]
    gmix_ref, g2_ref, sh_ref, sc_ref, gffn_ref, w1_ref, w2_ref = refs[1 + 2 * n_a:8 + 2 * n_a]
    rest = refs[8 + 2 * n_a:]
    out_ref = rest[-1]
    mix = _dot(a_refs[0][...], a_refs[1][...])
    for k in range(1, n_a):
        mix = mix + _dot(a_refs[2 * k][...], a_refs[2 * k + 1][...])
    y1 = y_ref[...] + gmix_ref[0] * mix
    h = _modulate(y1, g2_ref[...], sh_ref[0], sc_ref[0]).astype(BF16)
    d_ff = w2_ref.shape[0]
    acc = jnp.zeros(y1.shape, F32)
    for c in range(d_ff // FFN_CHUNK):
        c0 = c * FFN_CHUNK
        gate = _dot(h, w1_ref[:, c0:c0 + FFN_CHUNK])
        up = _dot(h, w1_ref[:, d_ff + c0:d_ff + c0 + FFN_CHUNK])
        acc = acc + _dot((_silu(gate) * up).astype(BF16), w2_ref[c0:c0 + FFN_CHUNK, :])
    y2 = y1 + gffn_ref[0] * acc
    if final:
        fg_ref = rest[0]
        ms = jnp.mean(y2 * y2, axis=-1, keepdims=True)
        y2 = (y2 * lax.rsqrt(ms + NORM_EPS)) * fg_ref[...]
    out_ref[...] = y2


def _post(y, a_list, wo_list, gmix, g2, shift, scale, gffn, w1, w2, final_g, tm, rows_per_mod):
    n, d = y.shape
    r = shift.shape[1]
    mod_spec = pl.BlockSpec((1, r, d), lambda i: (i // rows_per_mod, 0, 0))
    const = lambda arr: pl.BlockSpec(arr.shape, lambda i: (0,) * arr.ndim, pipeline_mode=pl.Buffered(1))
    in_specs = [pl.BlockSpec((tm, d), lambda i: (i, 0))]
    args = [y]
    for a, wo in zip(a_list, wo_list):
        in_specs += [pl.BlockSpec((tm, a.shape[1]), lambda i: (i, 0)), const(wo)]
        args += [a, wo]
    in_specs += [mod_spec, pl.BlockSpec((1, d), lambda i: (0, 0)), mod_spec, mod_spec, mod_spec, const(w1), const(w2)]
    args += [gmix, g2, shift, scale, gffn, w1, w2]
    if final_g is not None:
        in_specs.append(pl.BlockSpec((1, d), lambda i: (0, 0)))
        args.append(final_g)
    return pl.pallas_call(
        functools.partial(_post_kernel, n_a=len(a_list), final=final_g is not None),
        out_shape=jax.ShapeDtypeStruct((n, d), F32),
        grid=(n // tm,),
        in_specs=in_specs,
        out_specs=pl.BlockSpec((tm, d), lambda i: (i, 0)),
        compiler_params=_cparams("parallel"),
        name="post_proj_ffn",
    )(*args)


def _pre1_kernel(x_ref, g_ref, sh_ref, sc_ref, w_ref, u_ref):
    h = _modulate(x_ref[...], g_ref[...], sh_ref[0], sc_ref[0]).astype(BF16)
    d = u_ref.shape[1]
    a = _dot(h, w_ref[:, 0:d])
    b = _dot(h, w_ref[:, d:2 * d])
    u_ref[...] = a * _sigmoid(b)


def _pre1(x2, g, shift, scale, pw1, tm, rows_per_mod):
    n, d = x2.shape
    r = shift.shape[1]
    mod_spec = pl.BlockSpec((1, r, d), lambda i: (i // rows_per_mod, 0, 0))
    return pl.pallas_call(
        _pre1_kernel,
        out_shape=jax.ShapeDtypeStruct((n, d), F32),
        grid=(n // tm,),
        in_specs=[
            pl.BlockSpec((tm, d), lambda i: (i, 0)),
            pl.BlockSpec((1, d), lambda i: (0, 0)),
            mod_spec, mod_spec,
            pl.BlockSpec(pw1.shape, lambda i: (0, 0)),
        ],
        out_specs=pl.BlockSpec((tm, d), lambda i: (i, 0)),
        compiler_params=_cparams("parallel"),
        name="pre1_pw_glu",
    )(x2, g, shift, scale, pw1)


def _conv_body(ext_ref, t_rows, dw_ref, dwb_ref, lng_ref, lnb_ref, out_ref):
    off = CONV_PAD - CONV_HIST
    acc = None
    for b in range(8):
        span = t_rows + (8 if b else 0)
        part = None
        for k in range(CONV_WIDTH):
            if (off + k) % 8 != b:
                continue
            base = off + k - b
            term = ext_ref[base:base + span, :] * dw_ref[k:k + 1, :]
            part = term if part is None else part + term
        part = part[b:b + t_rows, :]
        acc = part if acc is None else acc + part
    y = acc + dwb_ref[...]
    mu = jnp.mean(y, axis=-1, keepdims=True)
    var = jnp.mean(jnp.square(y - mu), axis=-1, keepdims=True)
    z = (y - mu) * lax.rsqrt(var + NORM_EPS) * lng_ref[...] + lnb_ref[...]
    out_ref[...] = _silu(z).astype(out_ref.dtype)


def _pre1_conv_kernel(x_ref, g_ref, sh_ref, sc_ref, w_ref, dw_ref, dwb_ref, lng_ref, lnb_ref,
                      out_ref, state_ref, ext_ref, *, tiles_per_batch, tm):
    t_in_b = pl.program_id(0) % tiles_per_batch

    @pl.when(t_in_b == 0)
    def _():
        ext_ref[0:CONV_PAD, :] = jnp.zeros((CONV_PAD, ext_ref.shape[1]), F32)

    @pl.when(t_in_b > 0)
    def _():
        ext_ref[0:CONV_PAD, :] = ext_ref[tm:tm + CONV_PAD, :]

    h = _modulate(x_ref[...], g_ref[...], sh_ref[0], sc_ref[0]).astype(BF16)
    d = out_ref.shape[1]
    a = _dot(h, w_ref[:, 0:d])
    b = _dot(h, w_ref[:, d:2 * d])
    ext_ref[CONV_PAD:, :] = a * _sigmoid(b)
    _conv_body(ext_ref, tm, dw_ref, dwb_ref, lng_ref, lnb_ref, out_ref)
    state_ref[0] = ext_ref[tm:tm + CONV_PAD, :]


def _pre1_conv_prompt(x2, g, shift, scale, pw1, dw, dwb, lng, lnb, seq, tm):
    assert tm >= CONV_PAD
    n, d = x2.shape
    tpb = seq // tm
    mod_spec = pl.BlockSpec((1, 1, d), lambda i: (i // tpb, 0, 0))
    vec = pl.BlockSpec((1, d), lambda i: (0, 0))
    return pl.pallas_call(
        functools.partial(_pre1_conv_kernel, tiles_per_batch=tpb, tm=tm),
        out_shape=(jax.ShapeDtypeStruct((n, d), BF16), jax.ShapeDtypeStruct((n // seq, CONV_PAD, d), F32)),
        grid=(n // tm,),
        in_specs=[
            pl.BlockSpec((tm, d), lambda i: (i, 0)),
            vec, mod_spec, mod_spec,
            pl.BlockSpec(pw1.shape, lambda i: (0, 0)),
            pl.BlockSpec(dw.shape, lambda i: (0, 0)),
            vec, vec, vec,
        ],
        out_specs=(pl.BlockSpec((tm, d), lambda i: (i, 0)),
                   pl.BlockSpec((1, CONV_PAD, d), lambda i: (i // tpb, 0, 0))),
        scratch_shapes=[pltpu.VMEM((CONV_PAD + tm, d), F32)],
        compiler_params=_cparams("arbitrary"),
        name="pre1_conv_prompt",
    )(x2, g, shift, scale, pw1, dw, dwb, lng, lnb)


def _conv_sample_kernel(ext_ref, dw_ref, dwb_ref, lng_ref, lnb_ref, out_ref, *, t_rows):
    _conv_body(ext_ref.at[0], t_rows, dw_ref, dwb_ref, lng_ref, lnb_ref, out_ref.at[0])


def _conv_sample(ext, dw, dwb, lng, lnb):
    b, rows, d = ext.shape
    t_rows = rows - CONV_PAD
    vec = pl.BlockSpec((1, d), lambda i: (0, 0))
    return pl.pallas_call(
        functools.partial(_conv_sample_kernel, t_rows=t_rows),
        out_shape=jax.ShapeDtypeStruct((b, t_rows, d), BF16),
        grid=(b,),
        in_specs=[
            pl.BlockSpec((1, rows, d), lambda i: (i, 0, 0)),
            pl.BlockSpec(dw.shape, lambda i: (0, 0)),
            vec, vec, vec,
        ],
        out_specs=pl.BlockSpec((1, t_rows, d), lambda i: (i, 0, 0)),
        compiler_params=_cparams("parallel"),
        name="conv_sample",
    )(ext, dw, dwb, lng, lnb)


ROW_TILE = 512
ATTN_TILE = 512


def _group_rows(x, nbatch, dec_seq):
    w = x.shape[1] // N_HEADS
    x = x.reshape(nbatch, dec_seq, N_KV, GROUP_R, w).transpose(0, 2, 3, 1, 4)
    x = jnp.pad(x, ((0, 0), (0, 0), (0, 0), (0, 8 - dec_seq), (0, 0)))
    return x.reshape(nbatch, N_KV, GROUP_R * 8, w)


def _ungroup_rows(x, nbatch, dec_seq):
    w = x.shape[3]
    x = x.reshape(nbatch, N_KV, GROUP_R, 8, w)[:, :, :, :dec_seq]
    return x.transpose(0, 3, 1, 2, 4).reshape(nbatch * dec_seq, N_HEADS * w)


def kernel(x_prompt, x_sample, cache_kv, page_table, cache_win, state_pool, state_conv, c_prompt, c_sample,
           ada_w, ada_b, norm_g, attn_w_in, attn_w_out, pool_w, pool_scale, cmp_pos, cmp_w1, cmp_w2,
           conv_pw1, conv_dw, conv_dw_b, conv_ln_g, conv_ln_b, conv_pw2, ffn_w1, ffn_w2, final_g):
    nb_p, seq, d = x_prompt.shape
    nb_s, dec_seq, _ = x_sample.shape
    n_p, n_s = nb_p * seq, nb_s * dec_seq
    n_pages = page_table.shape[1]
    past = n_pages * PAGE_SIZE
    wb = cache_win.shape[3]
    tm = min(ROW_TILE, seq)
    tq = min(ATTN_TILE, seq)
    pool_width = pool_w.shape[1] * pool_w.shape[2]
    att_width = N_HEADS * HEAD_DIM

    mods = _ada(jnp.concatenate([c_prompt, c_sample], axis=0), ada_w, ada_b)

    def mod_p(layer, k):
        return mods[layer, :nb_p, k * d:(k + 1) * d].reshape(nb_p, 1, d)

    def mod_s(layer, k):
        return jnp.repeat(mods[layer, nb_p:, k * d:(k + 1) * d], dec_seq, axis=0).reshape(1, n_s, d)

    y_p = x_prompt.reshape(n_p, d)
    y_s = x_sample.reshape(n_s, d)
    row = lambda v: v.reshape(1, -1)

    w_in = attn_w_in[0]
    w_in_pad = jnp.pad(w_in.T, ((0, (-w_in.shape[1]) % LANES), (0, 0))).astype(BF16)
    cos_p, sin_p = _rope_tables(np.arange(seq))
    cos_s, sin_s = _rope_tables(past + np.arange(n_s) % dec_seq)
    g0 = row(norm_g[0, 0])
    u_p, q_p, gate_p, kvt_p, wint_p, kvbt_p, cmp_in_p = _pre0(
        y_p, g0, mod_p(0, 0), mod_p(0, 1), w_in_pad, cos_p, sin_p, tm, seq // tm, seq // tm, seq=seq)
    u_s, q_s, gate_s, kv_s, win_s = _pre0(y_s, g0, mod_s(0, 0), mod_s(0, 1), w_in_pad, cos_s, sin_s, n_s, 1, 1)

    pool_w_b = pool_w[0].astype(BF16)
    pool_sc = row(pool_scale[0])
    ypool_p = _pool_prompt(u_p, pool_w_b, pool_sc, seq, tm)
    u_s3 = u_s.reshape(nb_s, dec_seq, pool_width)
    pool_ext = jnp.concatenate([
        jnp.zeros((nb_s, POOL_PAD - POOL_HIST, pool_width), F32), state_pool[0], u_s3,
        jnp.zeros((nb_s, 8 - dec_seq, pool_width), F32)], axis=1)
    ypool_s = _pool_sample(pool_ext, pool_w_b, pool_sc, past)[:, :dec_seq].reshape(n_s, pool_width)

    pos_ab, w1_ab, w2_bd = _cmp_weights(cmp_pos[0], cmp_w1[0], cmp_w2[0])
    cmp_in_p4 = cmp_in_p.reshape(2, nb_p, seq, LANES)
    _, cmp_t_p = _cmp_combine([_cmp_rows(cmp_in_p4, pos_ab, w1_ab, nb_p, seq)], w2_bd)
    cache_t = cache_kv[0].transpose(0, 1, 3, 4, 2)
    pt_flat = page_table.reshape(-1)
    part_past = _cmp_pages(cache_t, pt_flat, n_pages, pos_ab, w1_ab, nb_s)
    tail_rows = 2 * PAGE_SIZE
    kv_s4 = kv_s.reshape(4, nb_s, dec_seq, LANES)
    tail4 = jnp.pad(kv_s4, ((0, 0), (0, 0), (0, tail_rows - dec_seq), (0, 0)))
    part_tail = _cmp_rows(tail4, pos_ab, w1_ab, nb_s, tail_rows)
    cmp_s, _ = _cmp_combine([part_past, part_tail], w2_bd)
    total_len = past + dec_seq
    padded_len = -(-total_len // SEL_BLOCK) * SEL_BLOCK
    n_cmp_s = padded_len // CMP_STRIDE - CMP_BLOCK // CMP_STRIDE + 1
    nblk_s = padded_len // SEL_BLOCK

    assert seq // SEL_BLOCK <= LANES and dec_seq <= SEL_BLOCK
    ncmp_p = cmp_t_p.shape[3]
    mt_p = jnp.asarray(_imp_matrix(ncmp_p, LANES).T, BF16)
    e_mat = jnp.asarray((np.arange(seq)[None, :] // SEL_BLOCK == np.arange(LANES)[:, None]).astype(np.float32),
                        BF16)
    o_p = _attn_prompt(q_p, cmp_t_p, kvbt_p, gate_p, mt_p, e_mat, nb_p, seq, tq)

    qg_s = _group_rows(q_s, nb_s, dec_seq)
    nblk_pad = -(-nblk_s // LANES) * LANES
    m_s = jnp.asarray(_imp_matrix(cmp_s.shape[2], nblk_pad), BF16)
    o_cmp_s, picked = _attn_sample_cmp(qg_s, cmp_s, m_s, n_cmp_s, past, nblk_s)
    idx = picked.reshape(nb_s, N_KV, 8, LANES)[:, :, :dec_seq, :N_SEL].transpose(0, 2, 1, 3).reshape(-1)

    def new_rows_t(rows):
        x = rows.reshape(rows.shape[0], nb_s, dec_seq, N_KV, HEAD_DIM).transpose(0, 1, 3, 4, 2)
        return jnp.pad(x, ((0, 0), (0, 0), (0, 0), (0, 0), (0, LANES - dec_seq)))

    tail_t = new_rows_t(kv_s[2:4])
    wnew_t = new_rows_t(win_s)
    win_t = cache_win[0].transpose(0, 1, 3, 4, 2)
    gates_g = jnp.pad(_group_rows(gate_s[:, :3 * N_HEADS], nb_s, dec_seq), ((0, 0), (0, 0), (0, 0), (0, LANES - 3)))
    o_s = _attn_sample_sel(idx, pt_flat, cache_t, qg_s, tail_t, win_t, wnew_t, o_cmp_s, gates_g,
                           dec_seq, n_pages, past)
    o_s = _ungroup_rows(o_s, nb_s, dec_seq).astype(BF16)

    w_out = attn_w_out[0].astype(BF16)
    wo_list = [w_out[:pool_width], w_out[pool_width:]]
    g1 = row(norm_g[0, 1])
    ffn1_0, ffn2_0 = ffn_w1[0].astype(BF16), ffn_w2[0].astype(BF16)
    y_p = _post(y_p, [ypool_p, o_p], wo_list, mod_p(0, 2), g1, mod_p(0, 3), mod_p(0, 4), mod_p(0, 5),
                ffn1_0, ffn2_0, None, tm, seq // tm)
    y_s = _post(y_s, [ypool_s, o_s], wo_list, mod_s(0, 2), g1, mod_s(0, 3), mod_s(0, 4), mod_s(0, 5),
                ffn1_0, ffn2_0, None, n_s, 1)

    g0 = row(norm_g[1, 0])
    pw1 = conv_pw1[0].astype(BF16)
    dwb, lng, lnb = row(conv_dw_b[0]), row(conv_ln_g[0]), row(conv_ln_b[0])
    cv_p, uc_tail_p = _pre1_conv_prompt(y_p, g0, mod_p(1, 0), mod_p(1, 1), pw1, conv_dw[0], dwb, lng, lnb, seq, tm)
    uc_s = _pre1(y_s, g0, mod_s(1, 0), mod_s(1, 1), pw1, n_s, 1)
    uc_s3 = uc_s.reshape(nb_s, dec_seq, d)
    conv_ext = jnp.concatenate([
        jnp.zeros((nb_s, CONV_PAD - CONV_HIST, d), F32), state_conv[0], uc_s3,
        jnp.zeros((nb_s, 8 - dec_seq, d), F32)], axis=1)
    cv_s = _conv_sample(conv_ext, conv_dw[0], dwb, lng, lnb)[:, :dec_seq].reshape(n_s, d)

    pw2 = conv_pw2[0].astype(BF16)
    g1 = row(norm_g[1, 1])
    ffn1_1, ffn2_1 = ffn_w1[1].astype(BF16), ffn_w2[1].astype(BF16)
    fg = row(final_g)
    y_p = _post(y_p, [cv_p], [pw2], mod_p(1, 2), g1, mod_p(1, 3), mod_p(1, 4), mod_p(1, 5),
                ffn1_1, ffn2_1, fg, tm, seq // tm)
    y_s = _post(y_s, [cv_s], [pw2], mod_s(1, 2), g1, mod_s(1, 3), mod_s(1, 4), mod_s(1, 5),
                ffn1_1, ffn2_1, fg, n_s, 1)

    y_prompt = y_p.reshape(nb_p, seq, d)
    y_sample = y_s.reshape(nb_s, dec_seq, d)
    kv_prompt = kvt_p.transpose(0, 1, 4, 2, 3)[None]
    kv_sample = kv_s.reshape(1, 4, nb_s, dec_seq, N_KV, HEAD_DIM)
    if seq >= wb:
        win_prompt_t = wint_p[..., seq - wb:]
    else:
        win_prompt_t = jnp.pad(wint_p, ((0, 0),) * 4 + ((wb - seq, 0),))
    win_prompt = win_prompt_t.transpose(0, 1, 4, 2, 3)
    win_sample_t = jnp.concatenate([win_t, wnew_t[..., :dec_seq]], axis=-1)[..., -wb:]
    win_sample = win_sample_t.transpose(0, 1, 4, 2, 3)

    def last_rows(x3, hist, state):
        full = x3 if state is None and x3.shape[1] >= hist else jnp.concatenate(
            [jnp.zeros((x3.shape[0], hist, x3.shape[2]), F32) if state is None else state, x3], axis=1)
        return full[:, full.shape[1] - hist:]

    pool_prompt = last_rows(u_p.reshape(nb_p, seq, pool_width), POOL_HIST, None)
    pool_sample = last_rows(u_s3, POOL_HIST, state_pool[0])
    conv_prompt = uc_tail_p[:, CONV_PAD - CONV_HIST:]
    conv_sample = last_rows(uc_s3, CONV_HIST, state_conv[0])
    return (y_prompt, y_sample, kv_prompt, kv_sample, win_prompt[None], win_sample[None],
            pool_prompt[None], pool_sample[None], conv_prompt[None], conv_sample[None])
```

```python
import functools

import numpy as np
import jax
import jax.numpy as jnp
from jax import lax
from jax.experimental import pallas as pl
from jax.experimental.pallas import tpu as pltpu

F32 = jnp.float32
BF16 = jnp.bfloat16

NORM_EPS = 1e-6
N_HEADS = 8
HEAD_DIM = 64
N_KV = 2
GROUP_R = N_HEADS // N_KV
POOL_WINDOWS = (2, 4, 8, 16)
POOL_HIST = 15
POOL_PAD = 16
CMP_BLOCK = 32
CMP_STRIDE = 16
SEL_BLOCK = 64
N_SEL = 16
WINDOW = 512
PAGE_SIZE = 128
ROPE_THETA = 10000.0
CONV_WIDTH = 31
CONV_HIST = CONV_WIDTH - 1
CONV_PAD = 32
LANES = 128
NEG = -1e9
V7X_VMEM_LIMIT = 56 * 1024 * 1024


def _cparams(*sem):
    return pltpu.CompilerParams(dimension_semantics=sem, vmem_limit_bytes=V7X_VMEM_LIMIT)


def _dot(a, b):
    return jnp.dot(a, b, preferred_element_type=F32)


def _dot_nt(a, b):
    return lax.dot_general(a, b, (((1,), (1,)), ((), ())), preferred_element_type=F32)


def _sigmoid(x):
    return 1.0 / (1.0 + jnp.exp(-x))


def _silu(x):
    return x * _sigmoid(x)


def _modulate(x, g, shift, scale):
    ms = jnp.mean(x * x, axis=-1, keepdims=True)
    y = x * lax.rsqrt(ms + NORM_EPS)
    return (y * g) * (1.0 + scale) + shift


def _ada_kernel(c_ref, w_ref, b_ref, o_ref):
    a = _silu(c_ref[...]).astype(BF16)
    o_ref[0] = _dot(a, w_ref[0].astype(BF16)) + b_ref[0]


def _ada(c_all, ada_w, ada_b):
    depth, d, n6 = ada_w.shape
    bc = c_all.shape[0]
    tn = n6 // 4
    return pl.pallas_call(
        _ada_kernel,
        out_shape=jax.ShapeDtypeStruct((depth, bc, n6), F32),
        grid=(depth, n6 // tn),
        in_specs=[
            pl.BlockSpec((bc, d), lambda l, j: (0, 0)),
            pl.BlockSpec((1, d, tn), lambda l, j: (l, 0, j)),
            pl.BlockSpec((1, 1, tn), lambda l, j: (l, 0, j)),
        ],
        out_specs=pl.BlockSpec((1, bc, tn), lambda l, j: (l, 0, j)),
        compiler_params=_cparams("parallel", "parallel"),
        name="ada_mod",
    )(c_all, ada_w, ada_b.reshape(depth, 1, n6))


def _rope(x, cos, sin_signed):
    lane = lax.broadcasted_iota(jnp.int32, x.shape, 1)
    first = (lane % HEAD_DIM) < (HEAD_DIM // 2)
    swapped = jnp.where(first, pltpu.roll(x, LANES - HEAD_DIM // 2, 1), pltpu.roll(x, HEAD_DIM // 2, 1))
    return x * cos + swapped * sin_signed


def _pre0_project(x_ref, g_ref, sh_ref, sc_ref, w_ref, cos_ref, sin_ref, u_ref, q_ref, gate_ref):
    h = _modulate(x_ref[...], g_ref[...], sh_ref[0], sc_ref[0]).astype(BF16)
    cos = cos_ref[...]
    sin = sin_ref[...]
    full = _dot_nt(h, w_ref[...])

    def proj(c0, width):
        return full[:, c0:c0 + width]

    u_ref[...] = proj(0, 512)
    for j in range(4):
        qj = _rope(proj(512 + LANES * j, LANES), cos, sin)
        q_ref[:, LANES * j:LANES * (j + 1)] = (qj * (HEAD_DIM ** -0.5)).astype(BF16)
    gate_ref[...] = _sigmoid(proj(1792, LANES))
    kc = _rope(proj(1024, LANES), cos, sin)
    vc = proj(1152, LANES)
    ks = _rope(proj(1280, LANES), cos, sin)
    vs = proj(1408, LANES)
    kw = _rope(proj(1536, LANES), cos, sin)
    vw = proj(1664, LANES)
    return kc, vc, ks, vs, kw, vw


def _pre0_rows_kernel(x_ref, g_ref, sh_ref, sc_ref, w_ref, cos_ref, sin_ref,
                      u_ref, q_ref, gate_ref, kv_ref, win_ref):
    kc, vc, ks, vs, kw, vw = _pre0_project(x_ref, g_ref, sh_ref, sc_ref, w_ref, cos_ref, sin_ref,
                                           u_ref, q_ref, gate_ref)
    for k, seg in enumerate((kc, vc, ks, vs)):
        kv_ref[k] = seg
    win_ref[0] = kw
    win_ref[1] = vw


def _store_transposed(x, *targets):
    xt = x.T
    for ref, lead in targets:
        for g in range(N_KV):
            ref[lead + (g,)] = xt[HEAD_DIM * g:HEAD_DIM * (g + 1), :].astype(ref.dtype)


def _pre0_seq_kernel(x_ref, g_ref, sh_ref, sc_ref, w_ref, cos_ref, sin_ref,
                     u_ref, q_ref, gate_ref, kvt_ref, wint_ref, kvbt_ref, cmp_ref):
    kc, vc, ks, vs, kw, vw = _pre0_project(x_ref, g_ref, sh_ref, sc_ref, w_ref, cos_ref, sin_ref,
                                           u_ref, q_ref, gate_ref)
    _store_transposed(kc, (kvt_ref, (0, 0)))
    _store_transposed(vc, (kvt_ref, (1, 0)))
    _store_transposed(ks, (kvt_ref, (2, 0)), (kvbt_ref, (0, 0)))
    _store_transposed(vs, (kvt_ref, (3, 0)), (kvbt_ref, (1, 0)))
    _store_transposed(kw, (wint_ref, (0, 0)), (kvbt_ref, (2, 0)))
    _store_transposed(vw, (wint_ref, (1, 0)), (kvbt_ref, (3, 0)))
    cmp_ref[0] = kc
    cmp_ref[1] = vc


def _pre0(x2, g, shift, scale, w_pad, cos, sin, tm, rows_per_mod, pos_tiles, seq=None):
    n, d = x2.shape
    r = shift.shape[1]
    mod_spec = pl.BlockSpec((1, r, d), lambda i: (i // rows_per_mod, 0, 0))
    pos_spec = pl.BlockSpec((tm, LANES), lambda i: (i % pos_tiles, 0))
    row_spec = lambda w: pl.BlockSpec((tm, w), lambda i: (i, 0))
    out_shape = [jax.ShapeDtypeStruct((n, 512), F32), jax.ShapeDtypeStruct((n, 512), BF16),
                 jax.ShapeDtypeStruct((n, LANES), F32)]
    out_specs = [row_spec(512), row_spec(512), row_spec(LANES)]
    if seq is None:
        body = _pre0_rows_kernel
        out_shape += [jax.ShapeDtypeStruct((4, n, LANES), F32), jax.ShapeDtypeStruct((2, n, LANES), F32)]
        out_specs += [pl.BlockSpec((4, tm, LANES), lambda i: (0, i, 0)),
                      pl.BlockSpec((2, tm, LANES), lambda i: (0, i, 0))]
    else:
        body = _pre0_seq_kernel
        nb, tpb = n // seq, seq // tm
        t_spec = lambda k: pl.BlockSpec((k, 1, N_KV, HEAD_DIM, tm), lambda i: (0, i // tpb, 0, 0, i % tpb))
        out_shape += [jax.ShapeDtypeStruct((4, nb, N_KV, HEAD_DIM, seq), F32),
                      jax.ShapeDtypeStruct((2, nb, N_KV, HEAD_DIM, seq), F32),
                      jax.ShapeDtypeStruct((4, nb, N_KV, HEAD_DIM, seq), BF16),
                      jax.ShapeDtypeStruct((2, n, LANES), F32)]
        out_specs += [t_spec(4), t_spec(2), t_spec(4), pl.BlockSpec((2, tm, LANES), lambda i: (0, i, 0))]
    return pl.pallas_call(
        body,
        out_shape=tuple(out_shape),
        grid=(n // tm,),
        in_specs=[
            pl.BlockSpec((tm, d), lambda i: (i, 0)),
            pl.BlockSpec((1, d), lambda i: (0, 0)),
            mod_spec, mod_spec,
            pl.BlockSpec(w_pad.shape, lambda i: (0, 0)),
            pos_spec, pos_spec,
        ],
        out_specs=tuple(out_specs),
        compiler_params=_cparams("parallel"),
        name="pre0_in_proj",
    )(x2, g, shift, scale, w_pad, cos, sin)


def _rope_tables(pos):
    half = HEAD_DIM // 2
    inv = ROPE_THETA ** (-np.arange(half, dtype=np.float64) / half)
    ang = np.asarray(pos, np.float64)[:, None] * inv[None, :]
    cos, sin = np.cos(ang), np.sin(ang)
    cos_t = np.tile(np.concatenate([cos, cos], axis=1), (1, LANES // HEAD_DIM))
    sin_t = np.tile(np.concatenate([-sin, sin], axis=1), (1, LANES // HEAD_DIM))
    return jnp.asarray(cos_t, F32), jnp.asarray(sin_t, F32)


def _pool_body(ext_ref, t_rows, pos0, w_ref, scale_ref, out_ref):
    pos = pos0 + lax.broadcasted_iota(jnp.int32, (t_rows, 1), 0)
    for g, w in enumerate(POOL_WINDOWS):
        cols = slice(LANES * g, LANES * (g + 1))
        x = ext_ref[POOL_PAD:POOL_PAD + t_rows, cols]
        s = x
        for j in range(1, w):
            s = s + ext_ref[POOL_PAD - j:POOL_PAD - j + t_rows, cols]
        cnt = jnp.minimum(pos + 1, w).astype(F32)
        dlt = (s / cnt - x).astype(BF16)
        y = _dot(dlt, w_ref[g]) * scale_ref[:, cols]
        out_ref[:, cols] = y.astype(out_ref.dtype)


def _pool_prompt_kernel(x_ref, prev_ref, w_ref, scale_ref, out_ref, ext_ref, *, tiles_per_batch, tm):
    t_in_b = pl.program_id(0) % tiles_per_batch
    ext_ref[0:POOL_PAD, :] = jnp.where(t_in_b > 0, prev_ref[...], 0.0)
    ext_ref[POOL_PAD:, :] = x_ref[...]
    _pool_body(ext_ref, tm, t_in_b * tm, w_ref, scale_ref, out_ref)


def _pool_prompt(u, w_grp, scale, seq, tm):
    n, c = u.shape
    tpb = seq // tm
    hb = tm // POOL_PAD
    return pl.pallas_call(
        functools.partial(_pool_prompt_kernel, tiles_per_batch=tpb, tm=tm),
        out_shape=jax.ShapeDtypeStruct((n, c), BF16),
        grid=(n // tm,),
        in_specs=[
            pl.BlockSpec((tm, c), lambda i: (i, 0)),
            pl.BlockSpec((POOL_PAD, c), lambda i: (jnp.maximum(i * hb - 1, 0), 0)),
            pl.BlockSpec(w_grp.shape, lambda i: (0, 0, 0)),
            pl.BlockSpec((1, c), lambda i: (0, 0)),
        ],
        out_specs=pl.BlockSpec((tm, c), lambda i: (i, 0)),
        scratch_shapes=[pltpu.VMEM((POOL_PAD + tm, c), F32)],
        compiler_params=_cparams("parallel"),
        name="pool_prompt",
    )(u, u, w_grp, scale)


def _pool_sample_kernel(ext_ref, w_ref, scale_ref, out_ref, *, pos0, t_rows):
    _pool_body(ext_ref.at[0], t_rows, pos0, w_ref, scale_ref, out_ref.at[0])


def _pool_sample(ext, w_grp, scale, pos0):
    b, rows, c = ext.shape
    t_rows = rows - POOL_PAD
    return pl.pallas_call(
        functools.partial(_pool_sample_kernel, pos0=pos0, t_rows=t_rows),
        out_shape=jax.ShapeDtypeStruct((b, t_rows, c), BF16),
        grid=(b,),
        in_specs=[
            pl.BlockSpec((1, rows, c), lambda i: (i, 0, 0)),
            pl.BlockSpec(w_grp.shape, lambda i: (0, 0, 0)),
            pl.BlockSpec((1, c), lambda i: (0, 0)),
        ],
        out_specs=pl.BlockSpec((1, t_rows, c), lambda i: (i, 0, 0)),
        compiler_params=_cparams("parallel"),
        name="pool_sample",
    )(ext, w_grp, scale)


def _chunk_rows(ref2d, rows):
    n = rows // CMP_STRIDE
    return jnp.concatenate([ref2d[pl.ds(r, n, stride=CMP_STRIDE), :] for r in range(CMP_STRIDE)], axis=1)


def _cmp_partial(a, pos_ref, w_ref, p_ref):
    hid2 = w_ref.shape[2] // 2
    p_ref[0, 0, :, 0:hid2] = _dot((a + pos_ref[0, 0:1, :]).astype(BF16), w_ref[0, :, 0:hid2])
    p_ref[0, 0, :, hid2:] = _dot((a + pos_ref[0, 1:2, :]).astype(BF16), w_ref[0, :, hid2:])


def _cmp_rows_kernel(x_ref, pos_ref, w_ref, p_ref, *, rows):
    _cmp_partial(_chunk_rows(x_ref.at[0, 0], rows), pos_ref, w_ref, p_ref)


def _cmp_rows(x4, pos_ab, w1_ab, nbatch, rows):
    nch = rows // CMP_STRIDE
    return pl.pallas_call(
        functools.partial(_cmp_rows_kernel, rows=rows),
        out_shape=jax.ShapeDtypeStruct((2, nbatch, nch, w1_ab.shape[2]), F32),
        grid=(2, nbatch),
        in_specs=[
            pl.BlockSpec((1, 1, rows, LANES), lambda s, b: (s, b, 0, 0)),
            pl.BlockSpec((1, 2, pos_ab.shape[2]), lambda s, b: (s, 0, 0)),
            pl.BlockSpec((1,) + w1_ab.shape[1:], lambda s, b: (s, 0, 0)),
        ],
        out_specs=pl.BlockSpec((1, 1, nch, w1_ab.shape[2]), lambda s, b: (s, b, 0, 0)),
        compiler_params=_cparams("parallel", "parallel"),
        name="cmp_rows",
    )(x4, pos_ab, w1_ab)


PAGES_PER_STEP = 64


def _cmp_pages_kernel(pt_ref, *refs, pps):
    del pt_ref
    page_refs = refs[:pps]
    pos_ref, w_ref, p_ref, rows_ref = refs[pps:]
    for k, r in enumerate(page_refs):
        page_t = jnp.concatenate([r[0, 0, g] for g in range(N_KV)], axis=0)
        rows_ref[k * PAGE_SIZE:(k + 1) * PAGE_SIZE, :] = page_t.T
    _cmp_partial(_chunk_rows(rows_ref, pps * PAGE_SIZE), pos_ref, w_ref, p_ref)


def _cmp_pages(cache_t, page_table_flat, n_pages, pos_ab, w1_ab, nbatch):
    pps = min(PAGES_PER_STEP, n_pages)
    assert n_pages % pps == 0
    steps = n_pages // pps
    nch = pps * PAGE_SIZE // CMP_STRIDE

    def page_spec(k):
        return pl.BlockSpec(
            (1, 1, N_KV, HEAD_DIM, PAGE_SIZE),
            lambda s, b, j, pt: (s, pt[b * n_pages + j * pps + k], 0, 0, 0))

    grid_spec = pltpu.PrefetchScalarGridSpec(
        num_scalar_prefetch=1,
        grid=(2, nbatch, steps),
        in_specs=[page_spec(k) for k in range(pps)] + [
            pl.BlockSpec((1, 2, pos_ab.shape[2]), lambda s, b, j, pt: (s, 0, 0)),
            pl.BlockSpec((1,) + w1_ab.shape[1:], lambda s, b, j, pt: (s, 0, 0)),
        ],
        out_specs=pl.BlockSpec((1, 1, nch, w1_ab.shape[2]), lambda s, b, j, pt: (s, b, j, 0)),
        scratch_shapes=[pltpu.VMEM((pps * PAGE_SIZE, LANES), F32)],
    )
    return pl.pallas_call(
        functools.partial(_cmp_pages_kernel, pps=pps),
        out_shape=jax.ShapeDtypeStruct((2, nbatch, steps * nch, w1_ab.shape[2]), F32),
        grid_spec=grid_spec,
        compiler_params=_cparams("parallel", "parallel", "parallel"),
        name="cmp_pages",
    )(page_table_flat, *([cache_t] * pps), pos_ab, w1_ab)


def _gelu_tanh(x):
    return x * (0.5 * (1.0 + jnp.tanh(np.sqrt(2.0 / np.pi).astype(np.float32) * (x + 0.044715 * (x * x * x)))))


def _cmp_combine_kernel(*refs, n_parts):
    p_refs = refs[:n_parts]
    w2_ref, w2t_ref, o_ref, ot_ref, pa_ref, pb_ref = refs[n_parts:]
    hid2 = pa_ref.shape[1]
    n_p = pa_ref.shape[0]
    r0 = 0
    for p_ref in p_refs:
        rows = p_ref.shape[2]
        pa_ref[r0:r0 + rows, :] = p_ref[0, 0, :, 0:hid2]
        pb_ref[r0:r0 + rows, :] = p_ref[0, 0, :, hid2:]
        r0 += rows
    pb_ref[n_p:n_p + 8, :] = jnp.zeros((8, hid2), F32)
    hsum = pa_ref[...] + pb_ref[1:n_p + 1, :]
    act = _gelu_tanh(hsum).astype(BF16)
    o_ref[0, 0] = _dot(act, w2_ref[0]).astype(o_ref.dtype)
    ot_ref[0, 0] = _dot_nt(w2t_ref[0], act).astype(ot_ref.dtype)


def _cmp_combine(parts, w2_bd):
    nbatch, width = parts[0].shape[1], parts[0].shape[3]
    n_p = sum(p.shape[2] for p in parts)
    w2t_bd = w2_bd.transpose(0, 2, 1)
    return pl.pallas_call(
        functools.partial(_cmp_combine_kernel, n_parts=len(parts)),
        out_shape=(jax.ShapeDtypeStruct((2, nbatch, n_p, LANES), BF16),
                   jax.ShapeDtypeStruct((2, nbatch, LANES, n_p), BF16)),
        grid=(2, nbatch),
        in_specs=[pl.BlockSpec((1, 1, p.shape[2], width), lambda s, b: (s, b, 0, 0)) for p in parts] + [
            pl.BlockSpec((1,) + w2_bd.shape[1:], lambda s, b: (s, 0, 0)),
            pl.BlockSpec((1,) + w2t_bd.shape[1:], lambda s, b: (s, 0, 0)),
        ],
        out_specs=(pl.BlockSpec((1, 1, n_p, LANES), lambda s, b: (s, b, 0, 0)),
                   pl.BlockSpec((1, 1, LANES, n_p), lambda s, b: (s, b, 0, 0))),
        scratch_shapes=[pltpu.VMEM((n_p, width // 2), F32), pltpu.VMEM((n_p + 8, width // 2), F32)],
        compiler_params=_cparams("parallel", "parallel"),
        name="cmp_combine",
    )(*parts, w2_bd, w2t_bd)


def _cmp_weights(cmp_pos, cmp_w1, cmp_w2):
    hid = cmp_w1.shape[2]
    half = CMP_STRIDE * HEAD_DIM
    eye = jnp.eye(N_KV, dtype=F32)
    pos_ab = jnp.tile(cmp_pos.reshape(2, 2, CMP_STRIDE, 1, HEAD_DIM), (1, 1, 1, N_KV, 1))
    pos_ab = pos_ab.reshape(2, 2, CMP_STRIDE * N_KV * HEAD_DIM)
    w1 = cmp_w1.reshape(2, 2, CMP_STRIDE, HEAD_DIM, hid)
    w1_bd = jnp.einsum("shrdj,gk->shrgdkj", w1, eye)
    w1_bd = w1_bd.reshape(2, 2, CMP_STRIDE * N_KV * HEAD_DIM, N_KV * hid)
    w1_ab = jnp.concatenate([w1_bd[:, 0], w1_bd[:, 1]], axis=2).astype(BF16)
    w2_bd = jnp.einsum("sjd,gk->sgjkd", cmp_w2, eye).reshape(2, N_KV * hid, N_KV * HEAD_DIM).astype(BF16)
    del half
    return pos_ab, w1_ab, w2_bd


def _online_update(s, v_aug_t, state):
    tq, ck = s.shape
    m_new = jnp.broadcast_to(jnp.max(s, axis=-1, keepdims=True), (tq, LANES))
    if state is not None:
        acc_old, m_old = state
        m_new = jnp.maximum(m_old, m_new)
    p = jnp.exp(s - pltpu.repeat(m_new, ck // LANES, axis=1)).astype(BF16)
    acc = _dot_nt(p, v_aug_t)
    if state is not None:
        acc = jnp.exp(m_old - m_new) * acc_old + acc
    return acc, m_new


def _attn_prompt_kernel(q_ref, cmpt_ref, kst_ref, vst_ref, kwt_ref, vwt_ref, gate_ref, mt_ref, e_ref,
                        o_ref, kaug_s, vaug_s, vwaug_s, *, tq, nblk):
    i = pl.program_id(1)
    s0 = i * tq
    seq = kst_ref.shape[4]
    ncmp = cmpt_ref.shape[3]

    @pl.when(i == 0)
    def _():
        ones_row = (lax.broadcasted_iota(jnp.int32, (HEAD_DIM, seq), 0) == 0).astype(BF16)
        for g in range(N_KV):
            kaug_s[g, 0:LANES, :] = e_ref[...]
            kaug_s[g, LANES:LANES + HEAD_DIM, :] = kst_ref[0, 0, g]
            vaug_s[g, 0:HEAD_DIM, :] = vst_ref[0, 0, g]
            vaug_s[g, HEAD_DIM:, :] = ones_row
            vwaug_s[g, 0:HEAD_DIM, :] = vwt_ref[0, 0, g]
            vwaug_s[g, HEAD_DIM:, :] = ones_row

    row = lax.broadcasted_iota(jnp.int32, (tq, 1), 0)
    col = lax.broadcasted_iota(jnp.int32, (1, tq), 1)
    qpos = s0 + row
    gates = gate_ref[...]
    c_end = lax.broadcasted_iota(jnp.int32, (1, ncmp), 1) * CMP_STRIDE + (CMP_BLOCK - 1)
    mask_c = c_end <= qpos
    qp_l = s0 + col
    cur = qp_l // SEL_BLOCK
    n_rb = -(-nblk // 8)
    qs = [q_ref[:, HEAD_DIM * hh:HEAD_DIM * (hh + 1)] for hh in range(N_HEADS)]

    o_cmp, q_aug = [], []
    for g in range(N_KV):
        kct_g = cmpt_ref[0, 0, HEAD_DIM * g:HEAD_DIM * (g + 1), :]
        vct_g = cmpt_ref[1, 0, HEAD_DIM * g:HEAD_DIM * (g + 1), :]
        p_sum = jnp.zeros((tq, ncmp), F32)
        for qh in qs[GROUP_R * g:GROUP_R * (g + 1)]:
            s = jnp.where(mask_c, _dot(qh, kct_g), -jnp.inf)
            m = jnp.max(s, axis=-1, keepdims=True)
            m = jnp.where(m == -jnp.inf, 0.0, m)
            e = jnp.exp(s - m)
            d = jnp.sum(e, axis=-1, keepdims=True)
            p = e / jnp.where(d > 0, d, 1.0)
            p_sum = p_sum + p
            o_cmp.append(_dot_nt(p.astype(BF16), vct_g))

        p_hi = p_sum.astype(BF16)
        p_lo = (p_sum - p_hi.astype(F32)).astype(BF16)
        imp = _dot_nt(mt_ref[...], p_hi) + _dot_nt(mt_ref[...], p_lo)
        score, valid = [], []
        for r in range(n_rb):
            jb = 8 * r + lax.broadcasted_iota(jnp.int32, (8, 1), 0)
            ok = (jb * SEL_BLOCK <= qp_l) & (jb < nblk)
            forced = (jb == 0) | (jb == cur) | (jb == cur - 1)
            valid.append(ok)
            score.append(jnp.where(ok, jnp.where(forced, jnp.inf, imp[8 * r:8 * r + 8, :]), -jnp.inf))
        cnt = [jnp.zeros((8, tq), jnp.int32) for _ in range(n_rb)]
        for j in range(nblk):
            rj = jnp.broadcast_to(score[j // 8][j % 8:j % 8 + 1, :], (8, tq))
            for r in range(n_rb):
                if 8 * r > j:
                    beats = rj >= score[r]
                elif 8 * r + 7 < j:
                    beats = rj > score[r]
                else:
                    later = 8 * r + lax.broadcasted_iota(jnp.int32, (8, 1), 0) > j
                    beats = (rj > score[r]) | ((rj == score[r]) & later)
                cnt[r] = cnt[r] + jnp.where(beats, 1, 0)
        selneg_t = [jnp.where((cnt[r] < N_SEL) & valid[r], 0.0, NEG) for r in range(n_rb)]
        selneg_t.append(jnp.zeros((LANES - 8 * n_rb, tq), F32))
        selneg = jnp.concatenate(selneg_t, axis=0).T.astype(BF16)
        q_aug += [jnp.concatenate([selneg, qh], axis=1) for qh in qs[GROUP_R * g:GROUP_R * (g + 1)]]

    ck = 2 * tq

    def sel_chunk(c0, state, keep):
        out = []
        for g in range(N_KV):
            k_aug = kaug_s[g, :, pl.ds(c0, ck)]
            v_aug = vaug_s[g, :, pl.ds(c0, ck)]
            for hh in range(GROUP_R * g, GROUP_R * (g + 1)):
                s = _dot(q_aug[hh], k_aug)
                if keep is not None:
                    s = jnp.where(keep, s, NEG)
                out.append(_online_update(s, v_aug, None if state is None else state[hh]))
        return tuple(out)

    n_full = i // 2
    last0 = pl.multiple_of(jnp.maximum(i - 1, 0) * tq, tq)
    kpos = last0 + lax.broadcasted_iota(jnp.int32, (1, ck), 1)
    state = sel_chunk(last0, None, (kpos <= qpos) & (kpos >= n_full * ck))
    state = lax.fori_loop(0, n_full, lambda c, st: sel_chunk(pl.multiple_of(c * ck, ck), st, None), state)
    o_sel = [acc[:, 0:HEAD_DIM] / acc[:, HEAD_DIM:HEAD_DIM + 1] for acc, _ in state]

    keep_w = (kpos <= qpos) & (kpos > qpos - WINDOW)
    wstate = []
    for g in range(N_KV):
        k_t = kwt_ref[0, 0, g, :, pl.ds(last0, ck)]
        v_aug = vwaug_s[g, :, pl.ds(last0, ck)]
        for hh in range(GROUP_R * g, GROUP_R * (g + 1)):
            wstate.append(_online_update(jnp.where(keep_w, _dot(qs[hh], k_t), NEG), v_aug, None))
    for hh in range(N_HEADS):
        acc_w = wstate[hh][0]
        o_win = acc_w[:, 0:HEAD_DIM] / acc_w[:, HEAD_DIM:HEAD_DIM + 1]
        o = (o_cmp[hh] * gates[:, 3 * hh:3 * hh + 1] + o_sel[hh] * gates[:, 3 * hh + 1:3 * hh + 2]
             + o_win * gates[:, 3 * hh + 2:3 * hh + 3])
        o_ref[:, HEAD_DIM * hh:HEAD_DIM * (hh + 1)] = o.astype(o_ref.dtype)


def _attn_prompt(q, cmp_t, kvbt, gates, mt, e_mat, nbatch, seq, tq):
    assert tq >= WINDOW and seq >= 2 * tq
    n = q.shape[0]
    nq = seq // tq
    ncmp = cmp_t.shape[3]
    kv_spec = lambda slot: pl.BlockSpec((1, 1, N_KV, HEAD_DIM, seq), lambda b, i: (slot, b, 0, 0, 0))
    return pl.pallas_call(
        functools.partial(_attn_prompt_kernel, tq=tq, nblk=seq // SEL_BLOCK),
        out_shape=jax.ShapeDtypeStruct((n, N_HEADS * HEAD_DIM), BF16),
        grid=(nbatch, nq),
        in_specs=[
            pl.BlockSpec((tq, N_HEADS * HEAD_DIM), lambda b, i: (b * nq + i, 0)),
            pl.BlockSpec((2, 1, LANES, ncmp), lambda b, i: (0, b, 0, 0)),
            kv_spec(0), kv_spec(1), kv_spec(2), kv_spec(3),
            pl.BlockSpec((tq, LANES), lambda b, i: (b * nq + i, 0)),
            pl.BlockSpec(mt.shape, lambda b, i: (0, 0)),
            pl.BlockSpec(e_mat.shape, lambda b, i: (0, 0)),
        ],
        out_specs=pl.BlockSpec((tq, N_HEADS * HEAD_DIM), lambda b, i: (b * nq + i, 0)),
        scratch_shapes=[
            pltpu.VMEM((N_KV, LANES + HEAD_DIM, seq), BF16),
            pltpu.VMEM((N_KV, LANES, seq), BF16),
            pltpu.VMEM((N_KV, LANES, seq), BF16),
        ],
        compiler_params=_cparams("arbitrary", "arbitrary"),
        name="attn_prompt",
    )(q, cmp_t, kvbt, kvbt, kvbt, kvbt, gates, mt, e_mat)


def _imp_matrix(ncmp, nblk_pad):
    per = SEL_BLOCK // CMP_STRIDE
    pad = CMP_BLOCK // CMP_STRIDE - 1
    n = np.arange(ncmp)[:, None]
    j = np.arange(nblk_pad)[None, :]
    return ((n >= per * j - pad) & (n <= per * j + per - 1)).astype(np.float32)


def _attn_sample_cmp_kernel(q_ref, kc_ref, vc_ref, m_ref, oc_ref, idx_ref, *, n_valid_cmp, qpos0, nblk):
    ncmp = kc_ref.shape[2]
    rows = q_ref.shape[2]
    t_row = lax.broadcasted_iota(jnp.int32, (rows, 1), 0) % 8
    c_idx = lax.broadcasted_iota(jnp.int32, (1, ncmp), 1)
    mask_c = (c_idx * CMP_STRIDE + (CMP_BLOCK - 1) <= qpos0 + t_row) & (c_idx < n_valid_cmp)
    p_tok = []
    for g in range(N_KV):
        gs = slice(HEAD_DIM * g, HEAD_DIM * (g + 1))
        s = jnp.where(mask_c, _dot_nt(q_ref[0, g], kc_ref[0, 0, :, gs]), -jnp.inf)
        m = jnp.max(s, axis=-1, keepdims=True)
        m = jnp.where(m == -jnp.inf, 0.0, m)
        e = jnp.exp(s - m)
        d = jnp.sum(e, axis=-1, keepdims=True)
        p = e / jnp.where(d > 0, d, 1.0)
        oc_ref[0, g] = _dot(p.astype(BF16), vc_ref[0, 0, :, gs])
        p_tok.append(jnp.sum(p.reshape(GROUP_R, 8, ncmp), axis=0))
    p_all = jnp.concatenate(p_tok, axis=0)
    p_hi = p_all.astype(BF16)
    p_lo = (p_all - p_hi.astype(F32)).astype(BF16)
    imp = _dot(p_hi, m_ref[...]) + _dot(p_lo, m_ref[...])
    nb_pad = imp.shape[1]
    jb = lax.broadcasted_iota(jnp.int32, (1, nb_pad), 1)
    qpos = qpos0 + lax.broadcasted_iota(jnp.int32, (2 * 8, 1), 0) % 8
    cur = qpos // SEL_BLOCK
    valid = (jb * SEL_BLOCK <= qpos) & (jb < nblk)
    forced = (jb == 0) | (jb == cur) | (jb == cur - 1)
    score = jnp.where(valid, jnp.where(forced, jnp.inf, imp), -jnp.inf)
    avail = jb < nblk
    lane = lax.broadcasted_iota(jnp.int32, (2 * 8, LANES), 1)
    picked = jnp.zeros((2 * 8, LANES), jnp.int32)
    for k in range(N_SEL):
        best = jnp.max(jnp.where(avail, score, -jnp.inf), axis=-1, keepdims=True)
        cand = avail & (score == best)
        idx = jnp.min(jnp.where(cand, jb, nb_pad), axis=-1, keepdims=True)
        picked = jnp.where(lane == k, idx, picked)
        avail = avail & (jb != idx)
    idx_ref[0] = picked


def _attn_sample_cmp(qg, cmp_kv, m_mat, n_valid_cmp, qpos0, nblk):
    nbatch = qg.shape[0]
    ncmp = cmp_kv.shape[2]
    rows = qg.shape[2]
    cmp_spec = lambda slot: pl.BlockSpec((1, 1, ncmp, LANES), lambda b: (slot, b, 0, 0))
    return pl.pallas_call(
        functools.partial(_attn_sample_cmp_kernel, n_valid_cmp=n_valid_cmp, qpos0=qpos0, nblk=nblk),
        out_shape=(
            jax.ShapeDtypeStruct((nbatch, N_KV, rows, HEAD_DIM), F32),
            jax.ShapeDtypeStruct((nbatch, 2 * 8, LANES), jnp.int32),
        ),
        grid=(nbatch,),
        in_specs=[
            pl.BlockSpec((1, N_KV, rows, HEAD_DIM), lambda b: (b, 0, 0, 0)),
            cmp_spec(0), cmp_spec(1),
            pl.BlockSpec(m_mat.shape, lambda b: (0, 0)),
        ],
        out_specs=(
            pl.BlockSpec((1, N_KV, rows, HEAD_DIM), lambda b: (b, 0, 0, 0)),
            pl.BlockSpec((1, 2 * 8, LANES), lambda b: (b, 0, 0)),
        ),
        compiler_params=_cparams("parallel"),
        name="attn_sample_cmp",
    )(qg, cmp_kv, cmp_kv, m_mat)


def _masked_softmax(s, mask):
    s = jnp.where(mask, s, -jnp.inf)
    m = jnp.max(s, axis=-1, keepdims=True)
    m = jnp.where(m == -jnp.inf, 0.0, m)
    e = jnp.exp(s - m)
    d = jnp.sum(e, axis=-1, keepdims=True)
    return e / jnp.where(d > 0, d, 1.0)


def _attn_sample_sel_kernel(idx_ref, pt_ref, *refs, dec_seq, qpos0, n_cache_blk, wb):
    del pt_ref
    nsel = N_KV * N_SEL
    per_page = PAGE_SIZE // SEL_BLOCK
    kv_refs = refs[:nsel]
    q_ref, tail_ref, kw_ref, vw_ref, wnew_ref, oc_ref, gate_ref, o_ref, osel_ref = refs[nsel:]
    b = pl.program_id(0)
    t = pl.program_id(1)
    rows = q_ref.shape[2]
    t_row = lax.broadcasted_iota(jnp.int32, (rows, 1), 0) % 8
    qpos = qpos0 + t_row
    colk = lax.broadcasted_iota(jnp.int32, (1, N_SEL * PAGE_SIZE), 1)
    slot_of_col = colk // PAGE_SIZE
    row_in_page = colk % PAGE_SIZE

    @pl.when(t == 0)
    def _():
        osel_ref[...] = jnp.zeros(osel_ref.shape, F32)

    for g in range(N_KV):
        blk_of_col = jnp.zeros_like(colk)
        k_pages, v_pages = [], []
        for k in range(N_SEL):
            blk = idx_ref[((b * dec_seq + t) * N_KV + g) * N_SEL + k]
            is_new = blk >= n_cache_blk
            k_pages.append(jnp.where(is_new, tail_ref[0, 0, g], kv_refs[g * N_SEL + k][0, 0, 0]))
            v_pages.append(jnp.where(is_new, tail_ref[1, 0, g], kv_refs[g * N_SEL + k][1, 0, 0]))
            blk_of_col = blk_of_col + jnp.where(slot_of_col == k, blk, 0)
        in_block = row_in_page // SEL_BLOCK == blk_of_col % per_page
        kpos = blk_of_col * SEL_BLOCK + row_in_page % SEL_BLOCK
        k_t = jnp.concatenate(k_pages, axis=1).astype(BF16)
        v_t = jnp.concatenate(v_pages, axis=1).astype(BF16)
        p = _masked_softmax(_dot(q_ref[0, g], k_t), in_block & (kpos <= qpos))
        o = _dot_nt(p.astype(BF16), v_t)
        osel_ref[g] = osel_ref[g] + jnp.where(t_row == t, o, 0.0)

    @pl.when(t == dec_seq - 1)
    def _():
        colw = lax.broadcasted_iota(jnp.int32, (1, wb + LANES), 1)
        kwpos = jnp.where(colw < wb, qpos0 - wb + colw, qpos0 + colw - wb)
        dpos = qpos - kwpos
        mask_w = (dpos >= 0) & (dpos < WINDOW) & (kwpos >= 0) & (colw < wb + dec_seq)
        for g in range(N_KV):
            k_t = jnp.concatenate([kw_ref[0, 0, g], wnew_ref[0, 0, g]], axis=1).astype(BF16)
            v_t = jnp.concatenate([vw_ref[0, 0, g], wnew_ref[1, 0, g]], axis=1).astype(BF16)
            p = _masked_softmax(_dot(q_ref[0, g], k_t), mask_w)
            o_win = _dot_nt(p.astype(BF16), v_t)
            gt = gate_ref[0, g]
            o_ref[0, g] = oc_ref[0, g] * gt[:, 0:1] + osel_ref[g] * gt[:, 1:2] + o_win * gt[:, 2:3]


def _attn_sample_sel(idx_flat, pt_flat, cache_t, qg, tail_t, win_t, wnew_t, o_cmp, gates_g, dec_seq, n_pages,
                     qpos0):
    nbatch = qg.shape[0]
    rows = qg.shape[2]
    wb = win_t.shape[4]
    per_page = PAGE_SIZE // SEL_BLOCK
    n_cache_blk = n_pages * per_page

    def page_spec(g, k):
        def imap(b, t, idx, pt):
            blk = jnp.minimum(idx[((b * dec_seq + t) * N_KV + g) * N_SEL + k], n_cache_blk - 1)
            return (1, pt[b * n_pages + blk // per_page], g, 0, 0)
        return pl.BlockSpec((2, 1, 1, HEAD_DIM, PAGE_SIZE), imap)

    kv_specs = [page_spec(g, k) for g in range(N_KV) for k in range(N_SEL)]
    per_b = lambda shape: pl.BlockSpec((1,) + shape, lambda b, t, idx, pt: (b,) + (0,) * len(shape))
    new_spec = pl.BlockSpec((2, 1, N_KV, HEAD_DIM, LANES), lambda b, t, idx, pt: (0, b, 0, 0, 0))
    win_spec = lambda s: pl.BlockSpec((1, 1, N_KV, HEAD_DIM, wb), lambda b, t, idx, pt: (s, b, 0, 0, 0))
    grid_spec = pltpu.PrefetchScalarGridSpec(
        num_scalar_prefetch=2,
        grid=(nbatch, dec_seq),
        in_specs=kv_specs + [
            per_b((N_KV, rows, HEAD_DIM)),
            new_spec, win_spec(0), win_spec(1), new_spec,
            per_b((N_KV, rows, HEAD_DIM)),
            per_b((N_KV, rows, LANES)),
        ],
        out_specs=per_b((N_KV, rows, HEAD_DIM)),
        scratch_shapes=[pltpu.VMEM((N_KV, rows, HEAD_DIM), F32)],
    )
    n_page_specs = N_KV * N_SEL
    return pl.pallas_call(
        functools.partial(_attn_sample_sel_kernel, dec_seq=dec_seq, qpos0=qpos0, n_cache_blk=n_cache_blk, wb=wb),
        out_shape=jax.ShapeDtypeStruct((nbatch, N_KV, rows, HEAD_DIM), F32),
        grid_spec=grid_spec,
        compiler_params=_cparams("parallel", "arbitrary"),
        name="attn_sample_sel",
    )(idx_flat, pt_flat, *([cache_t] * n_page_specs), qg, tail_t, win_t, win_t, wnew_t, o_cmp, gates_g)


FFN_CHUNK = 256


def _post_kernel(*refs, n_a, final):
    y_ref = refs[0]
    a_refs = refs[1:1 + 2 * n_a]
    gmix_ref, g2_ref, sh_ref, sc_ref, gffn_ref, w1_ref, w2_ref = refs[1 + 2 * n_a:8 + 2 * n_a]
    rest = refs[8 + 2 * n_a:]
    out_ref = rest[-1]
    mix = _dot(a_refs[0][...], a_refs[1][...])
    for k in range(1, n_a):
        mix = mix + _dot(a_refs[2 * k][...], a_refs[2 * k + 1][...])
    y1 = y_ref[...] + gmix_ref[0] * mix
    h = _modulate(y1, g2_ref[...], sh_ref[0], sc_ref[0]).astype(BF16)
    d_ff = w2_ref.shape[0]
    acc = jnp.zeros(y1.shape, F32)
    for c in range(d_ff // FFN_CHUNK):
        c0 = c * FFN_CHUNK
        gate = _dot(h, w1_ref[:, c0:c0 + FFN_CHUNK])
        up = _dot(h, w1_ref[:, d_ff + c0:d_ff + c0 + FFN_CHUNK])
        acc = acc + _dot((_silu(gate) * up).astype(BF16), w2_ref[c0:c0 + FFN_CHUNK, :])
    y2 = y1 + gffn_ref[0] * acc
    if final:
        fg_ref = rest[0]
        ms = jnp.mean(y2 * y2, axis=-1, keepdims=True)
        y2 = (y2 * lax.rsqrt(ms + NORM_EPS)) * fg_ref[...]
    out_ref[...] = y2


def _post(y, a_list, wo_list, gmix, g2, shift, scale, gffn, w1, w2, final_g, tm, rows_per_mod):
    n, d = y.shape
    r = shift.shape[1]
    mod_spec = pl.BlockSpec((1, r, d), lambda i: (i // rows_per_mod, 0, 0))
    const = lambda arr: pl.BlockSpec(arr.shape, lambda i: (0,) * arr.ndim, pipeline_mode=pl.Buffered(1))
    in_specs = [pl.BlockSpec((tm, d), lambda i: (i, 0))]
    args = [y]
    for a, wo in zip(a_list, wo_list):
        in_specs += [pl.BlockSpec((tm, a.shape[1]), lambda i: (i, 0)), const(wo)]
        args += [a, wo]
    in_specs += [mod_spec, pl.BlockSpec((1, d), lambda i: (0, 0)), mod_spec, mod_spec, mod_spec, const(w1), const(w2)]
    args += [gmix, g2, shift, scale, gffn, w1, w2]
    if final_g is not None:
        in_specs.append(pl.BlockSpec((1, d), lambda i: (0, 0)))
        args.append(final_g)
    return pl.pallas_call(
        functools.partial(_post_kernel, n_a=len(a_list), final=final_g is not None),
        out_shape=jax.ShapeDtypeStruct((n, d), F32),
        grid=(n // tm,),
        in_specs=in_specs,
        out_specs=pl.BlockSpec((tm, d), lambda i: (i, 0)),
        compiler_params=_cparams("parallel"),
        name="post_proj_ffn",
    )(*args)


def _pre1_kernel(x_ref, g_ref, sh_ref, sc_ref, w_ref, u_ref):
    h = _modulate(x_ref[...], g_ref[...], sh_ref[0], sc_ref[0]).astype(BF16)
    d = u_ref.shape[1]
    a = _dot(h, w_ref[:, 0:d])
    b = _dot(h, w_ref[:, d:2 * d])
    u_ref[...] = a * _sigmoid(b)


def _pre1(x2, g, shift, scale, pw1, tm, rows_per_mod):
    n, d = x2.shape
    r = shift.shape[1]
    mod_spec = pl.BlockSpec((1, r, d), lambda i: (i // rows_per_mod, 0, 0))
    return pl.pallas_call(
        _pre1_kernel,
        out_shape=jax.ShapeDtypeStruct((n, d), F32),
        grid=(n // tm,),
        in_specs=[
            pl.BlockSpec((tm, d), lambda i: (i, 0)),
            pl.BlockSpec((1, d), lambda i: (0, 0)),
            mod_spec, mod_spec,
            pl.BlockSpec(pw1.shape, lambda i: (0, 0)),
        ],
        out_specs=pl.BlockSpec((tm, d), lambda i: (i, 0)),
        compiler_params=_cparams("parallel"),
        name="pre1_pw_glu",
    )(x2, g, shift, scale, pw1)


def _conv_body(ext_ref, t_rows, dw_ref, dwb_ref, lng_ref, lnb_ref, out_ref):
    off = CONV_PAD - CONV_HIST
    acc = None
    for b in range(8):
        span = t_rows + (8 if b else 0)
        part = None
        for k in range(CONV_WIDTH):
            if (off + k) % 8 != b:
                continue
            base = off + k - b
            term = ext_ref[base:base + span, :] * dw_ref[k:k + 1, :]
            part = term if part is None else part + term
        part = part[b:b + t_rows, :]
        acc = part if acc is None else acc + part
    y = acc + dwb_ref[...]
    mu = jnp.mean(y, axis=-1, keepdims=True)
    var = jnp.mean(jnp.square(y - mu), axis=-1, keepdims=True)
    z = (y - mu) * lax.rsqrt(var + NORM_EPS) * lng_ref[...] + lnb_ref[...]
    out_ref[...] = _silu(z).astype(out_ref.dtype)


def _pre1_conv_kernel(x_ref, g_ref, sh_ref, sc_ref, w_ref, dw_ref, dwb_ref, lng_ref, lnb_ref,
                      out_ref, state_ref, ext_ref, *, tiles_per_batch, tm):
    t_in_b = pl.program_id(0) % tiles_per_batch

    @pl.when(t_in_b == 0)
    def _():
        ext_ref[0:CONV_PAD, :] = jnp.zeros((CONV_PAD, ext_ref.shape[1]), F32)

    @pl.when(t_in_b > 0)
    def _():
        ext_ref[0:CONV_PAD, :] = ext_ref[tm:tm + CONV_PAD, :]

    h = _modulate(x_ref[...], g_ref[...], sh_ref[0], sc_ref[0]).astype(BF16)
    d = out_ref.shape[1]
    a = _dot(h, w_ref[:, 0:d])
    b = _dot(h, w_ref[:, d:2 * d])
    ext_ref[CONV_PAD:, :] = a * _sigmoid(b)
    _conv_body(ext_ref, tm, dw_ref, dwb_ref, lng_ref, lnb_ref, out_ref)
    state_ref[0] = ext_ref[tm:tm + CONV_PAD, :]


def _pre1_conv_prompt(x2, g, shift, scale, pw1, dw, dwb, lng, lnb, seq, tm):
    assert tm >= CONV_PAD
    n, d = x2.shape
    tpb = seq // tm
    mod_spec = pl.BlockSpec((1, 1, d), lambda i: (i // tpb, 0, 0))
    vec = pl.BlockSpec((1, d), lambda i: (0, 0))
    return pl.pallas_call(
        functools.partial(_pre1_conv_kernel, tiles_per_batch=tpb, tm=tm),
        out_shape=(jax.ShapeDtypeStruct((n, d), BF16), jax.ShapeDtypeStruct((n // seq, CONV_PAD, d), F32)),
        grid=(n // tm,),
        in_specs=[
            pl.BlockSpec((tm, d), lambda i: (i, 0)),
            vec, mod_spec, mod_spec,
            pl.BlockSpec(pw1.shape, lambda i: (0, 0)),
            pl.BlockSpec(dw.shape, lambda i: (0, 0)),
            vec, vec, vec,
        ],
        out_specs=(pl.BlockSpec((tm, d), lambda i: (i, 0)),
                   pl.BlockSpec((1, CONV_PAD, d), lambda i: (i // tpb, 0, 0))),
        scratch_shapes=[pltpu.VMEM((CONV_PAD + tm, d), F32)],
        compiler_params=_cparams("arbitrary"),
        name="pre1_conv_prompt",
    )(x2, g, shift, scale, pw1, dw, dwb, lng, lnb)


def _conv_sample_kernel(ext_ref, dw_ref, dwb_ref, lng_ref, lnb_ref, out_ref, *, t_rows):
    _conv_body(ext_ref.at[0], t_rows, dw_ref, dwb_ref, lng_ref, lnb_ref, out_ref.at[0])


def _conv_sample(ext, dw, dwb, lng, lnb):
    b, rows, d = ext.shape
    t_rows = rows - CONV_PAD
    vec = pl.BlockSpec((1, d), lambda i: (0, 0))
    return pl.pallas_call(
        functools.partial(_conv_sample_kernel, t_rows=t_rows),
        out_shape=jax.ShapeDtypeStruct((b, t_rows, d), BF16),
        grid=(b,),
        in_specs=[
            pl.BlockSpec((1, rows, d), lambda i: (i, 0, 0)),
            pl.BlockSpec(dw.shape, lambda i: (0, 0)),
            vec, vec, vec,
        ],
        out_specs=pl.BlockSpec((1, t_rows, d), lambda i: (i, 0, 0)),
        compiler_params=_cparams("parallel"),
        name="conv_sample",
    )(ext, dw, dwb, lng, lnb)


ROW_TILE = 512
ATTN_TILE = 512


def _group_rows(x, nbatch, dec_seq):
    w = x.shape[1] // N_HEADS
    x = x.reshape(nbatch, dec_seq, N_KV, GROUP_R, w).transpose(0, 2, 3, 1, 4)
    x = jnp.pad(x, ((0, 0), (0, 0), (0, 0), (0, 8 - dec_seq), (0, 0)))
    return x.reshape(nbatch, N_KV, GROUP_R * 8, w)


def _ungroup_rows(x, nbatch, dec_seq):
    w = x.shape[3]
    x = x.reshape(nbatch, N_KV, GROUP_R, 8, w)[:, :, :, :dec_seq]
    return x.transpose(0, 3, 1, 2, 4).reshape(nbatch * dec_seq, N_HEADS * w)


def kernel(x_prompt, x_sample, cache_kv, page_table, cache_win, state_pool, state_conv, c_prompt, c_sample,
           ada_w, ada_b, norm_g, attn_w_in, attn_w_out, pool_w, pool_scale, cmp_pos, cmp_w1, cmp_w2,
           conv_pw1, conv_dw, conv_dw_b, conv_ln_g, conv_ln_b, conv_pw2, ffn_w1, ffn_w2, final_g):
    nb_p, seq, d = x_prompt.shape
    nb_s, dec_seq, _ = x_sample.shape
    n_p, n_s = nb_p * seq, nb_s * dec_seq
    n_pages = page_table.shape[1]
    past = n_pages * PAGE_SIZE
    wb = cache_win.shape[3]
    tm = min(ROW_TILE, seq)
    tq = min(ATTN_TILE, seq)
    pool_width = pool_w.shape[1] * pool_w.shape[2]
    att_width = N_HEADS * HEAD_DIM

    mods = _ada(jnp.concatenate([c_prompt, c_sample], axis=0), ada_w, ada_b)

    def mod_p(layer, k):
        return mods[layer, :nb_p, k * d:(k + 1) * d].reshape(nb_p, 1, d)

    def mod_s(layer, k):
        return jnp.repeat(mods[layer, nb_p:, k * d:(k + 1) * d], dec_seq, axis=0).reshape(1, n_s, d)

    y_p = x_prompt.reshape(n_p, d)
    y_s = x_sample.reshape(n_s, d)
    row = lambda v: v.reshape(1, -1)

    w_in = attn_w_in[0]
    w_in_pad = jnp.pad(w_in.T, ((0, (-w_in.shape[1]) % LANES), (0, 0))).astype(BF16)
    cos_p, sin_p = _rope_tables(np.arange(seq))
    cos_s, sin_s = _rope_tables(past + np.arange(n_s) % dec_seq)
    g0 = row(norm_g[0, 0])
    u_p, q_p, gate_p, kvt_p, wint_p, kvbt_p, cmp_in_p = _pre0(
        y_p, g0, mod_p(0, 0), mod_p(0, 1), w_in_pad, cos_p, sin_p, tm, seq // tm, seq // tm, seq=seq)
    u_s, q_s, gate_s, kv_s, win_s = _pre0(y_s, g0, mod_s(0, 0), mod_s(0, 1), w_in_pad, cos_s, sin_s, n_s, 1, 1)

    pool_w_b = pool_w[0].astype(BF16)
    pool_sc = row(pool_scale[0])
    ypool_p = _pool_prompt(u_p, pool_w_b, pool_sc, seq, tm)
    u_s3 = u_s.reshape(nb_s, dec_seq, pool_width)
    pool_ext = jnp.concatenate([
        jnp.zeros((nb_s, POOL_PAD - POOL_HIST, pool_width), F32), state_pool[0], u_s3,
        jnp.zeros((nb_s, 8 - dec_seq, pool_width), F32)], axis=1)
    ypool_s = _pool_sample(pool_ext, pool_w_b, pool_sc, past)[:, :dec_seq].reshape(n_s, pool_width)

    pos_ab, w1_ab, w2_bd = _cmp_weights(cmp_pos[0], cmp_w1[0], cmp_w2[0])
    cmp_in_p4 = cmp_in_p.reshape(2, nb_p, seq, LANES)
    _, cmp_t_p = _cmp_combine([_cmp_rows(cmp_in_p4, pos_ab, w1_ab, nb_p, seq)], w2_bd)
    cache_t = cache_kv[0].transpose(0, 1, 3, 4, 2)
    pt_flat = page_table.reshape(-1)
    part_past = _cmp_pages(cache_t, pt_flat, n_pages, pos_ab, w1_ab, nb_s)
    tail_rows = 2 * PAGE_SIZE
    kv_s4 = kv_s.reshape(4, nb_s, dec_seq, LANES)
    tail4 = jnp.pad(kv_s4, ((0, 0), (0, 0), (0, tail_rows - dec_seq), (0, 0)))
    part_tail = _cmp_rows(tail4.reshape(4, 1, nb_s * tail_rows, LANES), pos_ab, w1_ab, 1, nb_s * tail_rows)
    part_tail = part_tail.reshape(2, nb_s, tail_rows // CMP_STRIDE, part_tail.shape[3])
    cmp_s, _ = _cmp_combine([part_past, part_tail], w2_bd)
    total_len = past + dec_seq
    padded_len = -(-total_len // SEL_BLOCK) * SEL_BLOCK
    n_cmp_s = padded_len // CMP_STRIDE - CMP_BLOCK // CMP_STRIDE + 1
    nblk_s = padded_len // SEL_BLOCK

    assert seq // SEL_BLOCK <= LANES and dec_seq <= SEL_BLOCK
    ncmp_p = cmp_t_p.shape[3]
    mt_p = jnp.asarray(_imp_matrix(ncmp_p, LANES).T, BF16)
    e_mat = jnp.asarray((np.arange(seq)[None, :] // SEL_BLOCK == np.arange(LANES)[:, None]).astype(np.float32),
                        BF16)
    o_p = _attn_prompt(q_p, cmp_t_p, kvbt_p, gate_p, mt_p, e_mat, nb_p, seq, tq)

    qg_s = _group_rows(q_s, nb_s, dec_seq)
    nblk_pad = -(-nblk_s // LANES) * LANES
    m_s = jnp.asarray(_imp_matrix(cmp_s.shape[2], nblk_pad), BF16)
    o_cmp_s, picked = _attn_sample_cmp(qg_s, cmp_s, m_s, n_cmp_s, past, nblk_s)
    idx = picked.reshape(nb_s, N_KV, 8, LANES)[:, :, :dec_seq, :N_SEL].transpose(0, 2, 1, 3).reshape(-1)

    def new_rows_t(rows):
        x = rows.reshape(rows.shape[0], nb_s, dec_seq, N_KV, HEAD_DIM).transpose(0, 1, 3, 4, 2)
        return jnp.pad(x, ((0, 0), (0, 0), (0, 0), (0, 0), (0, LANES - dec_seq)))

    tail_t = new_rows_t(kv_s[2:4])
    wnew_t = new_rows_t(win_s)
    win_t = cache_win[0].transpose(0, 1, 3, 4, 2)
    gates_g = jnp.pad(_group_rows(gate_s[:, :3 * N_HEADS], nb_s, dec_seq), ((0, 0), (0, 0), (0, 0), (0, LANES - 3)))
    o_s = _attn_sample_sel(idx, pt_flat, cache_t, qg_s, tail_t, win_t, wnew_t, o_cmp_s, gates_g,
                           dec_seq, n_pages, past)
    o_s = _ungroup_rows(o_s, nb_s, dec_seq).astype(BF16)

    w_out = attn_w_out[0].astype(BF16)
    wo_list = [w_out[:pool_width], w_out[pool_width:]]
    g1 = row(norm_g[0, 1])
    ffn1_0, ffn2_0 = ffn_w1[0].astype(BF16), ffn_w2[0].astype(BF16)
    y_p = _post(y_p, [ypool_p, o_p], wo_list, mod_p(0, 2), g1, mod_p(0, 3), mod_p(0, 4), mod_p(0, 5),
                ffn1_0, ffn2_0, None, tm, seq // tm)
    y_s = _post(y_s, [ypool_s, o_s], wo_list, mod_s(0, 2), g1, mod_s(0, 3), mod_s(0, 4), mod_s(0, 5),
                ffn1_0, ffn2_0, None, n_s, 1)

    g0 = row(norm_g[1, 0])
    pw1 = conv_pw1[0].astype(BF16)
    dwb, lng, lnb = row(conv_dw_b[0]), row(conv_ln_g[0]), row(conv_ln_b[0])
    cv_p, uc_tail_p = _pre1_conv_prompt(y_p, g0, mod_p(1, 0), mod_p(1, 1), pw1, conv_dw[0], dwb, lng, lnb, seq, tm)
    uc_s = _pre1(y_s, g0, mod_s(1, 0), mod_s(1, 1), pw1, n_s, 1)
    uc_s3 = uc_s.reshape(nb_s, dec_seq, d)
    conv_ext = jnp.concatenate([
        jnp.zeros((nb_s, CONV_PAD - CONV_HIST, d), F32), state_conv[0], uc_s3,
        jnp.zeros((nb_s, 8 - dec_seq, d), F32)], axis=1)
    cv_s = _conv_sample(conv_ext, conv_dw[0], dwb, lng, lnb)[:, :dec_seq].reshape(n_s, d)

    pw2 = conv_pw2[0].astype(BF16)
    g1 = row(norm_g[1, 1])
    ffn1_1, ffn2_1 = ffn_w1[1].astype(BF16), ffn_w2[1].astype(BF16)
    fg = row(final_g)
    y_p = _post(y_p, [cv_p], [pw2], mod_p(1, 2), g1, mod_p(1, 3), mod_p(1, 4), mod_p(1, 5),
                ffn1_1, ffn2_1, fg, tm, seq // tm)
    y_s = _post(y_s, [cv_s], [pw2], mod_s(1, 2), g1, mod_s(1, 3), mod_s(1, 4), mod_s(1, 5),
                ffn1_1, ffn2_1, fg, n_s, 1)

    y_prompt = y_p.reshape(nb_p, seq, d)
    y_sample = y_s.reshape(nb_s, dec_seq, d)
    kv_prompt = kvt_p.transpose(0, 1, 4, 2, 3)[None]
    kv_sample = kv_s.reshape(1, 4, nb_s, dec_seq, N_KV, HEAD_DIM)
    if seq >= wb:
        win_prompt_t = wint_p[..., seq - wb:]
    else:
        win_prompt_t = jnp.pad(wint_p, ((0, 0),) * 4 + ((wb - seq, 0),))
    win_prompt = win_prompt_t.transpose(0, 1, 4, 2, 3)
    win_sample_t = jnp.concatenate([win_t, wnew_t[..., :dec_seq]], axis=-1)[..., -wb:]
    win_sample = win_sample_t.transpose(0, 1, 4, 2, 3)

    def last_rows(x3, hist, state):
        full = x3 if state is None and x3.shape[1] >= hist else jnp.concatenate(
            [jnp.zeros((x3.shape[0], hist, x3.shape[2]), F32) if state is None else state, x3], axis=1)
        return full[:, full.shape[1] - hist:]

    pool_prompt = last_rows(u_p.reshape(nb_p, seq, pool_width), POOL_HIST, None)
    pool_sample = last_rows(u_s3, POOL_HIST, state_pool[0])
    conv_prompt = uc_tail_p[:, CONV_PAD - CONV_HIST:]
    conv_sample = last_rows(uc_s3, CONV_HIST, state_conv[0])
    return (y_prompt, y_sample, kv_prompt, kv_sample, win_prompt[None], win_sample[None],
            pool_prompt[None], pool_sample[None], conv_prompt[None], conv_sample[None])
```

```python
import functools

import numpy as np
import jax
import jax.numpy as jnp
from jax import lax
from jax.experimental import pallas as pl
from jax.experimental.pallas import tpu as pltpu

F32 = jnp.float32
BF16 = jnp.bfloat16

NORM_EPS = 1e-6
N_HEADS = 8
HEAD_DIM = 64
N_KV = 2
GROUP_R = N_HEADS // N_KV
POOL_WINDOWS = (2, 4, 8, 16)
POOL_HIST = 15
POOL_PAD = 16
CMP_BLOCK = 32
CMP_STRIDE = 16
SEL_BLOCK = 64
N_SEL = 16
WINDOW = 512
PAGE_SIZE = 128
ROPE_THETA = 10000.0
CONV_WIDTH = 31
CONV_HIST = CONV_WIDTH - 1
CONV_PAD = 32
LANES = 128
NEG = -1e9
V7X_VMEM_LIMIT = 56 * 1024 * 1024


def _cparams(*sem):
    return pltpu.CompilerParams(dimension_semantics=sem, vmem_limit_bytes=V7X_VMEM_LIMIT)


def _dot(a, b):
    return jnp.dot(a, b, preferred_element_type=F32)


def _dot_nt(a, b):
    return lax.dot_general(a, b, (((1,), (1,)), ((), ())), preferred_element_type=F32)


def _sigmoid(x):
    return 1.0 / (1.0 + jnp.exp(-x))


def _silu(x):
    return x * _sigmoid(x)


def _modulate(x, g, shift, scale):
    ms = jnp.mean(x * x, axis=-1, keepdims=True)
    y = x * lax.rsqrt(ms + NORM_EPS)
    return (y * g) * (1.0 + scale) + shift


def _ada_kernel(c_ref, w_ref, b_ref, o_ref):
    a = _silu(c_ref[...]).astype(BF16)
    o_ref[0] = _dot(a, w_ref[0].astype(BF16)) + b_ref[0]


def _ada(c_all, ada_w, ada_b):
    depth, d, n6 = ada_w.shape
    bc = c_all.shape[0]
    tn = n6 // 4
    return pl.pallas_call(
        _ada_kernel,
        out_shape=jax.ShapeDtypeStruct((depth, bc, n6), F32),
        grid=(depth, n6 // tn),
        in_specs=[
            pl.BlockSpec((bc, d), lambda l, j: (0, 0)),
            pl.BlockSpec((1, d, tn), lambda l, j: (l, 0, j)),
            pl.BlockSpec((1, 1, tn), lambda l, j: (l, 0, j)),
        ],
        out_specs=pl.BlockSpec((1, bc, tn), lambda l, j: (l, 0, j)),
        compiler_params=_cparams("parallel", "parallel"),
        name="ada_mod",
    )(c_all, ada_w, ada_b.reshape(depth, 1, n6))


def _rope(x, cos, sin_signed):
    lane = lax.broadcasted_iota(jnp.int32, x.shape, 1)
    first = (lane % HEAD_DIM) < (HEAD_DIM // 2)
    swapped = jnp.where(first, pltpu.roll(x, LANES - HEAD_DIM // 2, 1), pltpu.roll(x, HEAD_DIM // 2, 1))
    return x * cos + swapped * sin_signed


def _pre0_project(x_ref, g_ref, sh_ref, sc_ref, w_ref, cos_ref, sin_ref, u_ref, q_ref, gate_ref):
    h = _modulate(x_ref[...], g_ref[...], sh_ref[0], sc_ref[0]).astype(BF16)
    cos = cos_ref[...]
    sin = sin_ref[...]
    full = _dot_nt(h, w_ref[...])

    def proj(c0, width):
        return full[:, c0:c0 + width]

    u_ref[...] = proj(0, 512)
    for j in range(4):
        qj = _rope(proj(512 + LANES * j, LANES), cos, sin)
        q_ref[:, LANES * j:LANES * (j + 1)] = (qj * (HEAD_DIM ** -0.5)).astype(BF16)
    gate_ref[...] = _sigmoid(proj(1792, LANES))
    kc = _rope(proj(1024, LANES), cos, sin)
    vc = proj(1152, LANES)
    ks = _rope(proj(1280, LANES), cos, sin)
    vs = proj(1408, LANES)
    kw = _rope(proj(1536, LANES), cos, sin)
    vw = proj(1664, LANES)
    return kc, vc, ks, vs, kw, vw


def _pre0_rows_kernel(x_ref, g_ref, sh_ref, sc_ref, w_ref, cos_ref, sin_ref,
                      u_ref, q_ref, gate_ref, kv_ref, win_ref):
    kc, vc, ks, vs, kw, vw = _pre0_project(x_ref, g_ref, sh_ref, sc_ref, w_ref, cos_ref, sin_ref,
                                           u_ref, q_ref, gate_ref)
    for k, seg in enumerate((kc, vc, ks, vs)):
        kv_ref[k] = seg
    win_ref[0] = kw
    win_ref[1] = vw


def _store_transposed(x, *targets):
    xt = x.T
    for ref, lead in targets:
        for g in range(N_KV):
            ref[lead + (g,)] = xt[HEAD_DIM * g:HEAD_DIM * (g + 1), :].astype(ref.dtype)


def _pre0_seq_kernel(x_ref, g_ref, sh_ref, sc_ref, w_ref, cos_ref, sin_ref,
                     u_ref, q_ref, gate_ref, kvt_ref, wint_ref, kvbt_ref, cmp_ref):
    kc, vc, ks, vs, kw, vw = _pre0_project(x_ref, g_ref, sh_ref, sc_ref, w_ref, cos_ref, sin_ref,
                                           u_ref, q_ref, gate_ref)
    _store_transposed(kc, (kvt_ref, (0, 0)))
    _store_transposed(vc, (kvt_ref, (1, 0)))
    _store_transposed(ks, (kvt_ref, (2, 0)), (kvbt_ref, (0, 0)))
    _store_transposed(vs, (kvt_ref, (3, 0)), (kvbt_ref, (1, 0)))
    _store_transposed(kw, (wint_ref, (0, 0)), (kvbt_ref, (2, 0)))
    _store_transposed(vw, (wint_ref, (1, 0)), (kvbt_ref, (3, 0)))
    cmp_ref[0] = kc
    cmp_ref[1] = vc


def _pre0(x2, g, shift, scale, w_pad, cos, sin, tm, rows_per_mod, pos_tiles, seq=None):
    n, d = x2.shape
    r = shift.shape[1]
    mod_spec = pl.BlockSpec((1, r, d), lambda i: (i // rows_per_mod, 0, 0))
    pos_spec = pl.BlockSpec((tm, LANES), lambda i: (i % pos_tiles, 0))
    row_spec = lambda w: pl.BlockSpec((tm, w), lambda i: (i, 0))
    out_shape = [jax.ShapeDtypeStruct((n, 512), F32), jax.ShapeDtypeStruct((n, 512), BF16),
                 jax.ShapeDtypeStruct((n, LANES), F32)]
    out_specs = [row_spec(512), row_spec(512), row_spec(LANES)]
    if seq is None:
        body = _pre0_rows_kernel
        out_shape += [jax.ShapeDtypeStruct((4, n, LANES), F32), jax.ShapeDtypeStruct((2, n, LANES), F32)]
        out_specs += [pl.BlockSpec((4, tm, LANES), lambda i: (0, i, 0)),
                      pl.BlockSpec((2, tm, LANES), lambda i: (0, i, 0))]
    else:
        body = _pre0_seq_kernel
        nb, tpb = n // seq, seq // tm
        t_spec = lambda k: pl.BlockSpec((k, 1, N_KV, HEAD_DIM, tm), lambda i: (0, i // tpb, 0, 0, i % tpb))
        out_shape += [jax.ShapeDtypeStruct((4, nb, N_KV, HEAD_DIM, seq), F32),
                      jax.ShapeDtypeStruct((2, nb, N_KV, HEAD_DIM, seq), F32),
                      jax.ShapeDtypeStruct((4, nb, N_KV, HEAD_DIM, seq), BF16),
                      jax.ShapeDtypeStruct((2, n, LANES), F32)]
        out_specs += [t_spec(4), t_spec(2), t_spec(4), pl.BlockSpec((2, tm, LANES), lambda i: (0, i, 0))]
    return pl.pallas_call(
        body,
        out_shape=tuple(out_shape),
        grid=(n // tm,),
        in_specs=[
            pl.BlockSpec((tm, d), lambda i: (i, 0)),
            pl.BlockSpec((1, d), lambda i: (0, 0)),
            mod_spec, mod_spec,
            pl.BlockSpec(w_pad.shape, lambda i: (0, 0)),
            pos_spec, pos_spec,
        ],
        out_specs=tuple(out_specs),
        compiler_params=_cparams("parallel"),
        name="pre0_in_proj",
    )(x2, g, shift, scale, w_pad, cos, sin)


def _rope_tables(pos):
    half = HEAD_DIM // 2
    inv = ROPE_THETA ** (-np.arange(half, dtype=np.float64) / half)
    ang = np.asarray(pos, np.float64)[:, None] * inv[None, :]
    cos, sin = np.cos(ang), np.sin(ang)
    cos_t = np.tile(np.concatenate([cos, cos], axis=1), (1, LANES // HEAD_DIM))
    sin_t = np.tile(np.concatenate([-sin, sin], axis=1), (1, LANES // HEAD_DIM))
    return jnp.asarray(cos_t, F32), jnp.asarray(sin_t, F32)


def _pool_body(ext_ref, t_rows, pos0, w_ref, scale_ref, out_ref):
    pos = pos0 + lax.broadcasted_iota(jnp.int32, (t_rows, 1), 0)
    for g, w in enumerate(POOL_WINDOWS):
        cols = slice(LANES * g, LANES * (g + 1))
        x = ext_ref[POOL_PAD:POOL_PAD + t_rows, cols]
        s = x
        for j in range(1, w):
            s = s + ext_ref[POOL_PAD - j:POOL_PAD - j + t_rows, cols]
        cnt = jnp.minimum(pos + 1, w).astype(F32)
        dlt = (s / cnt - x).astype(BF16)
        y = _dot(dlt, w_ref[g]) * scale_ref[:, cols]
        out_ref[:, cols] = y.astype(out_ref.dtype)


def _pool_prompt_kernel(x_ref, prev_ref, w_ref, scale_ref, out_ref, ext_ref, *, tiles_per_batch, tm):
    t_in_b = pl.program_id(0) % tiles_per_batch
    ext_ref[0:POOL_PAD, :] = jnp.where(t_in_b > 0, prev_ref[...], 0.0)
    ext_ref[POOL_PAD:, :] = x_ref[...]
    _pool_body(ext_ref, tm, t_in_b * tm, w_ref, scale_ref, out_ref)


def _pool_prompt(u, w_grp, scale, seq, tm):
    n, c = u.shape
    tpb = seq // tm
    hb = tm // POOL_PAD
    return pl.pallas_call(
        functools.partial(_pool_prompt_kernel, tiles_per_batch=tpb, tm=tm),
        out_shape=jax.ShapeDtypeStruct((n, c), BF16),
        grid=(n // tm,),
        in_specs=[
            pl.BlockSpec((tm, c), lambda i: (i, 0)),
            pl.BlockSpec((POOL_PAD, c), lambda i: (jnp.maximum(i * hb - 1, 0), 0)),
            pl.BlockSpec(w_grp.shape, lambda i: (0, 0, 0)),
            pl.BlockSpec((1, c), lambda i: (0, 0)),
        ],
        out_specs=pl.BlockSpec((tm, c), lambda i: (i, 0)),
        scratch_shapes=[pltpu.VMEM((POOL_PAD + tm, c), F32)],
        compiler_params=_cparams("parallel"),
        name="pool_prompt",
    )(u, u, w_grp, scale)


def _pool_sample_kernel(ext_ref, w_ref, scale_ref, out_ref, *, pos0, t_rows):
    _pool_body(ext_ref.at[0], t_rows, pos0, w_ref, scale_ref, out_ref.at[0])


def _pool_sample(ext, w_grp, scale, pos0):
    b, rows, c = ext.shape
    t_rows = rows - POOL_PAD
    return pl.pallas_call(
        functools.partial(_pool_sample_kernel, pos0=pos0, t_rows=t_rows),
        out_shape=jax.ShapeDtypeStruct((b, t_rows, c), BF16),
        grid=(b,),
        in_specs=[
            pl.BlockSpec((1, rows, c), lambda i: (i, 0, 0)),
            pl.BlockSpec(w_grp.shape, lambda i: (0, 0, 0)),
            pl.BlockSpec((1, c), lambda i: (0, 0)),
        ],
        out_specs=pl.BlockSpec((1, t_rows, c), lambda i: (i, 0, 0)),
        compiler_params=_cparams("parallel"),
        name="pool_sample",
    )(ext, w_grp, scale)


def _chunk_rows(ref2d, rows):
    n = rows // CMP_STRIDE
    return jnp.concatenate([ref2d[pl.ds(r, n, stride=CMP_STRIDE), :] for r in range(CMP_STRIDE)], axis=1)


def _cmp_partial(a, pos_ref, w_ref, p_ref):
    hid2 = w_ref.shape[2] // 2
    p_ref[0, 0, :, 0:hid2] = _dot((a + pos_ref[0, 0:1, :]).astype(BF16), w_ref[0, :, 0:hid2])
    p_ref[0, 0, :, hid2:] = _dot((a + pos_ref[0, 1:2, :]).astype(BF16), w_ref[0, :, hid2:])


def _cmp_rows_kernel(x_ref, pos_ref, w_ref, p_ref, *, rows):
    _cmp_partial(_chunk_rows(x_ref.at[0, 0], rows), pos_ref, w_ref, p_ref)


def _cmp_rows(x4, pos_ab, w1_ab, nbatch, rows):
    nch = rows // CMP_STRIDE
    return pl.pallas_call(
        functools.partial(_cmp_rows_kernel, rows=rows),
        out_shape=jax.ShapeDtypeStruct((2, nbatch, nch, w1_ab.shape[2]), F32),
        grid=(2, nbatch),
        in_specs=[
            pl.BlockSpec((1, 1, rows, LANES), lambda s, b: (s, b, 0, 0)),
            pl.BlockSpec((1, 2, pos_ab.shape[2]), lambda s, b: (s, 0, 0)),
            pl.BlockSpec((1,) + w1_ab.shape[1:], lambda s, b: (s, 0, 0)),
        ],
        out_specs=pl.BlockSpec((1, 1, nch, w1_ab.shape[2]), lambda s, b: (s, b, 0, 0)),
        compiler_params=_cparams("parallel", "parallel"),
        name="cmp_rows",
    )(x4, pos_ab, w1_ab)


PAGES_PER_STEP = 64


def _cmp_pages_kernel(pt_ref, *refs, pps):
    del pt_ref
    page_refs = refs[:pps]
    pos_ref, w_ref, p_ref, rows_ref = refs[pps:]
    for k, r in enumerate(page_refs):
        page_t = jnp.concatenate([r[0, 0, g] for g in range(N_KV)], axis=0)
        rows_ref[k * PAGE_SIZE:(k + 1) * PAGE_SIZE, :] = page_t.T
    _cmp_partial(_chunk_rows(rows_ref, pps * PAGE_SIZE), pos_ref, w_ref, p_ref)


def _cmp_pages(cache_t, page_table_flat, n_pages, pos_ab, w1_ab, nbatch):
    pps = min(PAGES_PER_STEP, n_pages)
    assert n_pages % pps == 0
    steps = n_pages // pps
    nch = pps * PAGE_SIZE // CMP_STRIDE

    def page_spec(k):
        return pl.BlockSpec(
            (1, 1, N_KV, HEAD_DIM, PAGE_SIZE),
            lambda s, b, j, pt: (s, pt[b * n_pages + j * pps + k], 0, 0, 0))

    grid_spec = pltpu.PrefetchScalarGridSpec(
        num_scalar_prefetch=1,
        grid=(2, nbatch, steps),
        in_specs=[page_spec(k) for k in range(pps)] + [
            pl.BlockSpec((1, 2, pos_ab.shape[2]), lambda s, b, j, pt: (s, 0, 0)),
            pl.BlockSpec((1,) + w1_ab.shape[1:], lambda s, b, j, pt: (s, 0, 0)),
        ],
        out_specs=pl.BlockSpec((1, 1, nch, w1_ab.shape[2]), lambda s, b, j, pt: (s, b, j, 0)),
        scratch_shapes=[pltpu.VMEM((pps * PAGE_SIZE, LANES), F32)],
    )
    return pl.pallas_call(
        functools.partial(_cmp_pages_kernel, pps=pps),
        out_shape=jax.ShapeDtypeStruct((2, nbatch, steps * nch, w1_ab.shape[2]), F32),
        grid_spec=grid_spec,
        compiler_params=_cparams("parallel", "parallel", "parallel"),
        name="cmp_pages",
    )(page_table_flat, *([cache_t] * pps), pos_ab, w1_ab)


def _gelu_tanh(x):
    return x * (0.5 * (1.0 + jnp.tanh(np.sqrt(2.0 / np.pi).astype(np.float32) * (x + 0.044715 * (x * x * x)))))


def _cmp_combine_kernel(*refs, n_parts):
    p_refs = refs[:n_parts]
    w2_ref, w2t_ref, o_ref, ot_ref, pa_ref, pb_ref = refs[n_parts:]
    hid2 = pa_ref.shape[1]
    n_p = pa_ref.shape[0]
    r0 = 0
    for p_ref in p_refs:
        rows = p_ref.shape[2]
        pa_ref[r0:r0 + rows, :] = p_ref[0, 0, :, 0:hid2]
        pb_ref[r0:r0 + rows, :] = p_ref[0, 0, :, hid2:]
        r0 += rows
    pb_ref[n_p:n_p + 8, :] = jnp.zeros((8, hid2), F32)
    hsum = pa_ref[...] + pb_ref[1:n_p + 1, :]
    act = _gelu_tanh(hsum).astype(BF16)
    o_ref[0, 0] = _dot(act, w2_ref[0]).astype(o_ref.dtype)
    ot_ref[0, 0] = _dot_nt(w2t_ref[0], act).astype(ot_ref.dtype)


def _cmp_combine(parts, w2_bd):
    nbatch, width = parts[0].shape[1], parts[0].shape[3]
    n_p = sum(p.shape[2] for p in parts)
    w2t_bd = w2_bd.transpose(0, 2, 1)
    return pl.pallas_call(
        functools.partial(_cmp_combine_kernel, n_parts=len(parts)),
        out_shape=(jax.ShapeDtypeStruct((2, nbatch, n_p, LANES), BF16),
                   jax.ShapeDtypeStruct((2, nbatch, LANES, n_p), BF16)),
        grid=(2, nbatch),
        in_specs=[pl.BlockSpec((1, 1, p.shape[2], width), lambda s, b: (s, b, 0, 0)) for p in parts] + [
            pl.BlockSpec((1,) + w2_bd.shape[1:], lambda s, b: (s, 0, 0)),
            pl.BlockSpec((1,) + w2t_bd.shape[1:], lambda s, b: (s, 0, 0)),
        ],
        out_specs=(pl.BlockSpec((1, 1, n_p, LANES), lambda s, b: (s, b, 0, 0)),
                   pl.BlockSpec((1, 1, LANES, n_p), lambda s, b: (s, b, 0, 0))),
        scratch_shapes=[pltpu.VMEM((n_p, width // 2), F32), pltpu.VMEM((n_p + 8, width // 2), F32)],
        compiler_params=_cparams("parallel", "parallel"),
        name="cmp_combine",
    )(*parts, w2_bd, w2t_bd)


def _cmp_weights(cmp_pos, cmp_w1, cmp_w2):
    hid = cmp_w1.shape[2]
    half = CMP_STRIDE * HEAD_DIM
    eye = jnp.eye(N_KV, dtype=F32)
    pos_ab = jnp.tile(cmp_pos.reshape(2, 2, CMP_STRIDE, 1, HEAD_DIM), (1, 1, 1, N_KV, 1))
    pos_ab = pos_ab.reshape(2, 2, CMP_STRIDE * N_KV * HEAD_DIM)
    w1 = cmp_w1.reshape(2, 2, CMP_STRIDE, HEAD_DIM, hid)
    w1_bd = jnp.einsum("shrdj,gk->shrgdkj", w1, eye)
    w1_bd = w1_bd.reshape(2, 2, CMP_STRIDE * N_KV * HEAD_DIM, N_KV * hid)
    w1_ab = jnp.concatenate([w1_bd[:, 0], w1_bd[:, 1]], axis=2).astype(BF16)
    w2_bd = jnp.einsum("sjd,gk->sgjkd", cmp_w2, eye).reshape(2, N_KV * hid, N_KV * HEAD_DIM).astype(BF16)
    del half
    return pos_ab, w1_ab, w2_bd


def _online_update(s, v_aug_t, state):
    tq, ck = s.shape
    m_new = jnp.broadcast_to(jnp.max(s, axis=-1, keepdims=True), (tq, LANES))
    if state is not None:
        acc_old, m_old = state
        m_new = jnp.maximum(m_old, m_new)
    p = jnp.exp(s - pltpu.repeat(m_new, ck // LANES, axis=1)).astype(BF16)
    acc = _dot_nt(p, v_aug_t)
    if state is not None:
        acc = jnp.exp(m_old - m_new) * acc_old + acc
    return acc, m_new


def _attn_prompt_kernel(q_ref, cmpt_ref, kst_ref, vst_ref, kwt_ref, vwt_ref, gate_ref, mt_ref, e_ref,
                        o_ref, kaug_s, vaug_s, vwaug_s, *, tq, nblk):
    i = pl.program_id(1)
    s0 = i * tq
    seq = kst_ref.shape[4]
    ncmp = cmpt_ref.shape[3]

    @pl.when(i == 0)
    def _():
        ones_row = (lax.broadcasted_iota(jnp.int32, (HEAD_DIM, seq), 0) == 0).astype(BF16)
        for g in range(N_KV):
            kaug_s[g, 0:LANES, :] = e_ref[...]
            kaug_s[g, LANES:LANES + HEAD_DIM, :] = kst_ref[0, 0, g]
            vaug_s[g, 0:HEAD_DIM, :] = vst_ref[0, 0, g]
            vaug_s[g, HEAD_DIM:, :] = ones_row
            vwaug_s[g, 0:HEAD_DIM, :] = vwt_ref[0, 0, g]
            vwaug_s[g, HEAD_DIM:, :] = ones_row

    row = lax.broadcasted_iota(jnp.int32, (tq, 1), 0)
    col = lax.broadcasted_iota(jnp.int32, (1, tq), 1)
    qpos = s0 + row
    gates = gate_ref[...]
    c_end = lax.broadcasted_iota(jnp.int32, (1, ncmp), 1) * CMP_STRIDE + (CMP_BLOCK - 1)
    mask_c = c_end <= qpos
    qp_l = s0 + col
    cur = qp_l // SEL_BLOCK
    n_rb = -(-nblk // 8)
    qs = [q_ref[:, HEAD_DIM * hh:HEAD_DIM * (hh + 1)] for hh in range(N_HEADS)]

    o_cmp, q_aug = [], []
    for g in range(N_KV):
        kct_g = cmpt_ref[0, 0, HEAD_DIM * g:HEAD_DIM * (g + 1), :]
        vct_g = cmpt_ref[1, 0, HEAD_DIM * g:HEAD_DIM * (g + 1), :]
        p_sum = jnp.zeros((tq, ncmp), F32)
        for qh in qs[GROUP_R * g:GROUP_R * (g + 1)]:
            s = jnp.where(mask_c, _dot(qh, kct_g), -jnp.inf)
            m = jnp.max(s, axis=-1, keepdims=True)
            m = jnp.where(m == -jnp.inf, 0.0, m)
            e = jnp.exp(s - m)
            d = jnp.sum(e, axis=-1, keepdims=True)
            p = e / jnp.where(d > 0, d, 1.0)
            p_sum = p_sum + p
            o_cmp.append(_dot_nt(p.astype(BF16), vct_g))

        p_hi = p_sum.astype(BF16)
        p_lo = (p_sum - p_hi.astype(F32)).astype(BF16)
        imp = _dot_nt(mt_ref[...], p_hi) + _dot_nt(mt_ref[...], p_lo)
        score, valid = [], []
        for r in range(n_rb):
            jb = 8 * r + lax.broadcasted_iota(jnp.int32, (8, 1), 0)
            ok = (jb * SEL_BLOCK <= qp_l) & (jb < nblk)
            forced = (jb == 0) | (jb == cur) | (jb == cur - 1)
            valid.append(ok)
            score.append(jnp.where(ok, jnp.where(forced, jnp.inf, imp[8 * r:8 * r + 8, :]), -jnp.inf))
        cnt = [jnp.zeros((8, tq), jnp.int32) for _ in range(n_rb)]
        for j in range(nblk):
            rj = jnp.broadcast_to(score[j // 8][j % 8:j % 8 + 1, :], (8, tq))
            for r in range(n_rb):
                if 8 * r > j:
                    beats = rj >= score[r]
                elif 8 * r + 7 < j:
                    beats = rj > score[r]
                else:
                    later = 8 * r + lax.broadcasted_iota(jnp.int32, (8, 1), 0) > j
                    beats = (rj > score[r]) | ((rj == score[r]) & later)
                cnt[r] = cnt[r] + jnp.where(beats, 1, 0)
        selneg_t = [jnp.where((cnt[r] < N_SEL) & valid[r], 0.0, NEG) for r in range(n_rb)]
        selneg_t.append(jnp.zeros((LANES - 8 * n_rb, tq), F32))
        selneg = jnp.concatenate(selneg_t, axis=0).T.astype(BF16)
        q_aug += [jnp.concatenate([selneg, qh], axis=1) for qh in qs[GROUP_R * g:GROUP_R * (g + 1)]]

    ck = 2 * tq

    def sel_chunk(c0, state, keep):
        out = []
        for g in range(N_KV):
            k_aug = kaug_s[g, :, pl.ds(c0, ck)]
            v_aug = vaug_s[g, :, pl.ds(c0, ck)]
            for hh in range(GROUP_R * g, GROUP_R * (g + 1)):
                s = _dot(q_aug[hh], k_aug)
                if keep is not None:
                    s = jnp.where(keep, s, NEG)
                out.append(_online_update(s, v_aug, None if state is None else state[hh]))
        return tuple(out)

    n_full = i // 2
    last0 = pl.multiple_of(jnp.maximum(i - 1, 0) * tq, tq)
    kpos = last0 + lax.broadcasted_iota(jnp.int32, (1, ck), 1)
    state = sel_chunk(last0, None, (kpos <= qpos) & (kpos >= n_full * ck))
    state = lax.fori_loop(0, n_full, lambda c, st: sel_chunk(pl.multiple_of(c * ck, ck), st, None), state)
    o_sel = [acc[:, 0:HEAD_DIM] / acc[:, HEAD_DIM:HEAD_DIM + 1] for acc, _ in state]

    keep_w = (kpos <= qpos) & (kpos > qpos - WINDOW)
    wstate = []
    for g in range(N_KV):
        k_t = kwt_ref[0, 0, g, :, pl.ds(last0, ck)]
        v_aug = vwaug_s[g, :, pl.ds(last0, ck)]
        for hh in range(GROUP_R * g, GROUP_R * (g + 1)):
            wstate.append(_online_update(jnp.where(keep_w, _dot(qs[hh], k_t), NEG), v_aug, None))
    for hh in range(N_HEADS):
        acc_w = wstate[hh][0]
        o_win = acc_w[:, 0:HEAD_DIM] / acc_w[:, HEAD_DIM:HEAD_DIM + 1]
        o = (o_cmp[hh] * gates[:, 3 * hh:3 * hh + 1] + o_sel[hh] * gates[:, 3 * hh + 1:3 * hh + 2]
             + o_win * gates[:, 3 * hh + 2:3 * hh + 3])
        o_ref[:, HEAD_DIM * hh:HEAD_DIM * (hh + 1)] = o.astype(o_ref.dtype)


def _attn_prompt(q, cmp_t, kvbt, gates, mt, e_mat, nbatch, seq, tq):
    assert tq >= WINDOW and seq >= 2 * tq
    n = q.shape[0]
    nq = seq // tq
    ncmp = cmp_t.shape[3]
    kv_spec = lambda slot: pl.BlockSpec((1, 1, N_KV, HEAD_DIM, seq), lambda b, i: (slot, b, 0, 0, 0))
    return pl.pallas_call(
        functools.partial(_attn_prompt_kernel, tq=tq, nblk=seq // SEL_BLOCK),
        out_shape=jax.ShapeDtypeStruct((n, N_HEADS * HEAD_DIM), BF16),
        grid=(nbatch, nq),
        in_specs=[
            pl.BlockSpec((tq, N_HEADS * HEAD_DIM), lambda b, i: (b * nq + i, 0)),
            pl.BlockSpec((2, 1, LANES, ncmp), lambda b, i: (0, b, 0, 0)),
            kv_spec(0), kv_spec(1), kv_spec(2), kv_spec(3),
            pl.BlockSpec((tq, LANES), lambda b, i: (b * nq + i, 0)),
            pl.BlockSpec(mt.shape, lambda b, i: (0, 0)),
            pl.BlockSpec(e_mat.shape, lambda b, i: (0, 0)),
        ],
        out_specs=pl.BlockSpec((tq, N_HEADS * HEAD_DIM), lambda b, i: (b * nq + i, 0)),
        scratch_shapes=[
            pltpu.VMEM((N_KV, LANES + HEAD_DIM, seq), BF16),
            pltpu.VMEM((N_KV, LANES, seq), BF16),
            pltpu.VMEM((N_KV, LANES, seq), BF16),
        ],
        compiler_params=_cparams("arbitrary", "arbitrary"),
        name="attn_prompt",
    )(q, cmp_t, kvbt, kvbt, kvbt, kvbt, gates, mt, e_mat)


def _imp_matrix(ncmp, nblk_pad):
    per = SEL_BLOCK // CMP_STRIDE
    pad = CMP_BLOCK // CMP_STRIDE - 1
    n = np.arange(ncmp)[:, None]
    j = np.arange(nblk_pad)[None, :]
    return ((n >= per * j - pad) & (n <= per * j + per - 1)).astype(np.float32)


def _attn_sample_cmp_kernel(q_ref, kc_ref, vc_ref, m_ref, oc_ref, idx_ref, *, n_valid_cmp, qpos0, nblk):
    ncmp = kc_ref.shape[2]
    rows = q_ref.shape[2]
    t_row = lax.broadcasted_iota(jnp.int32, (rows, 1), 0) % 8
    c_idx = lax.broadcasted_iota(jnp.int32, (1, ncmp), 1)
    mask_c = (c_idx * CMP_STRIDE + (CMP_BLOCK - 1) <= qpos0 + t_row) & (c_idx < n_valid_cmp)
    p_tok = []
    for g in range(N_KV):
        gs = slice(HEAD_DIM * g, HEAD_DIM * (g + 1))
        s = jnp.where(mask_c, _dot_nt(q_ref[0, g], kc_ref[0, 0, :, gs]), -jnp.inf)
        m = jnp.max(s, axis=-1, keepdims=True)
        m = jnp.where(m == -jnp.inf, 0.0, m)
        e = jnp.exp(s - m)
        d = jnp.sum(e, axis=-1, keepdims=True)
        p = e / jnp.where(d > 0, d, 1.0)
        oc_ref[0, g] = _dot(p.astype(BF16), vc_ref[0, 0, :, gs])
        p_tok.append(jnp.sum(p.reshape(GROUP_R, 8, ncmp), axis=0))
    p_all = jnp.concatenate(p_tok, axis=0)
    p_hi = p_all.astype(BF16)
    p_lo = (p_all - p_hi.astype(F32)).astype(BF16)
    imp = _dot(p_hi, m_ref[...]) + _dot(p_lo, m_ref[...])
    nb_pad = imp.shape[1]
    jb = lax.broadcasted_iota(jnp.int32, (1, nb_pad), 1)
    qpos = qpos0 + lax.broadcasted_iota(jnp.int32, (2 * 8, 1), 0) % 8
    cur = qpos // SEL_BLOCK
    valid = (jb * SEL_BLOCK <= qpos) & (jb < nblk)
    forced = (jb == 0) | (jb == cur) | (jb == cur - 1)
    score = jnp.where(valid, jnp.where(forced, jnp.inf, imp), -jnp.inf)
    cnt = jnp.zeros(score.shape, jnp.int32)
    for j in range(nblk):
        sj = score[:, j:j + 1]
        beats = (sj > score) | ((sj == score) & (jb > j))
        cnt = cnt + jnp.where(beats, 1, 0)
    lane = lax.broadcasted_iota(jnp.int32, (2 * 8, LANES), 1)
    picked = jnp.zeros((2 * 8, LANES), jnp.int32)
    jb_f = jb.astype(F32)
    for k in range(N_SEL):
        hit = (cnt == k) & (jb < nblk)
        idx = jnp.sum(jnp.where(hit, jb_f, 0.0), axis=-1, keepdims=True).astype(jnp.int32)
        picked = jnp.where(lane == k, idx, picked)
    idx_ref[0] = picked


def _attn_sample_cmp(qg, cmp_kv, m_mat, n_valid_cmp, qpos0, nblk):
    nbatch = qg.shape[0]
    ncmp = cmp_kv.shape[2]
    rows = qg.shape[2]
    cmp_spec = lambda slot: pl.BlockSpec((1, 1, ncmp, LANES), lambda b: (slot, b, 0, 0))
    return pl.pallas_call(
        functools.partial(_attn_sample_cmp_kernel, n_valid_cmp=n_valid_cmp, qpos0=qpos0, nblk=nblk),
        out_shape=(
            jax.ShapeDtypeStruct((nbatch, N_KV, rows, HEAD_DIM), F32),
            jax.ShapeDtypeStruct((nbatch, 2 * 8, LANES), jnp.int32),
        ),
        grid=(nbatch,),
        in_specs=[
            pl.BlockSpec((1, N_KV, rows, HEAD_DIM), lambda b: (b, 0, 0, 0)),
            cmp_spec(0), cmp_spec(1),
            pl.BlockSpec(m_mat.shape, lambda b: (0, 0)),
        ],
        out_specs=(
            pl.BlockSpec((1, N_KV, rows, HEAD_DIM), lambda b: (b, 0, 0, 0)),
            pl.BlockSpec((1, 2 * 8, LANES), lambda b: (b, 0, 0)),
        ),
        compiler_params=_cparams("parallel"),
        name="attn_sample_cmp",
    )(qg, cmp_kv, cmp_kv, m_mat)


def _masked_softmax(s, mask):
    s = jnp.where(mask, s, -jnp.inf)
    m = jnp.max(s, axis=-1, keepdims=True)
    m = jnp.where(m == -jnp.inf, 0.0, m)
    e = jnp.exp(s - m)
    d = jnp.sum(e, axis=-1, keepdims=True)
    return e / jnp.where(d > 0, d, 1.0)


def _attn_sample_sel_kernel(idx_ref, pt_ref, *refs, dec_seq, qpos0, n_cache_blk, wb):
    del pt_ref
    nsel = N_KV * N_SEL
    per_page = PAGE_SIZE // SEL_BLOCK
    kv_refs = refs[:nsel]
    q_ref, tail_ref, kw_ref, vw_ref, wnew_ref, oc_ref, gate_ref, o_ref, osel_ref = refs[nsel:]
    b = pl.program_id(0)
    t = pl.program_id(1)
    rows = q_ref.shape[2]
    t_row = lax.broadcasted_iota(jnp.int32, (rows, 1), 0) % 8
    qpos = qpos0 + t_row
    colk = lax.broadcasted_iota(jnp.int32, (1, N_SEL * PAGE_SIZE), 1)
    slot_of_col = colk // PAGE_SIZE
    row_in_page = colk % PAGE_SIZE

    @pl.when(t == 0)
    def _():
        osel_ref[...] = jnp.zeros(osel_ref.shape, F32)

    for g in range(N_KV):
        blk_of_col = jnp.zeros_like(colk)
        k_pages, v_pages = [], []
        for k in range(N_SEL):
            blk = idx_ref[((b * dec_seq + t) * N_KV + g) * N_SEL + k]
            is_new = blk >= n_cache_blk
            k_pages.append(jnp.where(is_new, tail_ref[0, 0, g], kv_refs[g * N_SEL + k][0, 0, 0]))
            v_pages.append(jnp.where(is_new, tail_ref[1, 0, g], kv_refs[g * N_SEL + k][1, 0, 0]))
            blk_of_col = blk_of_col + jnp.where(slot_of_col == k, blk, 0)
        in_block = row_in_page // SEL_BLOCK == blk_of_col % per_page
        kpos = blk_of_col * SEL_BLOCK + row_in_page % SEL_BLOCK
        k_t = jnp.concatenate(k_pages, axis=1).astype(BF16)
        v_t = jnp.concatenate(v_pages, axis=1).astype(BF16)
        p = _masked_softmax(_dot(q_ref[0, g], k_t), in_block & (kpos <= qpos))
        o = _dot_nt(p.astype(BF16), v_t)
        osel_ref[g] = osel_ref[g] + jnp.where(t_row == t, o, 0.0)

    @pl.when(t == dec_seq - 1)
    def _():
        colw = lax.broadcasted_iota(jnp.int32, (1, wb + LANES), 1)
        kwpos = jnp.where(colw < wb, qpos0 - wb + colw, qpos0 + colw - wb)
        dpos = qpos - kwpos
        mask_w = (dpos >= 0) & (dpos < WINDOW) & (kwpos >= 0) & (colw < wb + dec_seq)
        for g in range(N_KV):
            k_t = jnp.concatenate([kw_ref[0, 0, g], wnew_ref[0, 0, g]], axis=1).astype(BF16)
            v_t = jnp.concatenate([vw_ref[0, 0, g], wnew_ref[1, 0, g]], axis=1).astype(BF16)
            p = _masked_softmax(_dot(q_ref[0, g], k_t), mask_w)
            o_win = _dot_nt(p.astype(BF16), v_t)
            gt = gate_ref[0, g]
            o_ref[0, g] = oc_ref[0, g] * gt[:, 0:1] + osel_ref[g] * gt[:, 1:2] + o_win * gt[:, 2:3]


def _attn_sample_sel(idx_flat, pt_flat, cache_t, qg, tail_t, win_t, wnew_t, o_cmp, gates_g, dec_seq, n_pages,
                     qpos0):
    nbatch = qg.shape[0]
    rows = qg.shape[2]
    wb = win_t.shape[4]
    per_page = PAGE_SIZE // SEL_BLOCK
    n_cache_blk = n_pages * per_page

    def page_spec(g, k):
        def imap(b, t, idx, pt):
            blk = jnp.minimum(idx[((b * dec_seq + t) * N_KV + g) * N_SEL + k], n_cache_blk - 1)
            return (1, pt[b * n_pages + blk // per_page], g, 0, 0)
        return pl.BlockSpec((2, 1, 1, HEAD_DIM, PAGE_SIZE), imap)

    kv_specs = [page_spec(g, k) for g in range(N_KV) for k in range(N_SEL)]
    per_b = lambda shape: pl.BlockSpec((1,) + shape, lambda b, t, idx, pt: (b,) + (0,) * len(shape))
    new_spec = pl.BlockSpec((2, 1, N_KV, HEAD_DIM, LANES), lambda b, t, idx, pt: (0, b, 0, 0, 0))
    win_spec = lambda s: pl.BlockSpec((1, 1, N_KV, HEAD_DIM, wb), lambda b, t, idx, pt: (s, b, 0, 0, 0))
    grid_spec = pltpu.PrefetchScalarGridSpec(
        num_scalar_prefetch=2,
        grid=(nbatch, dec_seq),
        in_specs=kv_specs + [
            per_b((N_KV, rows, HEAD_DIM)),
            new_spec, win_spec(0), win_spec(1), new_spec,
            per_b((N_KV, rows, HEAD_DIM)),
            per_b((N_KV, rows, LANES)),
        ],
        out_specs=per_b((N_KV, rows, HEAD_DIM)),
        scratch_shapes=[pltpu.VMEM((N_KV, rows, HEAD_DIM), F32)],
    )
    n_page_specs = N_KV * N_SEL
    return pl.pallas_call(
        functools.partial(_attn_sample_sel_kernel, dec_seq=dec_seq, qpos0=qpos0, n_cache_blk=n_cache_blk, wb=wb),
        out_shape=jax.ShapeDtypeStruct((nbatch, N_KV, rows, HEAD_DIM), F32),
        grid_spec=grid_spec,
        compiler_params=_cparams("parallel", "arbitrary"),
        name="attn_sample_sel",
    )(idx_flat, pt_flat, *([cache_t] * n_page_specs), qg, tail_t, win_t, win_t, wnew_t, o_cmp, gates_g)


FFN_CHUNK = 256


def _post_kernel(*refs, n_a, final):
    y_ref = refs[0]
    a_refs = refs[1:1 + 2 * n_a]
    gmix_ref, g2_ref, sh_ref, sc_ref, gffn_ref, w1_ref, w2_ref = refs[1 + 2 * n_a:8 + 2 * n_a]
    rest = refs[8 + 2 * n_a:]
    out_ref = rest[-1]
    mix = _dot(a_refs[0][...], a_refs[1][...])
    for k in range(1, n_a):
        mix = mix + _dot(a_refs[2 * k][...], a_refs[2 * k + 1][...])
    y1 = y_ref[...] + gmix_ref[0] * mix
    h = _modulate(y1, g2_ref[...], sh_ref[0], sc_ref[0]).astype(BF16)
    d_ff = w2_ref.shape[0]
    acc = jnp.zeros(y1.shape, F32)
    for c in range(d_ff // FFN_CHUNK):
        c0 = c * FFN_CHUNK
        gate = _dot(h, w1_ref[:, c0:c0 + FFN_CHUNK])
        up = _dot(h, w1_ref[:, d_ff + c0:d_ff + c0 + FFN_CHUNK])
        acc = acc + _dot((_silu(gate) * up).astype(BF16), w2_ref[c0:c0 + FFN_CHUNK, :])
    y2 = y1 + gffn_ref[0] * acc
    if final:
        fg_ref = rest[0]
        ms = jnp.mean(y2 * y2, axis=-1, keepdims=True)
        y2 = (y2 * lax.rsqrt(ms + NORM_EPS)) * fg_ref[...]
    out_ref[...] = y2


def _post(y, a_list, wo_list, gmix, g2, shift, scale, gffn, w1, w2, final_g, tm, rows_per_mod):
    n, d = y.shape
    r = shift.shape[1]
    mod_spec = pl.BlockSpec((1, r, d), lambda i: (i // rows_per_mod, 0, 0))
    const = lambda arr: pl.BlockSpec(arr.shape, lambda i: (0,) * arr.ndim, pipeline_mode=pl.Buffered(1))
    in_specs = [pl.BlockSpec((tm, d), lambda i: (i, 0))]
    args = [y]
    for a, wo in zip(a_list, wo_list):
        in_specs += [pl.BlockSpec((tm, a.shape[1]), lambda i: (i, 0)), const(wo)]
        args += [a, wo]
    in_specs += [mod_spec, pl.BlockSpec((1, d), lambda i: (0, 0)), mod_spec, mod_spec, mod_spec, const(w1), const(w2)]
    args += [gmix, g2, shift, scale, gffn, w1, w2]
    if final_g is not None:
        in_specs.append(pl.BlockSpec((1, d), lambda i: (0, 0)))
        args.append(final_g)
    return pl.pallas_call(
        functools.partial(_post_kernel, n_a=len(a_list), final=final_g is not None),
        out_shape=jax.ShapeDtypeStruct((n, d), F32),
        grid=(n // tm,),
        in_specs=in_specs,
        out_specs=pl.BlockSpec((tm, d), lambda i: (i, 0)),
        compiler_params=_cparams("parallel"),
        name="post_proj_ffn",
    )(*args)


def _pre1_kernel(x_ref, g_ref, sh_ref, sc_ref, w_ref, u_ref):
    h = _modulate(x_ref[...], g_ref[...], sh_ref[0], sc_ref[0]).astype(BF16)
    d = u_ref.shape[1]
    a = _dot(h, w_ref[:, 0:d])
    b = _dot(h, w_ref[:, d:2 * d])
    u_ref[...] = a * _sigmoid(b)


def _pre1(x2, g, shift, scale, pw1, tm, rows_per_mod):
    n, d = x2.shape
    r = shift.shape[1]
    mod_spec = pl.BlockSpec((1, r, d), lambda i: (i // rows_per_mod, 0, 0))
    return pl.pallas_call(
        _pre1_kernel,
        out_shape=jax.ShapeDtypeStruct((n, d), F32),
        grid=(n // tm,),
        in_specs=[
            pl.BlockSpec((tm, d), lambda i: (i, 0)),
            pl.BlockSpec((1, d), lambda i: (0, 0)),
            mod_spec, mod_spec,
            pl.BlockSpec(pw1.shape, lambda i: (0, 0)),
        ],
        out_specs=pl.BlockSpec((tm, d), lambda i: (i, 0)),
        compiler_params=_cparams("parallel"),
        name="pre1_pw_glu",
    )(x2, g, shift, scale, pw1)


def _conv_body(ext_ref, t_rows, dw_ref, dwb_ref, lng_ref, lnb_ref, out_ref):
    off = CONV_PAD - CONV_HIST
    acc = None
    for b in range(8):
        span = t_rows + (8 if b else 0)
        part = None
        for k in range(CONV_WIDTH):
            if (off + k) % 8 != b:
                continue
            base = off + k - b
            term = ext_ref[base:base + span, :] * dw_ref[k:k + 1, :]
            part = term if part is None else part + term
        part = part[b:b + t_rows, :]
        acc = part if acc is None else acc + part
    y = acc + dwb_ref[...]
    mu = jnp.mean(y, axis=-1, keepdims=True)
    var = jnp.mean(jnp.square(y - mu), axis=-1, keepdims=True)
    z = (y - mu) * lax.rsqrt(var + NORM_EPS) * lng_ref[...] + lnb_ref[...]
    out_ref[...] = _silu(z).astype(out_ref.dtype)


def _pre1_conv_kernel(x_ref, g_ref, sh_ref, sc_ref, w_ref, dw_ref, dwb_ref, lng_ref, lnb_ref,
                      out_ref, state_ref, ext_ref, *, tiles_per_batch, tm):
    t_in_b = pl.program_id(0) % tiles_per_batch

    @pl.when(t_in_b == 0)
    def _():
        ext_ref[0:CONV_PAD, :] = jnp.zeros((CONV_PAD, ext_ref.shape[1]), F32)

    @pl.when(t_in_b > 0)
    def _():
        ext_ref[0:CONV_PAD, :] = ext_ref[tm:tm + CONV_PAD, :]

    h = _modulate(x_ref[...], g_ref[...], sh_ref[0], sc_ref[0]).astype(BF16)
    d = out_ref.shape[1]
    a = _dot(h, w_ref[:, 0:d])
    b = _dot(h, w_ref[:, d:2 * d])
    ext_ref[CONV_PAD:, :] = a * _sigmoid(b)
    _conv_body(ext_ref, tm, dw_ref, dwb_ref, lng_ref, lnb_ref, out_ref)
    state_ref[0] = ext_ref[tm:tm + CONV_PAD, :]


def _pre1_conv_prompt(x2, g, shift, scale, pw1, dw, dwb, lng, lnb, seq, tm):
    assert tm >= CONV_PAD
    n, d = x2.shape
    tpb = seq // tm
    mod_spec = pl.BlockSpec((1, 1, d), lambda i: (i // tpb, 0, 0))
    vec = pl.BlockSpec((1, d), lambda i: (0, 0))
    return pl.pallas_call(
        functools.partial(_pre1_conv_kernel, tiles_per_batch=tpb, tm=tm),
        out_shape=(jax.ShapeDtypeStruct((n, d), BF16), jax.ShapeDtypeStruct((n // seq, CONV_PAD, d), F32)),
        grid=(n // tm,),
        in_specs=[
            pl.BlockSpec((tm, d), lambda i: (i, 0)),
            vec, mod_spec, mod_spec,
            pl.BlockSpec(pw1.shape, lambda i: (0, 0)),
            pl.BlockSpec(dw.shape, lambda i: (0, 0)),
            vec, vec, vec,
        ],
        out_specs=(pl.BlockSpec((tm, d), lambda i: (i, 0)),
                   pl.BlockSpec((1, CONV_PAD, d), lambda i: (i // tpb, 0, 0))),
        scratch_shapes=[pltpu.VMEM((CONV_PAD + tm, d), F32)],
        compiler_params=_cparams("arbitrary"),
        name="pre1_conv_prompt",
    )(x2, g, shift, scale, pw1, dw, dwb, lng, lnb)


def _conv_sample_kernel(ext_ref, dw_ref, dwb_ref, lng_ref, lnb_ref, out_ref, *, t_rows):
    _conv_body(ext_ref.at[0], t_rows, dw_ref, dwb_ref, lng_ref, lnb_ref, out_ref.at[0])


def _conv_sample(ext, dw, dwb, lng, lnb):
    b, rows, d = ext.shape
    t_rows = rows - CONV_PAD
    vec = pl.BlockSpec((1, d), lambda i: (0, 0))
    return pl.pallas_call(
        functools.partial(_conv_sample_kernel, t_rows=t_rows),
        out_shape=jax.ShapeDtypeStruct((b, t_rows, d), BF16),
        grid=(b,),
        in_specs=[
            pl.BlockSpec((1, rows, d), lambda i: (i, 0, 0)),
            pl.BlockSpec(dw.shape, lambda i: (0, 0)),
            vec, vec, vec,
        ],
        out_specs=pl.BlockSpec((1, t_rows, d), lambda i: (i, 0, 0)),
        compiler_params=_cparams("parallel"),
        name="conv_sample",
    )(ext, dw, dwb, lng, lnb)


ROW_TILE = 512
ATTN_TILE = 512


def _group_rows(x, nbatch, dec_seq):
    w = x.shape[1] // N_HEADS
    x = x.reshape(nbatch, dec_seq, N_KV, GROUP_R, w).transpose(0, 2, 3, 1, 4)
    x = jnp.pad(x, ((0, 0), (0, 0), (0, 0), (0, 8 - dec_seq), (0, 0)))
    return x.reshape(nbatch, N_KV, GROUP_R * 8, w)


def _ungroup_rows(x, nbatch, dec_seq):
    w = x.shape[3]
    x = x.reshape(nbatch, N_KV, GROUP_R, 8, w)[:, :, :, :dec_seq]
    return x.transpose(0, 3, 1, 2, 4).reshape(nbatch * dec_seq, N_HEADS * w)


def kernel(x_prompt, x_sample, cache_kv, page_table, cache_win, state_pool, state_conv, c_prompt, c_sample,
           ada_w, ada_b, norm_g, attn_w_in, attn_w_out, pool_w, pool_scale, cmp_pos, cmp_w1, cmp_w2,
           conv_pw1, conv_dw, conv_dw_b, conv_ln_g, conv_ln_b, conv_pw2, ffn_w1, ffn_w2, final_g):
    nb_p, seq, d = x_prompt.shape
    nb_s, dec_seq, _ = x_sample.shape
    n_p, n_s = nb_p * seq, nb_s * dec_seq
    n_pages = page_table.shape[1]
    past = n_pages * PAGE_SIZE
    wb = cache_win.shape[3]
    tm = min(ROW_TILE, seq)
    tq = min(ATTN_TILE, seq)
    pool_width = pool_w.shape[1] * pool_w.shape[2]
    att_width = N_HEADS * HEAD_DIM

    mods = _ada(jnp.concatenate([c_prompt, c_sample], axis=0), ada_w, ada_b)

    def mod_p(layer, k):
        return mods[layer, :nb_p, k * d:(k + 1) * d].reshape(nb_p, 1, d)

    def mod_s(layer, k):
        return jnp.repeat(mods[layer, nb_p:, k * d:(k + 1) * d], dec_seq, axis=0).reshape(1, n_s, d)

    y_p = x_prompt.reshape(n_p, d)
    y_s = x_sample.reshape(n_s, d)
    row = lambda v: v.reshape(1, -1)

    w_in = attn_w_in[0]
    w_in_pad = jnp.pad(w_in.T, ((0, (-w_in.shape[1]) % LANES), (0, 0))).astype(BF16)
    cos_p, sin_p = _rope_tables(np.arange(seq))
    cos_s, sin_s = _rope_tables(past + np.arange(n_s) % dec_seq)
    g0 = row(norm_g[0, 0])
    u_p, q_p, gate_p, kvt_p, wint_p, kvbt_p, cmp_in_p = _pre0(
        y_p, g0, mod_p(0, 0), mod_p(0, 1), w_in_pad, cos_p, sin_p, tm, seq // tm, seq // tm, seq=seq)
    u_s, q_s, gate_s, kv_s, win_s = _pre0(y_s, g0, mod_s(0, 0), mod_s(0, 1), w_in_pad, cos_s, sin_s, n_s, 1, 1)

    pool_w_b = pool_w[0].astype(BF16)
    pool_sc = row(pool_scale[0])
    ypool_p = _pool_prompt(u_p, pool_w_b, pool_sc, seq, tm)
    u_s3 = u_s.reshape(nb_s, dec_seq, pool_width)
    pool_ext = jnp.concatenate([
        jnp.zeros((nb_s, POOL_PAD - POOL_HIST, pool_width), F32), state_pool[0], u_s3,
        jnp.zeros((nb_s, 8 - dec_seq, pool_width), F32)], axis=1)
    ypool_s = _pool_sample(pool_ext, pool_w_b, pool_sc, past)[:, :dec_seq].reshape(n_s, pool_width)

    pos_ab, w1_ab, w2_bd = _cmp_weights(cmp_pos[0], cmp_w1[0], cmp_w2[0])
    cmp_in_p4 = cmp_in_p.reshape(2, nb_p, seq, LANES)
    _, cmp_t_p = _cmp_combine([_cmp_rows(cmp_in_p4, pos_ab, w1_ab, nb_p, seq)], w2_bd)
    cache_t = cache_kv[0].transpose(0, 1, 3, 4, 2)
    pt_flat = page_table.reshape(-1)
    part_past = _cmp_pages(cache_t, pt_flat, n_pages, pos_ab, w1_ab, nb_s)
    tail_rows = 2 * PAGE_SIZE
    kv_s4 = kv_s.reshape(4, nb_s, dec_seq, LANES)
    tail4 = jnp.pad(kv_s4, ((0, 0), (0, 0), (0, tail_rows - dec_seq), (0, 0)))
    part_tail = _cmp_rows(tail4.reshape(4, 1, nb_s * tail_rows, LANES), pos_ab, w1_ab, 1, nb_s * tail_rows)
    part_tail = part_tail.reshape(2, nb_s, tail_rows // CMP_STRIDE, part_tail.shape[3])
    cmp_s, _ = _cmp_combine([part_past, part_tail], w2_bd)
    total_len = past + dec_seq
    padded_len = -(-total_len // SEL_BLOCK) * SEL_BLOCK
    n_cmp_s = padded_len // CMP_STRIDE - CMP_BLOCK // CMP_STRIDE + 1
    nblk_s = padded_len // SEL_BLOCK

    assert seq // SEL_BLOCK <= LANES and dec_seq <= SEL_BLOCK
    ncmp_p = cmp_t_p.shape[3]
    mt_p = jnp.asarray(_imp_matrix(ncmp_p, LANES).T, BF16)
    e_mat = jnp.asarray((np.arange(seq)[None, :] // SEL_BLOCK == np.arange(LANES)[:, None]).astype(np.float32),
                        BF16)
    o_p = _attn_prompt(q_p, cmp_t_p, kvbt_p, gate_p, mt_p, e_mat, nb_p, seq, tq)

    qg_s = _group_rows(q_s, nb_s, dec_seq)
    nblk_pad = -(-nblk_s // LANES) * LANES
    m_s = jnp.asarray(_imp_matrix(cmp_s.shape[2], nblk_pad), BF16)
    o_cmp_s, picked = _attn_sample_cmp(qg_s, cmp_s, m_s, n_cmp_s, past, nblk_s)
    idx = picked.reshape(nb_s, N_KV, 8, LANES)[:, :, :dec_seq, :N_SEL].transpose(0, 2, 1, 3).reshape(-1)

    def new_rows_t(rows):
        x = rows.reshape(rows.shape[0], nb_s, dec_seq, N_KV, HEAD_DIM).transpose(0, 1, 3, 4, 2)
        return jnp.pad(x, ((0, 0), (0, 0), (0, 0), (0, 0), (0, LANES - dec_seq)))

    tail_t = new_rows_t(kv_s[2:4])
    wnew_t = new_rows_t(win_s)
    win_t = cache_win[0].transpose(0, 1, 3, 4, 2)
    gates_g = jnp.pad(_group_rows(gate_s[:, :3 * N_HEADS], nb_s, dec_seq), ((0, 0), (0, 0), (0, 0), (0, LANES - 3)))
    o_s = _attn_sample_sel(idx, pt_flat, cache_t, qg_s, tail_t, win_t, wnew_t, o_cmp_s, gates_g,
                           dec_seq, n_pages, past)
    o_s = _ungroup_rows(o_s, nb_s, dec_seq).astype(BF16)

    w_out = attn_w_out[0].astype(BF16)
    wo_list = [w_out[:pool_width], w_out[pool_width:]]
    g1 = row(norm_g[0, 1])
    ffn1_0, ffn2_0 = ffn_w1[0].astype(BF16), ffn_w2[0].astype(BF16)
    y_p = _post(y_p, [ypool_p, o_p], wo_list, mod_p(0, 2), g1, mod_p(0, 3), mod_p(0, 4), mod_p(0, 5),
                ffn1_0, ffn2_0, None, tm, seq // tm)
    y_s = _post(y_s, [ypool_s, o_s], wo_list, mod_s(0, 2), g1, mod_s(0, 3), mod_s(0, 4), mod_s(0, 5),
                ffn1_0, ffn2_0, None, n_s, 1)

    g0 = row(norm_g[1, 0])
    pw1 = conv_pw1[0].astype(BF16)
    dwb, lng, lnb = row(conv_dw_b[0]), row(conv_ln_g[0]), row(conv_ln_b[0])
    cv_p, uc_tail_p = _pre1_conv_prompt(y_p, g0, mod_p(1, 0), mod_p(1, 1), pw1, conv_dw[0], dwb, lng, lnb, seq, tm)
    uc_s = _pre1(y_s, g0, mod_s(1, 0), mod_s(1, 1), pw1, n_s, 1)
    uc_s3 = uc_s.reshape(nb_s, dec_seq, d)
    conv_ext = jnp.concatenate([
        jnp.zeros((nb_s, CONV_PAD - CONV_HIST, d), F32), state_conv[0], uc_s3,
        jnp.zeros((nb_s, 8 - dec_seq, d), F32)], axis=1)
    cv_s = _conv_sample(conv_ext, conv_dw[0], dwb, lng, lnb)[:, :dec_seq].reshape(n_s, d)

    pw2 = conv_pw2[0].astype(BF16)
    g1 = row(norm_g[1, 1])
    ffn1_1, ffn2_1 = ffn_w1[1].astype(BF16), ffn_w2[1].astype(BF16)
    fg = row(final_g)
    y_p = _post(y_p, [cv_p], [pw2], mod_p(1, 2), g1, mod_p(1, 3), mod_p(1, 4), mod_p(1, 5),
                ffn1_1, ffn2_1, fg, tm, seq // tm)
    y_s = _post(y_s, [cv_s], [pw2], mod_s(1, 2), g1, mod_s(1, 3), mod_s(1, 4), mod_s(1, 5),
                ffn1_1, ffn2_1, fg, n_s, 1)

    y_prompt = y_p.reshape(nb_p, seq, d)
    y_sample = y_s.reshape(nb_s, dec_seq, d)
    kv_prompt = kvt_p.transpose(0, 1, 4, 2, 3)[None]
    kv_sample = kv_s.reshape(1, 4, nb_s, dec_seq, N_KV, HEAD_DIM)
    if seq >= wb:
        win_prompt_t = wint_p[..., seq - wb:]
    else:
        win_prompt_t = jnp.pad(wint_p, ((0, 0),) * 4 + ((wb - seq, 0),))
    win_prompt = win_prompt_t.transpose(0, 1, 4, 2, 3)
    win_sample_t = jnp.concatenate([win_t, wnew_t[..., :dec_seq]], axis=-1)[..., -wb:]
    win_sample = win_sample_t.transpose(0, 1, 4, 2, 3)

    def last_rows(x3, hist, state):
        full = x3 if state is None and x3.shape[1] >= hist else jnp.concatenate(
            [jnp.zeros((x3.shape[0], hist, x3.shape[2]), F32) if state is None else state, x3], axis=1)
        return full[:, full.shape[1] - hist:]

    pool_prompt = last_rows(u_p.reshape(nb_p, seq, pool_width), POOL_HIST, None)
    pool_sample = last_rows(u_s3, POOL_HIST, state_pool[0])
    conv_prompt = uc_tail_p[:, CONV_PAD - CONV_HIST:]
    conv_sample = last_rows(uc_s3, CONV_HIST, state_conv[0])
    return (y_prompt, y_sample, kv_prompt, kv_sample, win_prompt[None], win_sample[None],
            pool_prompt[None], pool_sample[None], conv_prompt[None], conv_sample[None])
```
